```python
import math
import jax, jax.numpy as jnp
from jax import lax
import numpy as np

D_MODEL = 1024
BATCH = 4
SEQ = 8192
DEPTH = 2

GRID_W = 64
CTX_LEN = 256
N_MIXERS = 2
N_MOD = 6
NORM_EPS = 1e-6

NA_HEADS = 16
NA_HEAD_DIM = D_MODEL // NA_HEADS
NA_WIN_H = 8
NA_WIN_W = 16

HY_FILTER_WIDTH = 64
HY_EMB_DIM = 33
HY_BANDS = (HY_EMB_DIM - 1) // 2
HY_SHORT_WIDTH = 3
HY_DECAY_TARGET = 1e-2
HY_SHORT_DECAY_PCT = 0.3
HY_LONG_DECAY_PCT = 1.5

D_FF = 2816
FFN_CONV_WIDTH = 3

kernel_name = 'hybrid_na_hyena_diffusion_block'


def rms_norm(x, gain):
    xf = x.astype(jnp.float32)
    y = xf * lax.rsqrt(jnp.mean(xf * xf, axis=-1, keepdims=True) + NORM_EPS)
    return (y * gain.astype(jnp.float32)).astype(x.dtype)


def adaln(cond, w_mod, b_mod):
    return jnp.split(jax.nn.silu(cond) @ w_mod + b_mod, N_MOD, axis=-1)


def modulate(h, shift, scale):
    return h * (1.0 + scale) + shift


def dwconv_centred(x, w, b):
    k = w.shape[0]
    L = x.shape[1]
    pad = k // 2
    xp = jnp.pad(x, ((0, 0), (pad, k - 1 - pad), (0, 0)))
    out = xp[:, 0:L] * w[0]
    for i in range(1, k):
        out = out + xp[:, i:i + L] * w[i]
    return out + b


def split_heads(t):
    return t.reshape(t.shape[0], t.shape[1], NA_HEADS, NA_HEAD_DIM)


def neighbourhood_attention(q, k, v, k_ctx, v_ctx, rpb):
    B, L, H, Dh = q.shape
    rows = L // GRID_W
    kh = min(NA_WIN_H, rows)
    kw = NA_WIN_W
    scale = Dh ** -0.5
    cols = np.arange(GRID_W)
    col_idx = (np.clip(cols - kw // 2, 0, GRID_W - kw)[:, None] + np.arange(kw)[None, :]).astype(np.int32)
    dc_idx = (col_idx - cols[:, None] + (NA_WIN_W - 1)).astype(np.int32)
    qg = jnp.moveaxis(q.reshape(B, rows, GRID_W, H, Dh), 1, 0)
    kg = k.reshape(B, rows, GRID_W, H, Dh)
    vg = v.reshape(B, rows, GRID_W, H, Dh)

    def one_row(args):
        r, q_row = args
        r0 = jnp.clip(r - kh // 2, 0, rows - kh)
        k_win = lax.dynamic_slice_in_dim(kg, r0, kh, axis=1)[:, :, col_idx]
        v_win = lax.dynamic_slice_in_dim(vg, r0, kh, axis=1)[:, :, col_idx]
        dr_idx = r0 + jnp.arange(kh, dtype=jnp.int32) - r + (NA_WIN_H - 1)
        bias = rpb[:, dr_idx[:, None, None], dc_idx[None, :, :]]
        s_loc = jnp.einsum('bqhd,biqjhd->bhqij', q_row, k_win) * scale + jnp.transpose(bias, (0, 2, 1, 3))[None]
        s_ctx = jnp.einsum('bqhd,bkhd->bhqk', q_row, k_ctx) * scale
        s = jnp.concatenate([s_loc.reshape(B, H, GRID_W, kh * kw), s_ctx], axis=-1)
        p = jax.nn.softmax(s.astype(jnp.float32), axis=-1).astype(v.dtype)
        p_loc = p[..., :kh * kw].reshape(B, H, GRID_W, kh, kw)
        return (jnp.einsum('bhqij,biqjhd->bqhd', p_loc, v_win)
                + jnp.einsum('bhqk,bkhd->bqhd', p[..., kh * kw:], v_ctx))

    out = lax.map(one_row, (jnp.arange(rows, dtype=jnp.int32), qg))
    return jnp.moveaxis(out, 0, 1).reshape(B, L, H * Dh)


def context_attention(q, k, v):
    s = jnp.einsum('bqhd,bkhd->bhqk', q, k) * (q.shape[-1] ** -0.5)
    p = jax.nn.softmax(s.astype(jnp.float32), axis=-1).astype(v.dtype)
    o = jnp.einsum('bhqk,bkhd->bqhd', p, v)
    return o.reshape(o.shape[0], o.shape[1], -1)


def hyena_filters(L, w1, b1, w2, b2, w3, b3, w_out, sin_freq):
    f32 = jnp.float32
    t = jnp.linspace(0.0, 1.0, L, dtype=f32)[:, None]
    w = (2.0 * math.pi / L) * jnp.arange(L, dtype=f32)[:, None]
    bands = jnp.linspace(1e-4, HY_BANDS - 1, HY_BANDS, dtype=f32)[None, :]
    z = jnp.concatenate([t, jnp.cos(bands * w), -jnp.sin(bands * w)], axis=-1)
    freq = sin_freq.astype(f32)
    hdn = jnp.sin(freq * (z @ w1.astype(f32) + b1.astype(f32)))
    hdn = jnp.sin(freq * (hdn @ w2.astype(f32) + b2.astype(f32)))
    hdn = jnp.sin(freq * (hdn @ w3.astype(f32) + b3.astype(f32)))
    h = (hdn @ w_out.astype(f32)).reshape(L, 2, D_MODEL)
    deltas = jnp.abs(jnp.linspace(math.log(HY_DECAY_TARGET) / HY_SHORT_DECAY_PCT,
                                  math.log(HY_DECAY_TARGET) / HY_LONG_DECAY_PCT, D_MODEL, dtype=f32))
    h = h * jnp.exp(-t * deltas)[:, None, :]
    h = h * lax.rsqrt(jnp.sum(h * h, axis=(0, 1), keepdims=True) + NORM_EPS)
    return jnp.concatenate([h[:, 0], jnp.zeros((1, D_MODEL), f32), h[:0:-1, 1]], axis=0)


def hyena_mixer(h, w_in, b_in, short_w, short_b, f_w1, f_b1, f_w2, f_b2, f_w3, f_b3, f_wout, f_freq,
                d_bias, w_out, b_out):
    L = h.shape[1]
    u = dwconv_centred(h @ w_in + b_in, short_w, short_b)
    x0, x1, v = jnp.split(u, 3, axis=-1)
    filt = hyena_filters(L, f_w1, f_b1, f_w2, f_b2, f_w3, f_b3, f_wout, f_freq)
    z = (v * x1).astype(jnp.float32)
    zf = jnp.fft.rfft(z, n=2 * L, axis=1)
    ff = jnp.fft.rfft(filt, n=2 * L, axis=0)
    y = jnp.fft.irfft(zf * ff[None], n=2 * L, axis=1)[:, :L] + z * d_bias.astype(jnp.float32)
    return (x0 * y.astype(h.dtype)) @ w_out + b_out


def conv_ffn(h, w_up, conv_w, conv_b, w_down):
    a, g = jnp.split(h @ w_up, 2, axis=-1)
    return (a * jax.nn.gelu(dwconv_centred(g, conv_w, conv_b), approximate=False)) @ w_down


def setup_inputs(seed: int = 0) -> dict:
    key = jax.random.key(seed)
    ks = iter(jax.random.split(key, 48))

    def nrm(shape, scale):
        return scale * jax.random.normal(next(ks), shape, jnp.float32)

    D = D_MODEL
    inp = {}
    inp['x'] = nrm((BATCH, SEQ, D), 1.0)
    inp['c'] = nrm((BATCH, D), 1.0)
    inp['ctx'] = nrm((BATCH, CTX_LEN, D), 1.0)
    inp['c_ctx'] = nrm((D,), 1.0)
    inp['l0_w_mod'] = nrm((D, N_MOD * D), D ** -0.5)
    inp['l0_b_mod'] = nrm((N_MOD * D,), 0.02)
    inp['l0_norm1'] = 1.0 + nrm((D,), 0.05)
    inp['l0_norm2'] = 1.0 + nrm((D,), 0.05)
    inp['l0_na_w_qkv'] = nrm((D, 3 * D), D ** -0.5)
    inp['l0_na_q_gain'] = 1.0 + nrm((NA_HEAD_DIM,), 0.05)
    inp['l0_na_k_gain'] = 1.0 + nrm((NA_HEAD_DIM,), 0.05)
    inp['l0_na_rpb'] = nrm((NA_HEADS, 2 * NA_WIN_H - 1, 2 * NA_WIN_W - 1), 0.1)
    inp['l0_na_w_o'] = nrm((D, D), D ** -0.5)
    inp['l0_ffn_w_up'] = nrm((D, 2 * D_FF), D ** -0.5)
    inp['l0_ffn_conv_w'] = nrm((FFN_CONV_WIDTH, D_FF), FFN_CONV_WIDTH ** -0.5)
    inp['l0_ffn_conv_b'] = nrm((D_FF,), 0.02)
    inp['l0_ffn_w_down'] = nrm((D_FF, D), D_FF ** -0.5)
    inp['l1_w_mod'] = nrm((D, N_MOD * D), D ** -0.5)
    inp['l1_b_mod'] = nrm((N_MOD * D,), 0.02)
    inp['l1_norm1'] = 1.0 + nrm((D,), 0.05)
    inp['l1_norm2'] = 1.0 + nrm((D,), 0.05)
    inp['l1_hy_w_in'] = nrm((D, 3 * D), D ** -0.5)
    inp['l1_hy_b_in'] = nrm((3 * D,), 0.02)
    inp['l1_hy_short_w'] = nrm((HY_SHORT_WIDTH, 3 * D), HY_SHORT_WIDTH ** -0.5)
    inp['l1_hy_short_b'] = nrm((3 * D,), 0.02)
    inp['l1_hy_f_w1'] = nrm((HY_EMB_DIM, HY_FILTER_WIDTH), HY_EMB_DIM ** -0.5)
    inp['l1_hy_f_b1'] = nrm((HY_FILTER_WIDTH,), 0.1)
    inp['l1_hy_f_w2'] = nrm((HY_FILTER_WIDTH, HY_FILTER_WIDTH), HY_FILTER_WIDTH ** -0.5)
    inp['l1_hy_f_b2'] = nrm((HY_FILTER_WIDTH,), 0.1)
    inp['l1_hy_f_w3'] = nrm((HY_FILTER_WIDTH, HY_FILTER_WIDTH), HY_FILTER_WIDTH ** -0.5)
    inp['l1_hy_f_b3'] = nrm((HY_FILTER_WIDTH,), 0.1)
    inp['l1_hy_f_wout'] = nrm((HY_FILTER_WIDTH, 2 * D), HY_FILTER_WIDTH ** -0.5)
    inp['l1_hy_f_freq'] = 1.0 + nrm((HY_FILTER_WIDTH,), 0.05)
    inp['l1_hy_d_bias'] = nrm((D,), 1.0)
    inp['l1_hy_w_out'] = nrm((D, D), D ** -0.5)
    inp['l1_hy_b_out'] = nrm((D,), 0.02)
    inp['l1_ffn_w_up'] = nrm((D, 2 * D_FF), D ** -0.5)
    inp['l1_ffn_conv_w'] = nrm((FFN_CONV_WIDTH, D_FF), FFN_CONV_WIDTH ** -0.5)
    inp['l1_ffn_conv_b'] = nrm((D_FF,), 0.02)
    inp['l1_ffn_w_down'] = nrm((D_FF, D), D_FF ** -0.5)
    return inp


def reference(x, c, ctx, c_ctx,
              l0_w_mod, l0_b_mod, l0_norm1, l0_norm2, l0_na_w_qkv, l0_na_q_gain, l0_na_k_gain, l0_na_rpb,
              l0_na_w_o, l0_ffn_w_up, l0_ffn_conv_w, l0_ffn_conv_b, l0_ffn_w_down,
              l1_w_mod, l1_b_mod, l1_norm1, l1_norm2, l1_hy_w_in, l1_hy_b_in, l1_hy_short_w, l1_hy_short_b,
              l1_hy_f_w1, l1_hy_f_b1, l1_hy_f_w2, l1_hy_f_b2, l1_hy_f_w3, l1_hy_f_b3, l1_hy_f_wout, l1_hy_f_freq,
              l1_hy_d_bias, l1_hy_w_out, l1_hy_b_out, l1_ffn_w_up, l1_ffn_conv_w, l1_ffn_conv_b, l1_ffn_w_down):
    layers = [
        dict(w_mod=l0_w_mod, b_mod=l0_b_mod, norm1=l0_norm1, norm2=l0_norm2,
             mixer=(l0_na_w_qkv, l0_na_q_gain, l0_na_k_gain, l0_na_rpb, l0_na_w_o),
             ffn=(l0_ffn_w_up, l0_ffn_conv_w, l0_ffn_conv_b, l0_ffn_w_down)),
        dict(w_mod=l1_w_mod, b_mod=l1_b_mod, norm1=l1_norm1, norm2=l1_norm2,
             mixer=(l1_hy_w_in, l1_hy_b_in, l1_hy_short_w, l1_hy_short_b, l1_hy_f_w1, l1_hy_f_b1,
                    l1_hy_f_w2, l1_hy_f_b2, l1_hy_f_w3, l1_hy_f_b3, l1_hy_f_wout, l1_hy_f_freq,
                    l1_hy_d_bias, l1_hy_w_out, l1_hy_b_out),
             ffn=(l1_ffn_w_up, l1_ffn_conv_w, l1_ffn_conv_b, l1_ffn_w_down)),
    ]
    xc = ctx
    for i in range(DEPTH):
        p = layers[i]
        ctx_read_later = any(j % N_MIXERS == 0 for j in range(i + 1, DEPTH))
        sh1, sc1, g1, sh2, sc2, g2 = adaln(c[:, None, :], p['w_mod'], p['b_mod'])
        csh1, csc1, cg1, csh2, csc2, cg2 = adaln(c_ctx[None, None, :], p['w_mod'], p['b_mod'])
        h = modulate(rms_norm(x, p['norm1']), sh1, sc1)
        hc = modulate(rms_norm(xc, p['norm1']), csh1, csc1)
        if i % N_MIXERS == 0:
            w_qkv, q_gain, k_gain, rpb, w_o = p['mixer']
            w_q, w_k, w_v = jnp.split(w_qkv, 3, axis=1)
            q = rms_norm(split_heads(h @ w_q), q_gain)
            k = rms_norm(split_heads(h @ w_k), k_gain)
            v = split_heads(h @ w_v)
            kc = rms_norm(split_heads(hc @ w_k), k_gain)
            vc = split_heads(hc @ w_v)
            y = neighbourhood_attention(q, k, v, kc, vc, rpb) @ w_o
            if ctx_read_later:
                qc = rms_norm(split_heads(hc @ w_q), q_gain)
                yc = context_attention(qc, kc, vc) @ w_o
        else:
            y = hyena_mixer(h, *p['mixer'])
            if ctx_read_later:
                yc = hyena_mixer(hc, *p['mixer'])
        x = x + g1 * y
        x = x + g2 * conv_ffn(modulate(rms_norm(x, p['norm2']), sh2, sc2), *p['ffn'])
        if ctx_read_later:
            xc = xc + cg1 * yc
            xc = xc + cg2 * conv_ffn(modulate(rms_norm(xc, p['norm2']), csh2, csc2), *p['ffn'])
    return x
```

```python
import functools
import math

import numpy as np
import jax
import jax.numpy as jnp
from jax import lax
from jax.experimental import pallas as pl
from jax.experimental.pallas import tpu as pltpu

F32 = jnp.float32
BF16 = jnp.bfloat16
HIGHEST = lax.Precision.HIGHEST

NORM_EPS = 1e-6
N_MOD = 6
HEAD_DIM = 64
GRID_W = 64
WIN_H = 8
WIN_W = 16
HY_BANDS = 16
HY_DECAY_TARGET = 1e-2
HY_SHORT_DECAY_PCT = 0.3
HY_LONG_DECAY_PCT = 1.5

LANES = 128
HALO = 16
Q_ROWS = 8
K_ROWS = 16
NEG = -1e30
VMEM_LIMIT = 56 * 1024 * 1024


def _params(sem):
    return pltpu.CompilerParams(dimension_semantics=sem, vmem_limit_bytes=VMEM_LIMIT)


def _const_spec(shape):
    return pl.BlockSpec(shape, lambda *_: (0,) * len(shape), pipeline_mode=pl.Buffered(1))


def _rms_mod(x, gain, shift, scale):
    ms = jnp.mean(x * x, axis=-1, keepdims=True)
    y = x * lax.rsqrt(ms + NORM_EPS) * gain
    return y * (1.0 + scale) + shift


def _adaln_kernel(c_ref, w_ref, b_ref, o_ref):
    c = c_ref[...]
    s = c / (1.0 + jnp.exp(-c))
    o_ref[...] = jnp.dot(s, w_ref[...], preferred_element_type=F32, precision=HIGHEST) + b_ref[...]


def _adaln(cond, w_mod, b_mod):
    rows, d = cond.shape
    n = w_mod.shape[1]
    tn = d
    return pl.pallas_call(
        _adaln_kernel,
        out_shape=jax.ShapeDtypeStruct((rows, n), F32),
        grid=(n // tn,),
        in_specs=[
            pl.BlockSpec((rows, d), lambda j: (0, 0)),
            pl.BlockSpec((d, tn), lambda j: (0, j)),
            pl.BlockSpec((1, tn), lambda j: (0, j)),
        ],
        out_specs=pl.BlockSpec((rows, tn), lambda j: (0, j)),
        compiler_params=_params(("arbitrary",)),
        name="adaln",
    )(cond, w_mod, b_mod.reshape(1, n))


def _qkv_kernel(x_ref, sh_ref, sc_ref, g_ref, w_ref, qg_ref, kg_ref, p1_ref, p2_ref, o_ref):
    d = x_ref.shape[-1]
    h = _rms_mod(x_ref[0], g_ref[...], sh_ref[0], sc_ref[0]).astype(BF16)
    for j, gain_ref in ((0, qg_ref), (1, kg_ref)):
        t = jnp.dot(h, w_ref[:, j * d:(j + 1) * d], preferred_element_type=F32)
        ms = jnp.dot((t * t).astype(BF16), p1_ref[...], preferred_element_type=F32)
        r = lax.rsqrt(ms + NORM_EPS)
        r_hi = r.astype(BF16)
        r_lo = (r - r_hi.astype(F32)).astype(BF16)
        rr = jnp.dot(jnp.concatenate([r_hi, r_lo], axis=-1), p2_ref[...], preferred_element_type=F32)
        o_ref[0, :, j * d:(j + 1) * d] = (t * rr * gain_ref[...]).astype(BF16)
    v = jnp.dot(h, w_ref[:, 2 * d:3 * d], preferred_element_type=F32)
    o_ref[0, :, 2 * d:3 * d] = v.astype(BF16)


def _qkv(x, shift, scale, gain, w_qkv, q_gain, k_gain, tm):
    b, l, d = x.shape
    heads = d // HEAD_DIM
    per_batch = shift.shape[0] == b
    mod_map = (lambda bi, i: (bi, 0, 0)) if per_batch else (lambda bi, i: (0, 0, 0))
    p1 = np.zeros((d, LANES), np.float32)
    p1[np.arange(d), np.arange(d) // HEAD_DIM] = 1.0 / HEAD_DIM
    p2 = np.zeros((2 * LANES, d), np.float32)
    p2[np.arange(d) // HEAD_DIM, np.arange(d)] = 1.0
    p2[LANES + np.arange(d) // HEAD_DIM, np.arange(d)] = 1.0
    qg = (jnp.tile(q_gain, heads) * (HEAD_DIM ** -0.5)).reshape(1, d)
    kg = jnp.tile(k_gain, heads).reshape(1, d)
    return pl.pallas_call(
        _qkv_kernel,
        out_shape=jax.ShapeDtypeStruct((b, l, 3 * d), BF16),
        grid=(b, l // tm),
        in_specs=[
            pl.BlockSpec((1, tm, d), lambda bi, i: (bi, i, 0)),
            pl.BlockSpec((1, 1, d), mod_map),
            pl.BlockSpec((1, 1, d), mod_map),
            _const_spec((1, d)),
            _const_spec((d, 3 * d)),
            _const_spec((1, d)),
            _const_spec((1, d)),
            _const_spec((d, LANES)),
            _const_spec((2 * LANES, d)),
        ],
        out_specs=pl.BlockSpec((1, tm, 3 * d), lambda bi, i: (bi, i, 0)),
        compiler_params=_params(("parallel", "parallel")),
        name="qkv",
    )(x, shift, scale, gain.reshape(1, d), w_qkv.astype(BF16), qg, kg,
      jnp.asarray(p1, BF16), jnp.asarray(p2, BF16))


def _toeplitz_kernel(r_ref, oh_ref, m_ref, o_ref):
    o_ref[...] = jnp.dot(r_ref[...], oh_ref[...], preferred_element_type=F32, precision=HIGHEST) + m_ref[...]


def _col_bias(rpb):
    heads, n_dr, n_dc = rpb.shape
    k_pad = 32
    qc = np.arange(GRID_W)[:, None]
    kc = np.arange(GRID_W)[None, :]
    c0 = np.clip(qc - WIN_W // 2, 0, GRID_W - WIN_W)
    valid = (kc >= c0) & (kc < c0 + WIN_W)
    dc = kc - qc + (WIN_W - 1)
    onehot = np.zeros((k_pad, GRID_W, GRID_W), np.float32)
    for j in range(n_dc):
        onehot[j] = ((dc == j) & valid).astype(np.float32)
    onehot = onehot.reshape(k_pad, GRID_W * GRID_W)
    mask = np.where(valid, 0.0, NEG).astype(np.float32).reshape(1, GRID_W * GRID_W)
    rows = heads * n_dr
    rows_pad = -(-rows // 8) * 8
    r2 = jnp.zeros((rows_pad, k_pad), F32).at[:rows, :n_dc].set(rpb.reshape(rows, n_dc))
    t = pl.pallas_call(
        _toeplitz_kernel,
        out_shape=jax.ShapeDtypeStruct((rows_pad, GRID_W * GRID_W), F32),
        name="rpb_toeplitz",
    )(r2, jnp.asarray(onehot), jnp.asarray(mask))
    return t[:rows].reshape(heads, n_dr, GRID_W, GRID_W)


def _block_bias(t, n_rows):
    heads = t.shape[0]
    kh = min(WIN_H, n_rows)
    masked = jnp.full((heads, GRID_W, GRID_W), NEG, F32)
    classes = []
    for q0, k0 in ((0, 0), (Q_ROWS, Q_ROWS - WIN_H // 2), (n_rows - Q_ROWS, n_rows - K_ROWS)):
        strips = []
        for qr in range(Q_ROWS):
            r = q0 + qr
            r0 = min(max(r - kh // 2, 0), n_rows - kh)
            blocks = []
            for kr in range(K_ROWS):
                kk = k0 + kr
                blocks.append(t[:, kk - r + WIN_H - 1] if r0 <= kk < r0 + kh else masked)
            strips.append(jnp.concatenate(blocks, axis=-1))
        classes.append(jnp.concatenate(strips, axis=-2))
    return jnp.stack(classes).astype(BF16)


def _na_kernel(q_ref, k_ref, v_ref, kc_ref, vc_ref, bias_ref, o_ref):
    i = pl.program_id(2)
    tq = q_ref.shape[1]
    tk = bias_ref.shape[-1]
    l = k_ref.shape[1]
    start = pl.multiple_of(jnp.clip(i * tq - (tk - tq) // 2, 0, l - tk), tq // 2)
    q = q_ref[0]
    kw = k_ref[0, pl.ds(start, tk), :]
    vw = v_ref[0, pl.ds(start, tk), :]
    kc = kc_ref[0]
    vc = vc_ref[0]
    first_head = lax.broadcasted_iota(jnp.int32, (1, LANES), 1) < HEAD_DIM
    nt = (((1,), (1,)), ((), ()))
    outs = []
    for h in range(2):
        qh = jnp.where(first_head if h == 0 else jnp.logical_not(first_head), q, jnp.zeros_like(q))
        s_loc = lax.dot_general(qh, kw, nt, preferred_element_type=F32) + bias_ref[0, h].astype(F32)
        s_ctx = lax.dot_general(qh, kc, nt, preferred_element_type=F32)
        m = jnp.maximum(jnp.max(s_loc, axis=-1, keepdims=True), jnp.max(s_ctx, axis=-1, keepdims=True))
        p_loc = jnp.exp(s_loc - m)
        p_ctx = jnp.exp(s_ctx - m)
        denom = jnp.sum(p_loc, axis=-1, keepdims=True) + jnp.sum(p_ctx, axis=-1, keepdims=True)
        o = (jnp.dot(p_loc.astype(BF16), vw, preferred_element_type=F32)
             + jnp.dot(p_ctx.astype(BF16), vc, preferred_element_type=F32))
        outs.append(o / denom)
    o_ref[0] = jnp.where(first_head, outs[0], outs[1]).astype(BF16)


def _neighbourhood_attention(qkv, qkv_ctx, bias):
    b, l, d3 = qkv.shape
    d = d3 // 3
    n_ctx = qkv_ctx.shape[1]
    pairs = d // LANES
    tq, tk = bias.shape[-2:]
    nb = l // tq

    def bias_map(bi, hp, i):
        return (jnp.where(i == 0, 0, jnp.where(i == nb - 1, 2, 1)), hp, 0, 0)

    return pl.pallas_call(
        _na_kernel,
        out_shape=jax.ShapeDtypeStruct((b, l, d), BF16),
        grid=(b, pairs, nb),
        in_specs=[
            pl.BlockSpec((1, tq, LANES), lambda bi, hp, i: (bi, i, hp)),
            pl.BlockSpec((1, l, LANES), lambda bi, hp, i: (bi, 0, pairs + hp)),
            pl.BlockSpec((1, l, LANES), lambda bi, hp, i: (bi, 0, 2 * pairs + hp)),
            pl.BlockSpec((1, n_ctx, LANES), lambda bi, hp, i: (bi, 0, pairs + hp)),
            pl.BlockSpec((1, n_ctx, LANES), lambda bi, hp, i: (bi, 0, 2 * pairs + hp)),
            pl.BlockSpec((1, 2, tq, tk), bias_map),
        ],
        out_specs=pl.BlockSpec((1, tq, LANES), lambda bi, hp, i: (bi, i, hp)),
        compiler_params=_params(("parallel", "parallel", "arbitrary")),
        name="na_attention",
    )(qkv, qkv, qkv, qkv_ctx, qkv_ctx, bias)


def _halo_rows(prev_ref, main_ref, next_ref):
    return jnp.concatenate([prev_ref[0], main_ref[0], next_ref[0]], axis=0)


def _edge_mask(rows, axis):
    i = pl.program_id(axis)
    r = lax.broadcasted_iota(jnp.int32, (rows, 1), 0)
    lo = jnp.where(i == 0, HALO, 0)
    hi = jnp.where(i == pl.num_programs(axis) - 1, rows - HALO, rows)
    return jnp.where((r >= lo) & (r < hi), 1.0, 0.0).astype(F32)


def _shift_rows(g, rows):
    return pltpu.roll(g, 1, 0), pltpu.roll(g, rows - 1, 0)


def _erf(x):
    return lax.erf(x)


def _post_kernel(mp_ref, mm_ref, mn_ref, xp_ref, xm_ref, xn_ref, wmix_ref, bmix_ref, g1_ref, n2_ref,
                 sh_ref, sc_ref, g2_ref, wup_ref, cw_ref, cb_ref, wdn_ref, o_ref, *, ff_chunk):
    rows = mm_ref.shape[1] + 2 * HALO
    d_ff = wdn_ref.shape[0]
    mix = _halo_rows(mp_ref, mm_ref, mn_ref)
    x = _halo_rows(xp_ref, xm_ref, xn_ref)
    y = jnp.dot(mix, wmix_ref[...], preferred_element_type=F32) + bmix_ref[...]
    x1 = x + g1_ref[0] * y
    h = _rms_mod(x1, n2_ref[...], sh_ref[0], sc_ref[0]).astype(BF16)
    edge = _edge_mask(rows, 1)
    acc = jnp.zeros((rows - 2 * HALO, x.shape[-1]), F32)
    for c in range(0, d_ff, ff_chunk):
        a = jnp.dot(h, wup_ref[:, c:c + ff_chunk], preferred_element_type=F32)
        g = jnp.dot(h, wup_ref[:, d_ff + c:d_ff + c + ff_chunk], preferred_element_type=F32) * edge
        g_prev, g_next = _shift_rows(g, rows)
        cw = cw_ref[:, c:c + ff_chunk]
        gc = g_prev * cw[0:1] + g * cw[1:2] + g_next * cw[2:3] + cb_ref[:, c:c + ff_chunk]
        gc = gc[HALO:rows - HALO]
        u = a[HALO:rows - HALO] * (0.5 * gc * (1.0 + _erf(gc * (2.0 ** -0.5))))
        acc = acc + jnp.dot(u.astype(BF16), wdn_ref[c:c + ff_chunk, :], preferred_element_type=F32)
    o_ref[0] = x1[HALO:rows - HALO] + g2_ref[0] * acc


def _halo_specs(tm, l, d):
    nblk = tm // HALO
    last = l // HALO - 1
    return [
        pl.BlockSpec((1, HALO, d), lambda bi, i: (bi, jnp.maximum(i * nblk - 1, 0), 0)),
        pl.BlockSpec((1, tm, d), lambda bi, i: (bi, i, 0)),
        pl.BlockSpec((1, HALO, d), lambda bi, i: (bi, jnp.minimum((i + 1) * nblk, last), 0)),
    ]


def _post(mix, x, w_mix, b_mix, g1, norm2, sh2, sc2, g2, w_up, conv_w, conv_b, w_down, tm, ff_chunk):
    b, l, d = x.shape
    d_ff = w_down.shape[0]
    mod = pl.BlockSpec((1, 1, d), lambda bi, i: (bi, 0, 0))
    return pl.pallas_call(
        functools.partial(_post_kernel, ff_chunk=ff_chunk),
        out_shape=jax.ShapeDtypeStruct((b, l, d), F32),
        grid=(b, l // tm),
        in_specs=_halo_specs(tm, l, d) + _halo_specs(tm, l, d) + [
            _const_spec((d, d)), _const_spec((1, d)), mod, _const_spec((1, d)), mod, mod, mod,
            _const_spec((d, 2 * d_ff)), _const_spec((3, d_ff)), _const_spec((1, d_ff)), _const_spec((d_ff, d)),
        ],
        out_specs=pl.BlockSpec((1, tm, d), lambda bi, i: (bi, i, 0)),
        compiler_params=_params(("parallel", "parallel")),
        name="post_ffn",
    )(mix, mix, mix, x, x, x, w_mix.astype(BF16), b_mix.reshape(1, d), g1, norm2.reshape(1, d), sh2, sc2, g2,
      w_up.astype(BF16), conv_w, conv_b.reshape(1, d_ff), w_down.astype(BF16))


def _hy_in_kernel(xp_ref, xm_ref, xn_ref, sh_ref, sc_ref, g_ref, w_ref, b_ref, cw_ref, cb_ref, x0_ref, z_ref):
    rows = xm_ref.shape[1] + 2 * HALO
    d = xm_ref.shape[-1]
    h = _rms_mod(_halo_rows(xp_ref, xm_ref, xn_ref), g_ref[...], sh_ref[0], sc_ref[0]).astype(BF16)
    edge = _edge_mask(rows, 1)
    parts = []
    for j in range(3):
        cols = slice(j * d, (j + 1) * d)
        u = (jnp.dot(h, w_ref[:, cols], preferred_element_type=F32) + b_ref[:, cols]) * edge
        u_prev, u_next = _shift_rows(u, rows)
        cw = cw_ref[:, cols]
        uc = u_prev * cw[0:1] + u * cw[1:2] + u_next * cw[2:3] + cb_ref[:, cols]
        parts.append(uc[HALO:rows - HALO])
    x0_ref[0] = parts[0].astype(BF16)
    z_ref[0] = (parts[2] * parts[1]).astype(BF16)


def _hy_in(x, shift, scale, gain, w_in, b_in, short_w, short_b, tm):
    b, l, d = x.shape
    mod = pl.BlockSpec((1, 1, d), lambda bi, i: (bi, 0, 0))
    out = pl.BlockSpec((1, tm, d), lambda bi, i: (bi, i, 0))
    return pl.pallas_call(
        _hy_in_kernel,
        out_shape=(jax.ShapeDtypeStruct((b, l, d), BF16), jax.ShapeDtypeStruct((b, l, d), BF16)),
        grid=(b, l // tm),
        in_specs=_halo_specs(tm, l, d) + [
            mod, mod, _const_spec((1, d)), _const_spec((d, 3 * d)), _const_spec((1, 3 * d)),
            _const_spec((3, 3 * d)), _const_spec((1, 3 * d)),
        ],
        out_specs=(out, out),
        compiler_params=_params(("parallel", "parallel")),
        name="hyena_in",
    )(x, x, x, shift, scale, gain.reshape(1, d), w_in.astype(BF16), b_in.reshape(1, 3 * d),
      short_w, short_b.reshape(1, 3 * d))


def _filter_kernel(bands_ref, w1t_ref, w1c_ref, w1s_ref, b1_ref, w2_ref, b2_ref, w3_ref, b3_ref, wo_ref,
                   freq_ref, delta_ref, hf_ref, hb_ref, ss_ref, *, seq_len):
    tl = hf_ref.shape[0]
    d = hf_ref.shape[1]
    pos = (pl.program_id(0) * tl + lax.broadcasted_iota(jnp.int32, (tl, 1), 0)).astype(F32)
    t = pos * (1.0 / (seq_len - 1))
    ang = bands_ref[...] * ((2.0 * math.pi / seq_len) * pos)
    freq = freq_ref[...]
    dot = functools.partial(jnp.dot, preferred_element_type=F32, precision=HIGHEST)
    pre = t * w1t_ref[...] + dot(jnp.cos(ang), w1c_ref[...]) - dot(jnp.sin(ang), w1s_ref[...])
    hdn = jnp.sin(freq * (pre + b1_ref[...]))
    hdn = jnp.sin(freq * (dot(hdn, w2_ref[...]) + b2_ref[...]))
    hdn = jnp.sin(freq * (dot(hdn, w3_ref[...]) + b3_ref[...]))
    decay = jnp.exp(-t * delta_ref[...])
    hf = dot(hdn, wo_ref[:, :d]) * decay
    hb = dot(hdn, wo_ref[:, d:]) * decay
    hf_ref[...] = hf
    hb_ref[...] = hb

    @pl.when(pl.program_id(0) == 0)
    def _():
        ss_ref[...] = jnp.zeros_like(ss_ref)

    ss_ref[...] += jnp.sum(hf * hf + hb * hb, axis=0, keepdims=True)


def _hyena_filter(seq_len, d, w1, b1, w2, b2, w3, b3, w_out, freq, tl):
    width = w2.shape[0]
    bands = np.linspace(1e-4, HY_BANDS - 1, HY_BANDS, dtype=np.float32).reshape(1, HY_BANDS)
    deltas = np.abs(np.linspace(math.log(HY_DECAY_TARGET) / HY_SHORT_DECAY_PCT,
                                math.log(HY_DECAY_TARGET) / HY_LONG_DECAY_PCT, d, dtype=np.float32)).reshape(1, d)
    small = [
        (1, HY_BANDS), (1, width), (HY_BANDS, width), (HY_BANDS, width), (1, width), (width, width), (1, width),
        (width, width), (1, width), (width, 2 * d), (1, width), (1, d),
    ]
    return pl.pallas_call(
        functools.partial(_filter_kernel, seq_len=seq_len),
        out_shape=(jax.ShapeDtypeStruct((seq_len, d), F32), jax.ShapeDtypeStruct((seq_len, d), F32),
                   jax.ShapeDtypeStruct((1, d), F32)),
        grid=(seq_len // tl,),
        in_specs=[pl.BlockSpec(s, lambda i: (0, 0)) for s in small],
        out_specs=(pl.BlockSpec((tl, d), lambda i: (i, 0)), pl.BlockSpec((tl, d), lambda i: (i, 0)),
                   pl.BlockSpec((1, d), lambda i: (0, 0))),
        compiler_params=_params(("arbitrary",)),
        name="hyena_filter",
    )(jnp.asarray(bands), w1[0:1], w1[1:1 + HY_BANDS], w1[1 + HY_BANDS:], b1.reshape(1, width), w2,
      b2.reshape(1, width), w3, b3.reshape(1, width), w_out, freq.reshape(1, width), jnp.asarray(deltas))


def _dft_tables(n1):
    n = n1 * n1
    half = n1 // 2
    idx = np.arange(n1)
    ang = 2.0 * np.pi * np.outer(idx, idx) / n1
    c, s = np.cos(ang), np.sin(ang)
    fa_pair = np.block([[c[:, :half], s[:, :half]], [-s[:, :half], c[:, :half]]])
    fa_real = np.concatenate([c, -s], axis=0)
    fa_inv = np.block([[c[:half], -s[:half]], [s[:half], c[:half]]]) / n
    k1 = idx[:, None, None]
    k2 = idx[None, :, None]
    n2 = idx[None, None, :]
    m = (n2 * (k1 + n1 * k2)) % n
    gang = 2.0 * np.pi * m / n
    gr, gi = np.cos(gang), -np.sin(gang)
    g_fwd = np.concatenate([np.concatenate([gr, -gi], axis=2), np.concatenate([gi, gr], axis=2)], axis=1)
    g_inv = np.transpose(g_fwd, (0, 2, 1))
    to = lambda a: jnp.asarray(a.astype(np.float32), BF16)
    return to(fa_pair), to(fa_real), to(fa_inv), to(g_fwd), to(g_inv)


def _stage_a_kernel(m_ref, z_ref, o_ref):
    o_ref[0] = jnp.dot(m_ref[...], z_ref[0], preferred_element_type=F32).astype(o_ref.dtype)


def _stage_a(mat, z, wb):
    p, k, w = z.shape
    m = mat.shape[0]
    return pl.pallas_call(
        _stage_a_kernel,
        out_shape=jax.ShapeDtypeStruct((p, m, w), BF16),
        grid=(p, w // wb),
        in_specs=[_const_spec((m, k)), pl.BlockSpec((1, k, wb), lambda pi, j: (pi, 0, j))],
        out_specs=pl.BlockSpec((1, m, wb), lambda pi, j: (pi, 0, j)),
        compiler_params=_params(("parallel", "parallel")),
        name="dft_stage_a",
    )(mat, z)


def _stacked(ref):
    _, _, _, n2, d = ref.shape
    return ref[0, :, 0].reshape(2 * n2, d)


def _filter_spectrum_kernel(g_ref, a_ref, ss_ref, o_ref):
    scale = lax.rsqrt(ss_ref[...] + NORM_EPS)
    o_ref[0] = jnp.dot(g_ref[0], _stacked(a_ref), preferred_element_type=F32) * scale


def _filter_spectrum(g_fwd, a, energy):
    n1 = g_fwd.shape[0]
    _, _, _, n2, d = a.shape
    return pl.pallas_call(
        _filter_spectrum_kernel,
        out_shape=jax.ShapeDtypeStruct((n1, 2 * n2, d), F32),
        grid=(n1,),
        in_specs=[
            pl.BlockSpec((1, 2 * n2, 2 * n2), lambda k: (k, 0, 0)),
            pl.BlockSpec((1, 2, 1, n2, d), lambda k: (0, 0, k, 0, 0)),
            pl.BlockSpec((1, d), lambda k: (0, 0)),
        ],
        out_specs=pl.BlockSpec((1, 2 * n2, d), lambda k: (k, 0, 0)),
        compiler_params=_params(("parallel",)),
        name="filter_spectrum",
    )(g_fwd, a, energy)


def _conv_mid_kernel(gf_ref, gi_ref, a_ref, hf_ref, t_ref):
    n2 = a_ref.shape[3]
    d = a_ref.shape[4]
    x = jnp.dot(gf_ref[0], _stacked(a_ref), preferred_element_type=F32)
    xr, xi = x[:n2], x[n2:]
    hr, hi = hf_ref[0, :n2], hf_ref[0, n2:]
    y = jnp.concatenate([xr * hr - xi * hi, xr * hi + xi * hr], axis=0).astype(BF16)
    t = jnp.dot(gi_ref[0], y, preferred_element_type=F32)
    t_ref[0, :, 0] = t.astype(BF16).reshape(2, n2, d)


def _conv_mid(g_fwd, g_inv, a, spec):
    n1 = g_fwd.shape[0]
    p, _, _, n2, d = a.shape
    blk = pl.BlockSpec((1, 2, 1, n2, d), lambda k, pi: (pi, 0, k, 0, 0))
    g_blk = pl.BlockSpec((1, 2 * n2, 2 * n2), lambda k, pi: (k, 0, 0))
    return pl.pallas_call(
        _conv_mid_kernel,
        out_shape=jax.ShapeDtypeStruct(a.shape, BF16),
        grid=(n1, p),
        in_specs=[g_blk, g_blk, blk, pl.BlockSpec((1, 2 * n2, d), lambda k, pi: (k, 0, 0))],
        out_specs=blk,
        compiler_params=_params(("parallel", "arbitrary")),
        name="conv_mid",
    )(g_fwd, g_inv, a, spec)


def _conv_out_kernel(m_ref, t_ref, z_ref, x0_ref, db_ref, o_ref):
    y = jnp.dot(m_ref[...], t_ref[0], preferred_element_type=F32)
    z = z_ref[0].astype(F32)
    o_ref[0] = (x0_ref[0].astype(F32) * (y + z * db_ref[...])).astype(BF16)


def _conv_out(mat, t, z, x0, d_bias_row, wb):
    p, k, w = t.shape
    m = mat.shape[0]
    blk = pl.BlockSpec((1, m, wb), lambda pi, j: (pi, 0, j))
    return pl.pallas_call(
        _conv_out_kernel,
        out_shape=jax.ShapeDtypeStruct((p, m, w), BF16),
        grid=(p, w // wb),
        in_specs=[_const_spec((m, k)), pl.BlockSpec((1, k, wb), lambda pi, j: (pi, 0, j)), blk, blk,
                  _const_spec((1, wb))],
        out_specs=blk,
        compiler_params=_params(("parallel", "parallel")),
        name="dft_stage_out",
    )(mat, t, z, x0, d_bias_row)


def _long_conv_gate(x0, z, filt, energy, d_bias, wb):
    b, l, d = z.shape
    n1 = math.isqrt(2 * l)
    assert n1 * n1 == 2 * l and b % 2 == 0
    p = b // 2
    w = n1 * d
    fa_pair, fa_real, fa_inv, g_fwd, g_inv = _dft_tables(n1)
    f_a = _stage_a(fa_real, filt.reshape(1, n1, w), wb)
    spec = _filter_spectrum(g_fwd, f_a.reshape(1, 2, n1, n1, d), energy)
    zp = z.reshape(p, n1, w)
    a = _stage_a(fa_pair, zp, wb)
    t = _conv_mid(g_fwd, g_inv, a.reshape(p, 2, n1, n1, d), spec)
    d_row = jnp.tile(d_bias, wb // d).reshape(1, wb)
    out = _conv_out(fa_inv, t.reshape(p, 2 * n1, w), zp, x0.reshape(p, n1, w), d_row, wb)
    return out.reshape(b, l, d)


def _mod_rows(mod, lo, hi, d):
    m = mod[lo:hi]
    return [m[:, None, j * d:(j + 1) * d] for j in range(N_MOD)]


def kernel(x, c, ctx, c_ctx, l0_w_mod, l0_b_mod, l0_norm1, l0_norm2, l0_na_w_qkv, l0_na_q_gain, l0_na_k_gain, l0_na_rpb, l0_na_w_o, l0_ffn_w_up, l0_ffn_conv_w, l0_ffn_conv_b, l0_ffn_w_down, l1_w_mod, l1_b_mod, l1_norm1, l1_norm2, l1_hy_w_in, l1_hy_b_in, l1_hy_short_w, l1_hy_short_b, l1_hy_f_w1, l1_hy_f_b1, l1_hy_f_w2, l1_hy_f_b2, l1_hy_f_w3, l1_hy_f_b3, l1_hy_f_wout, l1_hy_f_freq, l1_hy_d_bias, l1_hy_w_out, l1_hy_b_out, l1_ffn_w_up, l1_ffn_conv_w, l1_ffn_conv_b, l1_ffn_w_down):
    b, l, d = x.shape
    n_ctx = ctx.shape[1]
    n_rows = l // GRID_W
    assert n_rows >= K_ROWS and n_rows % Q_ROWS == 0
    tm = min(512, l)
    ff_chunk = 256
    wb = 8 * d

    cond = jnp.zeros((8, d), F32).at[:b].set(c).at[b].set(c_ctx)

    mod = _adaln(cond, l0_w_mod, l0_b_mod)
    sh1, sc1, g1, sh2, sc2, g2 = _mod_rows(mod, 0, b, d)
    csh1, csc1 = _mod_rows(mod, b, b + 1, d)[:2]
    qkv = _qkv(x, sh1, sc1, l0_norm1, l0_na_w_qkv, l0_na_q_gain, l0_na_k_gain, tm)
    qkv_ctx = _qkv(ctx, csh1, csc1, l0_norm1, l0_na_w_qkv, l0_na_q_gain, l0_na_k_gain, n_ctx)
    bias = _block_bias(_col_bias(l0_na_rpb), n_rows)
    attn = _neighbourhood_attention(qkv, qkv_ctx, bias)
    x = _post(attn, x, l0_na_w_o, jnp.zeros((d,), F32), g1, l0_norm2, sh2, sc2, g2,
              l0_ffn_w_up, l0_ffn_conv_w, l0_ffn_conv_b, l0_ffn_w_down, tm, ff_chunk)

    mod = _adaln(cond, l1_w_mod, l1_b_mod)
    sh1, sc1, g1, sh2, sc2, g2 = _mod_rows(mod, 0, b, d)
    x0, z = _hy_in(x, sh1, sc1, l1_norm1, l1_hy_w_in, l1_hy_b_in, l1_hy_short_w, l1_hy_short_b, tm)
    hf, hb, energy = _hyena_filter(l, d, l1_hy_f_w1, l1_hy_f_b1, l1_hy_f_w2, l1_hy_f_b2, l1_hy_f_w3, l1_hy_f_b3,
                                   l1_hy_f_wout, l1_hy_f_freq, min(1024, l))
    filt = jnp.concatenate([hf, jnp.zeros((1, d), F32), hb[:0:-1]], axis=0).astype(BF16)
    gated = _long_conv_gate(x0, z, filt, energy, l1_hy_d_bias, wb)
    x = _post(gated, x, l1_hy_w_out, l1_hy_b_out, g1, l1_norm2, sh2, sc2, g2,
              l1_ffn_w_up, l1_ffn_conv_w, l1_ffn_conv_b, l1_ffn_w_down, tm, ff_chunk)
    return x
```

```python
import functools
import math

import numpy as np
import jax
import jax.numpy as jnp
from jax import lax
from jax.experimental import pallas as pl
from jax.experimental.pallas import tpu as pltpu

F32 = jnp.float32
BF16 = jnp.bfloat16
U32 = jnp.uint32
HIGHEST = lax.Precision.HIGHEST

NORM_EPS = 1e-6
N_MOD = 6
HEAD_DIM = 64
GRID_W = 64
WIN_H = 8
WIN_W = 16
HY_BANDS = 16
HY_DECAY_TARGET = 1e-2
HY_SHORT_DECAY_PCT = 0.3
HY_LONG_DECAY_PCT = 1.5

LANES = 128
HALO = 16
Q_ROWS = 8
K_ROWS = 16
NEG = -1e30
VMEM_LIMIT = 56 * 1024 * 1024


def _params(sem):
    return pltpu.CompilerParams(dimension_semantics=sem, vmem_limit_bytes=VMEM_LIMIT)


def _const_spec(shape):
    return pl.BlockSpec(shape, lambda *_: (0,) * len(shape), pipeline_mode=pl.Buffered(1))


def _rms_mod(x, gain, shift, scale):
    ms = jnp.mean(x * x, axis=-1, keepdims=True)
    y = x * lax.rsqrt(ms + NORM_EPS) * gain
    return y * (1.0 + scale) + shift


def _adaln_kernel(c_ref, w_ref, b_ref, o_ref):
    c = c_ref[...]
    s = c / (1.0 + jnp.exp(-c))
    o_ref[...] = jnp.dot(s, w_ref[...], preferred_element_type=F32, precision=HIGHEST) + b_ref[...]


def _adaln(cond, w_mod, b_mod):
    rows, d = cond.shape
    n = w_mod.shape[1]
    tn = d
    return pl.pallas_call(
        _adaln_kernel,
        out_shape=jax.ShapeDtypeStruct((rows, n), F32),
        grid=(n // tn,),
        in_specs=[
            pl.BlockSpec((rows, d), lambda j: (0, 0)),
            pl.BlockSpec((d, tn), lambda j: (0, j)),
            pl.BlockSpec((1, tn), lambda j: (0, j)),
        ],
        out_specs=pl.BlockSpec((rows, tn), lambda j: (0, j)),
        compiler_params=_params(("arbitrary",)),
        name="adaln",
    )(cond, w_mod, b_mod.reshape(1, n))


def _qkv_kernel(x_ref, sh_ref, sc_ref, g_ref, w_ref, qg_ref, kg_ref, p1_ref, p2_ref, o_ref):
    d = x_ref.shape[-1]
    h = _rms_mod(x_ref[0], g_ref[...], sh_ref[0], sc_ref[0]).astype(BF16)
    for j, gain_ref in ((0, qg_ref), (1, kg_ref)):
        t = jnp.dot(h, w_ref[:, j * d:(j + 1) * d], preferred_element_type=F32)
        ms = jnp.dot((t * t).astype(BF16), p1_ref[...], preferred_element_type=F32)
        r = lax.rsqrt(ms + NORM_EPS)
        r_hi = r.astype(BF16)
        r_lo = (r - r_hi.astype(F32)).astype(BF16)
        rr = jnp.dot(jnp.concatenate([r_hi, r_lo], axis=-1), p2_ref[...], preferred_element_type=F32)
        o_ref[0, :, j * d:(j + 1) * d] = (t * rr * gain_ref[...]).astype(BF16)
    v = jnp.dot(h, w_ref[:, 2 * d:3 * d], preferred_element_type=F32)
    o_ref[0, :, 2 * d:3 * d] = v.astype(BF16)


def _qkv(x, shift, scale, gain, w_qkv, q_gain, k_gain, tm):
    b, l, d = x.shape
    heads = d // HEAD_DIM
    per_batch = shift.shape[0] == b
    mod_map = (lambda bi, i: (bi, 0, 0)) if per_batch else (lambda bi, i: (0, 0, 0))
    p1 = np.zeros((d, LANES), np.float32)
    p1[np.arange(d), np.arange(d) // HEAD_DIM] = 1.0 / HEAD_DIM
    p2 = np.zeros((2 * LANES, d), np.float32)
    p2[np.arange(d) // HEAD_DIM, np.arange(d)] = 1.0
    p2[LANES + np.arange(d) // HEAD_DIM, np.arange(d)] = 1.0
    qg = (jnp.tile(q_gain, heads) * (HEAD_DIM ** -0.5)).reshape(1, d)
    kg = jnp.tile(k_gain, heads).reshape(1, d)
    return pl.pallas_call(
        _qkv_kernel,
        out_shape=jax.ShapeDtypeStruct((b, l, 3 * d), BF16),
        grid=(b, l // tm),
        in_specs=[
            pl.BlockSpec((1, tm, d), lambda bi, i: (bi, i, 0)),
            pl.BlockSpec((1, 1, d), mod_map),
            pl.BlockSpec((1, 1, d), mod_map),
            _const_spec((1, d)),
            _const_spec((d, 3 * d)),
            _const_spec((1, d)),
            _const_spec((1, d)),
            _const_spec((d, LANES)),
            _const_spec((2 * LANES, d)),
        ],
        out_specs=pl.BlockSpec((1, tm, 3 * d), lambda bi, i: (bi, i, 0)),
        compiler_params=_params(("parallel", "parallel")),
        name="qkv",
    )(x, shift, scale, gain.reshape(1, d), w_qkv.astype(BF16), qg, kg,
      jnp.asarray(p1, BF16), jnp.asarray(p2, BF16))


def _toeplitz_kernel(r_ref, oh_ref, m_ref, o_ref):
    o_ref[...] = jnp.dot(r_ref[...], oh_ref[...], preferred_element_type=F32, precision=HIGHEST) + m_ref[...]


def _col_bias(rpb):
    heads, n_dr, n_dc = rpb.shape
    k_pad = 32
    qc = np.arange(GRID_W)[:, None]
    kc = np.arange(GRID_W)[None, :]
    c0 = np.clip(qc - WIN_W // 2, 0, GRID_W - WIN_W)
    valid = (kc >= c0) & (kc < c0 + WIN_W)
    dc = kc - qc + (WIN_W - 1)
    onehot = np.zeros((k_pad, GRID_W, GRID_W), np.float32)
    for j in range(n_dc):
        onehot[j] = ((dc == j) & valid).astype(np.float32)
    onehot = onehot.reshape(k_pad, GRID_W * GRID_W)
    mask = np.where(valid, 0.0, NEG).astype(np.float32).reshape(1, GRID_W * GRID_W)
    rows = heads * n_dr
    rows_pad = -(-rows // 8) * 8
    r2 = jnp.zeros((rows_pad, k_pad), F32).at[:rows, :n_dc].set(rpb.reshape(rows, n_dc))
    t = pl.pallas_call(
        _toeplitz_kernel,
        out_shape=jax.ShapeDtypeStruct((rows_pad, GRID_W * GRID_W), F32),
        name="rpb_toeplitz",
    )(r2, jnp.asarray(onehot), jnp.asarray(mask))
    return t[:rows].reshape(heads, n_dr, GRID_W, GRID_W)


def _block_bias(t, n_rows):
    heads = t.shape[0]
    kh = min(WIN_H, n_rows)
    masked = jnp.full((heads, GRID_W, GRID_W), NEG, F32)
    classes = []
    for q0, k0 in ((0, 0), (Q_ROWS, Q_ROWS - WIN_H // 2), (n_rows - Q_ROWS, n_rows - K_ROWS)):
        strips = []
        for qr in range(Q_ROWS):
            r = q0 + qr
            r0 = min(max(r - kh // 2, 0), n_rows - kh)
            blocks = []
            for kr in range(K_ROWS):
                kk = k0 + kr
                blocks.append(t[:, kk - r + WIN_H - 1] if r0 <= kk < r0 + kh else masked)
            strips.append(jnp.concatenate(blocks, axis=-1))
        classes.append(jnp.concatenate(strips, axis=-2))
    return jnp.stack(classes).astype(BF16)


def _na_kernel(q_ref, k_ref, v_ref, kc_ref, vc_ref, bias_ref, o_ref):
    i = pl.program_id(2)
    tq = q_ref.shape[1]
    tk = bias_ref.shape[-1]
    l = k_ref.shape[1]
    start = pl.multiple_of(jnp.clip(i * tq - (tk - tq) // 2, 0, l - tk), tq // 2)
    q = q_ref[0]
    kw = k_ref[0, pl.ds(start, tk), :]
    vw = v_ref[0, pl.ds(start, tk), :]
    kc = kc_ref[0]
    vc = vc_ref[0]
    first_head = lax.broadcasted_iota(jnp.int32, (1, LANES), 1) < HEAD_DIM
    nt = (((1,), (1,)), ((), ()))
    outs = []
    for h in range(2):
        qh = jnp.where(first_head if h == 0 else jnp.logical_not(first_head), q, jnp.zeros_like(q))
        s_loc = lax.dot_general(qh, kw, nt, preferred_element_type=F32) + bias_ref[0, h].astype(F32)
        s_ctx = lax.dot_general(qh, kc, nt, preferred_element_type=F32)
        m = jnp.maximum(jnp.max(s_loc, axis=-1, keepdims=True), jnp.max(s_ctx, axis=-1, keepdims=True))
        p_loc = jnp.exp(s_loc - m)
        p_ctx = jnp.exp(s_ctx - m)
        denom = jnp.sum(p_loc, axis=-1, keepdims=True) + jnp.sum(p_ctx, axis=-1, keepdims=True)
        o = (jnp.dot(p_loc.astype(BF16), vw, preferred_element_type=F32)
             + jnp.dot(p_ctx.astype(BF16), vc, preferred_element_type=F32))
        outs.append(o / denom)
    o_ref[0] = jnp.where(first_head, outs[0], outs[1]).astype(BF16)


def _neighbourhood_attention(qkv, qkv_ctx, bias):
    b, l, d3 = qkv.shape
    d = d3 // 3
    n_ctx = qkv_ctx.shape[1]
    pairs = d // LANES
    tq, tk = bias.shape[-2:]
    nb = l // tq

    def bias_map(bi, hp, i):
        return (jnp.where(i == 0, 0, jnp.where(i == nb - 1, 2, 1)), hp, 0, 0)

    return pl.pallas_call(
        _na_kernel,
        out_shape=jax.ShapeDtypeStruct((b, l, d), BF16),
        grid=(b, pairs, nb),
        in_specs=[
            pl.BlockSpec((1, tq, LANES), lambda bi, hp, i: (bi, i, hp)),
            pl.BlockSpec((1, l, LANES), lambda bi, hp, i: (bi, 0, pairs + hp)),
            pl.BlockSpec((1, l, LANES), lambda bi, hp, i: (bi, 0, 2 * pairs + hp)),
            pl.BlockSpec((1, n_ctx, LANES), lambda bi, hp, i: (bi, 0, pairs + hp)),
            pl.BlockSpec((1, n_ctx, LANES), lambda bi, hp, i: (bi, 0, 2 * pairs + hp)),
            pl.BlockSpec((1, 2, tq, tk), bias_map),
        ],
        out_specs=pl.BlockSpec((1, tq, LANES), lambda bi, hp, i: (bi, i, hp)),
        compiler_params=_params(("parallel", "parallel", "arbitrary")),
        name="na_attention",
    )(qkv, qkv, qkv, qkv_ctx, qkv_ctx, bias)


def _halo_rows(prev_ref, main_ref, next_ref):
    return jnp.concatenate([prev_ref[0], main_ref[0], next_ref[0]], axis=0)


def _edge_mask(rows, axis):
    i = pl.program_id(axis)
    r = lax.broadcasted_iota(jnp.int32, (rows, 1), 0)
    lo = jnp.where(i == 0, HALO, 0)
    hi = jnp.where(i == pl.num_programs(axis) - 1, rows - HALO, rows)
    return jnp.where((r >= lo) & (r < hi), 1.0, 0.0).astype(F32)


def _shift_rows(g, rows):
    return pltpu.roll(g, 1, 0), pltpu.roll(g, rows - 1, 0)


def _erf(x):
    return lax.erf(x)


def _post_kernel(mp_ref, mm_ref, mn_ref, xp_ref, xm_ref, xn_ref, wmix_ref, bmix_ref, g1_ref, n2_ref,
                 sh_ref, sc_ref, g2_ref, wup_ref, cw_ref, cb_ref, wdn_ref, o_ref, *, ff_chunk):
    rows = mm_ref.shape[1] + 2 * HALO
    d_ff = wdn_ref.shape[0]
    mix = _halo_rows(mp_ref, mm_ref, mn_ref)
    x = _halo_rows(xp_ref, xm_ref, xn_ref)
    y = jnp.dot(mix, wmix_ref[...], preferred_element_type=F32) + bmix_ref[...]
    x1 = x + g1_ref[0] * y
    h = _rms_mod(x1, n2_ref[...], sh_ref[0], sc_ref[0]).astype(BF16)
    edge = _edge_mask(rows, 1)
    acc = jnp.zeros((rows - 2 * HALO, x.shape[-1]), F32)
    for c in range(0, d_ff, ff_chunk):
        a = jnp.dot(h, wup_ref[:, c:c + ff_chunk], preferred_element_type=F32)
        g = jnp.dot(h, wup_ref[:, d_ff + c:d_ff + c + ff_chunk], preferred_element_type=F32) * edge
        g_prev, g_next = _shift_rows(g, rows)
        cw = cw_ref[:, c:c + ff_chunk]
        gc = g_prev * cw[0:1] + g * cw[1:2] + g_next * cw[2:3] + cb_ref[:, c:c + ff_chunk]
        gc = gc[HALO:rows - HALO]
        u = a[HALO:rows - HALO] * (0.5 * gc * (1.0 + _erf(gc * (2.0 ** -0.5))))
        acc = acc + jnp.dot(u.astype(BF16), wdn_ref[c:c + ff_chunk, :], preferred_element_type=F32)
    o_ref[0] = x1[HALO:rows - HALO] + g2_ref[0] * acc


def _halo_specs(tm, l, d):
    nblk = tm // HALO
    last = l // HALO - 1
    return [
        pl.BlockSpec((1, HALO, d), lambda bi, i: (bi, jnp.maximum(i * nblk - 1, 0), 0)),
        pl.BlockSpec((1, tm, d), lambda bi, i: (bi, i, 0)),
        pl.BlockSpec((1, HALO, d), lambda bi, i: (bi, jnp.minimum((i + 1) * nblk, last), 0)),
    ]


def _post(mix, x, w_mix, b_mix, g1, norm2, sh2, sc2, g2, w_up, conv_w, conv_b, w_down, tm, ff_chunk):
    b, l, d = x.shape
    d_ff = w_down.shape[0]
    mod = pl.BlockSpec((1, 1, d), lambda bi, i: (bi, 0, 0))
    return pl.pallas_call(
        functools.partial(_post_kernel, ff_chunk=ff_chunk),
        out_shape=jax.ShapeDtypeStruct((b, l, d), F32),
        grid=(b, l // tm),
        in_specs=_halo_specs(tm, l, d) + _halo_specs(tm, l, d) + [
            _const_spec((d, d)), _const_spec((1, d)), mod, _const_spec((1, d)), mod, mod, mod,
            _const_spec((d, 2 * d_ff)), _const_spec((3, d_ff)), _const_spec((1, d_ff)), _const_spec((d_ff, d)),
        ],
        out_specs=pl.BlockSpec((1, tm, d), lambda bi, i: (bi, i, 0)),
        compiler_params=_params(("parallel", "parallel")),
        name="post_ffn",
    )(mix, mix, mix, x, x, x, w_mix.astype(BF16), b_mix.reshape(1, d), g1, norm2.reshape(1, d), sh2, sc2, g2,
      w_up.astype(BF16), conv_w, conv_b.reshape(1, d_ff), w_down.astype(BF16))


def _hy_in_kernel(xp_ref, xm_ref, xn_ref, sh_ref, sc_ref, g_ref, w_ref, b_ref, cw_ref, cb_ref, x0_ref, z_ref):
    rows = xm_ref.shape[1] + 2 * HALO
    d = xm_ref.shape[-1]
    h = _rms_mod(_halo_rows(xp_ref, xm_ref, xn_ref), g_ref[...], sh_ref[0], sc_ref[0]).astype(BF16)
    edge = _edge_mask(rows, 1)
    parts = []
    for j in range(3):
        cols = slice(j * d, (j + 1) * d)
        u = (jnp.dot(h, w_ref[:, cols], preferred_element_type=F32) + b_ref[:, cols]) * edge
        u_prev, u_next = _shift_rows(u, rows)
        cw = cw_ref[:, cols]
        uc = u_prev * cw[0:1] + u * cw[1:2] + u_next * cw[2:3] + cb_ref[:, cols]
        parts.append(uc[HALO:rows - HALO])
    x0_ref[0] = parts[0].astype(BF16)
    z_ref[0] = (parts[2] * parts[1]).astype(BF16)


def _hy_in(x, shift, scale, gain, w_in, b_in, short_w, short_b, tm):
    b, l, d = x.shape
    mod = pl.BlockSpec((1, 1, d), lambda bi, i: (bi, 0, 0))
    out = pl.BlockSpec((1, tm, d), lambda bi, i: (bi, i, 0))
    return pl.pallas_call(
        _hy_in_kernel,
        out_shape=(jax.ShapeDtypeStruct((b, l, d), BF16), jax.ShapeDtypeStruct((b, l, d), BF16)),
        grid=(b, l // tm),
        in_specs=_halo_specs(tm, l, d) + [
            mod, mod, _const_spec((1, d)), _const_spec((d, 3 * d)), _const_spec((1, 3 * d)),
            _const_spec((3, 3 * d)), _const_spec((1, 3 * d)),
        ],
        out_specs=(out, out),
        compiler_params=_params(("parallel", "parallel")),
        name="hyena_in",
    )(x, x, x, shift, scale, gain.reshape(1, d), w_in.astype(BF16), b_in.reshape(1, 3 * d),
      short_w, short_b.reshape(1, 3 * d))


def _filter_kernel(bands_ref, w1t_ref, w1c_ref, w1s_ref, b1_ref, w2_ref, b2_ref, w3_ref, b3_ref, wo_ref,
                   freq_ref, delta_ref, hf_ref, hb_ref, ss_ref, *, seq_len):
    tl = hf_ref.shape[0]
    d = hf_ref.shape[1]
    pos = (pl.program_id(0) * tl + lax.broadcasted_iota(jnp.int32, (tl, 1), 0)).astype(F32)
    t = pos * (1.0 / (seq_len - 1))
    ang = bands_ref[...] * ((2.0 * math.pi / seq_len) * pos)
    freq = freq_ref[...]
    dot = functools.partial(jnp.dot, preferred_element_type=F32, precision=HIGHEST)
    pre = t * w1t_ref[...] + dot(jnp.cos(ang), w1c_ref[...]) - dot(jnp.sin(ang), w1s_ref[...])
    hdn = jnp.sin(freq * (pre + b1_ref[...]))
    hdn = jnp.sin(freq * (dot(hdn, w2_ref[...]) + b2_ref[...]))
    hdn = jnp.sin(freq * (dot(hdn, w3_ref[...]) + b3_ref[...]))
    decay = jnp.exp(-t * delta_ref[...])
    hf = dot(hdn, wo_ref[:, :d]) * decay
    hb = dot(hdn, wo_ref[:, d:]) * decay
    hf_ref[...] = hf
    hb_ref[...] = jnp.where(pos > 0.0, hb, 0.0)

    @pl.when(pl.program_id(0) == 0)
    def _():
        ss_ref[...] = jnp.zeros_like(ss_ref)

    ss_ref[...] += jnp.sum(hf * hf + hb * hb, axis=0, keepdims=True)


def _hyena_filter(seq_len, d, w1, b1, w2, b2, w3, b3, w_out, freq, tl):
    width = w2.shape[0]
    bands = np.linspace(1e-4, HY_BANDS - 1, HY_BANDS, dtype=np.float32).reshape(1, HY_BANDS)
    deltas = np.abs(np.linspace(math.log(HY_DECAY_TARGET) / HY_SHORT_DECAY_PCT,
                                math.log(HY_DECAY_TARGET) / HY_LONG_DECAY_PCT, d, dtype=np.float32)).reshape(1, d)
    small = [
        (1, HY_BANDS), (1, width), (HY_BANDS, width), (HY_BANDS, width), (1, width), (width, width), (1, width),
        (width, width), (1, width), (width, 2 * d), (1, width), (1, d),
    ]
    return pl.pallas_call(
        functools.partial(_filter_kernel, seq_len=seq_len),
        out_shape=(jax.ShapeDtypeStruct((seq_len, d), F32), jax.ShapeDtypeStruct((seq_len, d), F32),
                   jax.ShapeDtypeStruct((1, d), F32)),
        grid=(seq_len // tl,),
        in_specs=[pl.BlockSpec(s, lambda i: (0, 0)) for s in small],
        out_specs=(pl.BlockSpec((tl, d), lambda i: (i, 0)), pl.BlockSpec((tl, d), lambda i: (i, 0)),
                   pl.BlockSpec((1, d), lambda i: (0, 0))),
        compiler_params=_params(("arbitrary",)),
        name="hyena_filter",
    )(jnp.asarray(bands), w1[0:1], w1[1:1 + HY_BANDS], w1[1 + HY_BANDS:], b1.reshape(1, width), w2,
      b2.reshape(1, width), w3, b3.reshape(1, width), w_out, freq.reshape(1, width), jnp.asarray(deltas))


def _dft_tables(n1):
    n = n1 * n1
    half = n1 // 2
    idx = np.arange(n1)
    ang = 2.0 * np.pi * np.outer(idx, idx) / n1
    c, s = np.cos(ang), np.sin(ang)
    fa_pair = np.block([[c[:, :half], s[:, :half]], [-s[:, :half], c[:, :half]]])
    zero = np.zeros((n1, half))
    fa_real = np.block([[c[:, :half], zero], [-s[:, :half], zero], [zero, c[:, :half]], [zero, -s[:, :half]]])
    fa_inv = np.block([[c[:half], -s[:half]], [s[:half], c[:half]]]) / n
    k1 = idx[:, None, None]
    k2 = idx[None, :, None]
    n2 = idx[None, None, :]
    m = (n2 * (k1 + n1 * k2)) % n
    gang = 2.0 * np.pi * m / n
    gr, gi = np.cos(gang), -np.sin(gang)
    g_fwd = np.concatenate([np.concatenate([gr, -gi], axis=2), np.concatenate([gi, gr], axis=2)], axis=1)
    g_inv = np.transpose(g_fwd, (0, 2, 1))
    to = lambda a: jnp.asarray(a.astype(np.float32), BF16)
    return to(fa_pair), to(fa_real), to(fa_inv), to(g_fwd), to(g_inv)


def _stage_a_kernel(m_ref, z_ref, o_ref):
    o_ref[0] = jnp.dot(m_ref[...], z_ref[0], preferred_element_type=F32).astype(o_ref.dtype)


def _stage_a(mat, z, wb):
    p, k, w = z.shape
    m = mat.shape[0]
    return pl.pallas_call(
        _stage_a_kernel,
        out_shape=jax.ShapeDtypeStruct((p, m, w), BF16),
        grid=(p, w // wb),
        in_specs=[_const_spec((m, k)), pl.BlockSpec((1, k, wb), lambda pi, j: (pi, 0, j))],
        out_specs=pl.BlockSpec((1, m, wb), lambda pi, j: (pi, 0, j)),
        compiler_params=_params(("parallel", "parallel")),
        name="dft_stage_a",
    )(mat, z)


def _stacked(ref):
    _, _, _, n2, d = ref.shape
    return ref[0, :, 0].reshape(2 * n2, d)


def _pack_pair(hi, lo):
    hi_bits = pltpu.bitcast(hi.astype(BF16).astype(F32), U32)
    lo_bits = pltpu.bitcast(lo.astype(BF16).astype(F32), U32)
    return hi_bits | (lo_bits >> 16)


def _unpack_pair(word):
    return (pltpu.bitcast(word & jnp.uint32(0xFFFF0000), F32), pltpu.bitcast(word << 16, F32))


def _pitch(n2):
    return n2 + 8


def _filter_spectrum_kernel(hf_ref, hb_ref, ss_ref, fa_ref, g_ref, o_ref, a_scr):
    n1 = fa_ref.shape[0] // 4
    n2 = g_ref.shape[1] // 2
    half = hf_ref.shape[0] // n2
    pitch = _pitch(n2)
    kb = g_ref.shape[0]

    @pl.when(pl.program_id(1) == 0)
    def _():
        def body(j, carry):
            rows = jnp.concatenate([hf_ref[pl.ds(j, half, stride=n2), :], hb_ref[pl.ds(j, half, stride=n2), :]], axis=0)
            r = jnp.dot(fa_ref[...], rows.astype(BF16), preferred_element_type=F32)
            a_scr[0, pl.ds(j, n1, stride=pitch), :] = _pack_pair(r[:n1], r[n1:2 * n1])
            a_scr[1, pl.ds(j, n1, stride=pitch), :] = _pack_pair(r[2 * n1:3 * n1], r[3 * n1:])
            return carry

        lax.fori_loop(0, n2, body, 0)

    scale = lax.rsqrt(ss_ref[...] + NORM_EPS)
    for k in range(kb):
        row0 = pl.multiple_of((pl.program_id(1) * kb + k) * pitch, 8)
        fr, fi = _unpack_pair(a_scr[0, pl.ds(row0, n2), :])
        br, bi = _unpack_pair(a_scr[1, pl.ds(row0, n2), :])
        a = jnp.concatenate([jnp.concatenate([fr, fi], axis=0), jnp.concatenate([br, bi], axis=0)], axis=1)
        p = jnp.dot(g_ref[k], a.astype(BF16), preferred_element_type=F32)
        lanes = p.shape[1] // 2
        o_ref[k, :n2] = (p[:n2, :lanes] + p[:n2, lanes:]) * scale
        o_ref[k, n2:] = (p[n2:, :lanes] - p[n2:, lanes:]) * scale


def _filter_spectrum(fa_filt, g_fwd, hf, hb, energy, kb):
    n1 = g_fwd.shape[0]
    n2 = g_fwd.shape[1] // 2
    l, d = hf.shape
    tc = LANES
    half_spec = pl.BlockSpec((l, tc), lambda c, k: (0, c))
    return pl.pallas_call(
        _filter_spectrum_kernel,
        out_shape=jax.ShapeDtypeStruct((n1, 2 * n2, d), F32),
        grid=(d // tc, n1 // kb),
        in_specs=[
            half_spec, half_spec, pl.BlockSpec((1, tc), lambda c, k: (0, c)),
            _const_spec(fa_filt.shape),
            pl.BlockSpec((kb, 2 * n2, 2 * n2), lambda c, k: (k, 0, 0)),
        ],
        out_specs=pl.BlockSpec((kb, 2 * n2, tc), lambda c, k: (k, 0, c)),
        scratch_shapes=[pltpu.VMEM((2, n1 * _pitch(n2), tc), U32)],
        compiler_params=_params(("parallel", "arbitrary")),
        name="filter_spectrum",
    )(hf, hb, energy, fa_filt, g_fwd)


def _conv_mid_kernel(gf_ref, gi_ref, a_ref, hf_ref, t_ref):
    n2 = a_ref.shape[3]
    d = a_ref.shape[4]
    x = jnp.dot(gf_ref[0], _stacked(a_ref), preferred_element_type=F32)
    xr, xi = x[:n2], x[n2:]
    hr, hi = hf_ref[0, :n2], hf_ref[0, n2:]
    y = jnp.concatenate([xr * hr - xi * hi, xr * hi + xi * hr], axis=0).astype(BF16)
    t = jnp.dot(gi_ref[0], y, preferred_element_type=F32)
    t_ref[0, :, 0] = t.astype(BF16).reshape(2, n2, d)


def _conv_mid(g_fwd, g_inv, a, spec):
    n1 = g_fwd.shape[0]
    p, _, _, n2, d = a.shape
    blk = pl.BlockSpec((1, 2, 1, n2, d), lambda k, pi: (pi, 0, k, 0, 0))
    g_blk = pl.BlockSpec((1, 2 * n2, 2 * n2), lambda k, pi: (k, 0, 0))
    return pl.pallas_call(
        _conv_mid_kernel,
        out_shape=jax.ShapeDtypeStruct(a.shape, BF16),
        grid=(n1, p),
        in_specs=[g_blk, g_blk, blk, pl.BlockSpec((1, 2 * n2, d), lambda k, pi: (k, 0, 0))],
        out_specs=blk,
        compiler_params=_params(("parallel", "arbitrary")),
        name="conv_mid",
    )(g_fwd, g_inv, a, spec)


def _conv_out_kernel(m_ref, t_ref, z_ref, x0_ref, db_ref, o_ref):
    y = jnp.dot(m_ref[...], t_ref[0], preferred_element_type=F32)
    z = z_ref[0].astype(F32)
    o_ref[0] = (x0_ref[0].astype(F32) * (y + z * db_ref[...])).astype(BF16)


def _conv_out(mat, t, z, x0, d_bias_row, wb):
    p, k, w = t.shape
    m = mat.shape[0]
    blk = pl.BlockSpec((1, m, wb), lambda pi, j: (pi, 0, j))
    return pl.pallas_call(
        _conv_out_kernel,
        out_shape=jax.ShapeDtypeStruct((p, m, w), BF16),
        grid=(p, w // wb),
        in_specs=[_const_spec((m, k)), pl.BlockSpec((1, k, wb), lambda pi, j: (pi, 0, j)), blk, blk,
                  _const_spec((1, wb))],
        out_specs=blk,
        compiler_params=_params(("parallel", "parallel")),
        name="dft_stage_out",
    )(mat, t, z, x0, d_bias_row)


def _long_conv_gate(x0, z, hf, hb, energy, d_bias, wb):
    b, l, d = z.shape
    n1 = math.isqrt(2 * l)
    assert n1 * n1 == 2 * l and b % 2 == 0
    p = b // 2
    w = n1 * d
    fa_pair, fa_real, fa_inv, g_fwd, g_inv = _dft_tables(n1)
    spec = _filter_spectrum(fa_real, g_fwd, hf, hb, energy, min(16, n1))
    zp = z.reshape(p, n1, w)
    a = _stage_a(fa_pair, zp, wb)
    t = _conv_mid(g_fwd, g_inv, a.reshape(p, 2, n1, n1, d), spec)
    d_row = jnp.tile(d_bias, wb // d).reshape(1, wb)
    out = _conv_out(fa_inv, t.reshape(p, 2 * n1, w), zp, x0.reshape(p, n1, w), d_row, wb)
    return out.reshape(b, l, d)


def _mod_rows(mod, lo, hi, d):
    m = mod[lo:hi]
    return [m[:, None, j * d:(j + 1) * d] for j in range(N_MOD)]


def kernel(x, c, ctx, c_ctx, l0_w_mod, l0_b_mod, l0_norm1, l0_norm2, l0_na_w_qkv, l0_na_q_gain, l0_na_k_gain, l0_na_rpb, l0_na_w_o, l0_ffn_w_up, l0_ffn_conv_w, l0_ffn_conv_b, l0_ffn_w_down, l1_w_mod, l1_b_mod, l1_norm1, l1_norm2, l1_hy_w_in, l1_hy_b_in, l1_hy_short_w, l1_hy_short_b, l1_hy_f_w1, l1_hy_f_b1, l1_hy_f_w2, l1_hy_f_b2, l1_hy_f_w3, l1_hy_f_b3, l1_hy_f_wout, l1_hy_f_freq, l1_hy_d_bias, l1_hy_w_out, l1_hy_b_out, l1_ffn_w_up, l1_ffn_conv_w, l1_ffn_conv_b, l1_ffn_w_down):
    b, l, d = x.shape
    n_ctx = ctx.shape[1]
    n_rows = l // GRID_W
    assert n_rows >= K_ROWS and n_rows % Q_ROWS == 0
    tm = min(512, l)
    ff_chunk = 256
    wb = 8 * d

    cond = jnp.zeros((8, d), F32).at[:b].set(c).at[b].set(c_ctx)

    mod = _adaln(cond, l0_w_mod, l0_b_mod)
    sh1, sc1, g1, sh2, sc2, g2 = _mod_rows(mod, 0, b, d)
    csh1, csc1 = _mod_rows(mod, b, b + 1, d)[:2]
    qkv = _qkv(x, sh1, sc1, l0_norm1, l0_na_w_qkv, l0_na_q_gain, l0_na_k_gain, tm)
    qkv_ctx = _qkv(ctx, csh1, csc1, l0_norm1, l0_na_w_qkv, l0_na_q_gain, l0_na_k_gain, n_ctx)
    bias = _block_bias(_col_bias(l0_na_rpb), n_rows)
    attn = _neighbourhood_attention(qkv, qkv_ctx, bias)
    x = _post(attn, x, l0_na_w_o, jnp.zeros((d,), F32), g1, l0_norm2, sh2, sc2, g2,
              l0_ffn_w_up, l0_ffn_conv_w, l0_ffn_conv_b, l0_ffn_w_down, tm, ff_chunk)

    mod = _adaln(cond, l1_w_mod, l1_b_mod)
    sh1, sc1, g1, sh2, sc2, g2 = _mod_rows(mod, 0, b, d)
    x0, z = _hy_in(x, sh1, sc1, l1_norm1, l1_hy_w_in, l1_hy_b_in, l1_hy_short_w, l1_hy_short_b, tm)
    hf, hb, energy = _hyena_filter(l, d, l1_hy_f_w1, l1_hy_f_b1, l1_hy_f_w2, l1_hy_f_b2, l1_hy_f_w3, l1_hy_f_b3,
                                   l1_hy_f_wout, l1_hy_f_freq, min(1024, l))
    gated = _long_conv_gate(x0, z, hf, hb, energy, l1_hy_d_bias, wb)
    x = _post(gated, x, l1_hy_w_out, l1_hy_b_out, g1, l1_norm2, sh2, sc2, g2,
              l1_ffn_w_up, l1_ffn_conv_w, l1_ffn_conv_b, l1_ffn_w_down, tm, ff_chunk)
    return x
```

```python
import functools
import math

import numpy as np
import jax
import jax.numpy as jnp
from jax import lax
from jax.experimental import pallas as pl
from jax.experimental.pallas import tpu as pltpu

F32 = jnp.float32
BF16 = jnp.bfloat16
U32 = jnp.uint32
HIGHEST = lax.Precision.HIGHEST

NORM_EPS = 1e-6
N_MOD = 6
HEAD_DIM = 64
GRID_W = 64
WIN_H = 8
WIN_W = 16
HY_BANDS = 16
HY_DECAY_TARGET = 1e-2
HY_SHORT_DECAY_PCT = 0.3
HY_LONG_DECAY_PCT = 1.5

LANES = 128
HALO = 16
Q_ROWS = 8
K_ROWS = 16
NA_CHUNK = 32
NA_PV_ROWS = 256
LOG2E = math.log2(math.e)
NEG = -1e30
VMEM_LIMIT = 56 * 1024 * 1024


def _params(sem):
    return pltpu.CompilerParams(dimension_semantics=sem, vmem_limit_bytes=VMEM_LIMIT)


def _const_spec(shape):
    return pl.BlockSpec(shape, lambda *_: (0,) * len(shape), pipeline_mode=pl.Buffered(1))


def _rms_mod(x, gain, shift, scale):
    ms = jnp.mean(x * x, axis=-1, keepdims=True)
    y = x * lax.rsqrt(ms + NORM_EPS) * gain
    return y * (1.0 + scale) + shift


def _adaln_kernel(c_ref, w_ref, b_ref, o_ref):
    c = c_ref[...]
    s = c / (1.0 + jnp.exp(-c))
    o_ref[...] = jnp.dot(s, w_ref[...], preferred_element_type=F32, precision=HIGHEST) + b_ref[...]


def _adaln(cond, w_mod, b_mod):
    rows, d = cond.shape
    n = w_mod.shape[1]
    tn = d
    return pl.pallas_call(
        _adaln_kernel,
        out_shape=jax.ShapeDtypeStruct((rows, n), F32),
        grid=(n // tn,),
        in_specs=[
            pl.BlockSpec((rows, d), lambda j: (0, 0)),
            pl.BlockSpec((d, tn), lambda j: (0, j)),
            pl.BlockSpec((1, tn), lambda j: (0, j)),
        ],
        out_specs=pl.BlockSpec((rows, tn), lambda j: (0, j)),
        compiler_params=_params(("arbitrary",)),
        name="adaln",
    )(cond, w_mod, b_mod.reshape(1, n))


def _qkv_kernel(x_ref, sh_ref, sc_ref, g_ref, w_ref, qg_ref, kg_ref, p1_ref, p2_ref, o_ref):
    d = x_ref.shape[-1]
    h = _rms_mod(x_ref[0], g_ref[...], sh_ref[0], sc_ref[0]).astype(BF16)
    for j, gain_ref in ((0, qg_ref), (1, kg_ref)):
        t = jnp.dot(h, w_ref[:, j * d:(j + 1) * d], preferred_element_type=F32)
        ms = jnp.dot((t * t).astype(BF16), p1_ref[...], preferred_element_type=F32)
        r = lax.rsqrt(ms + NORM_EPS)
        r_hi = r.astype(BF16)
        r_lo = (r - r_hi.astype(F32)).astype(BF16)
        rr = jnp.dot(jnp.concatenate([r_hi, r_lo], axis=-1), p2_ref[...], preferred_element_type=F32)
        o_ref[0, :, j * d:(j + 1) * d] = (t * rr * gain_ref[...]).astype(BF16)
    v = jnp.dot(h, w_ref[:, 2 * d:3 * d], preferred_element_type=F32)
    o_ref[0, :, 2 * d:3 * d] = v.astype(BF16)


def _qkv(x, shift, scale, gain, w_qkv, q_gain, k_gain, tm):
    b, l, d = x.shape
    heads = d // HEAD_DIM
    per_batch = shift.shape[0] == b
    mod_map = (lambda bi, i: (bi, 0, 0)) if per_batch else (lambda bi, i: (0, 0, 0))
    p1 = np.zeros((d, LANES), np.float32)
    p1[np.arange(d), np.arange(d) // HEAD_DIM] = 1.0 / HEAD_DIM
    p2 = np.zeros((2 * LANES, d), np.float32)
    p2[np.arange(d) // HEAD_DIM, np.arange(d)] = 1.0
    p2[LANES + np.arange(d) // HEAD_DIM, np.arange(d)] = 1.0
    qg = (jnp.tile(q_gain, heads) * (HEAD_DIM ** -0.5 * LOG2E)).reshape(1, d)
    kg = jnp.tile(k_gain, heads).reshape(1, d)
    return pl.pallas_call(
        _qkv_kernel,
        out_shape=jax.ShapeDtypeStruct((b, l, 3 * d), BF16),
        grid=(b, l // tm),
        in_specs=[
            pl.BlockSpec((1, tm, d), lambda bi, i: (bi, i, 0)),
            pl.BlockSpec((1, 1, d), mod_map),
            pl.BlockSpec((1, 1, d), mod_map),
            _const_spec((1, d)),
            _const_spec((d, 3 * d)),
            _const_spec((1, d)),
            _const_spec((1, d)),
            _const_spec((d, LANES)),
            _const_spec((2 * LANES, d)),
        ],
        out_specs=pl.BlockSpec((1, tm, 3 * d), lambda bi, i: (bi, i, 0)),
        compiler_params=_params(("parallel", "parallel")),
        name="qkv",
    )(x, shift, scale, gain.reshape(1, d), w_qkv.astype(BF16), qg, kg,
      jnp.asarray(p1, BF16), jnp.asarray(p2, BF16))


def _toeplitz_kernel(r_ref, oh_ref, m_ref, o_ref):
    o_ref[...] = jnp.dot(r_ref[...], oh_ref[...], preferred_element_type=F32, precision=HIGHEST) + m_ref[...]


def _col_bias(rpb):
    heads, n_dr, n_dc = rpb.shape
    k_pad = 32
    qc = np.arange(GRID_W)[:, None]
    kc = np.arange(GRID_W)[None, :]
    c0 = np.clip(qc - WIN_W // 2, 0, GRID_W - WIN_W)
    valid = (kc >= c0) & (kc < c0 + WIN_W)
    dc = kc - qc + (WIN_W - 1)
    onehot = np.zeros((k_pad, GRID_W, GRID_W), np.float32)
    for j in range(n_dc):
        onehot[j] = ((dc == j) & valid).astype(np.float32)
    onehot = onehot.reshape(k_pad, GRID_W * GRID_W)
    mask = np.where(valid, 0.0, NEG).astype(np.float32).reshape(1, GRID_W * GRID_W)
    rows = heads * n_dr
    rows_pad = -(-rows // 8) * 8
    r2 = jnp.zeros((rows_pad, k_pad), F32).at[:rows, :n_dc].set(rpb.reshape(rows, n_dc))
    t = pl.pallas_call(
        _toeplitz_kernel,
        out_shape=jax.ShapeDtypeStruct((rows_pad, GRID_W * GRID_W), F32),
        name="rpb_toeplitz",
    )(r2, jnp.asarray(onehot), jnp.asarray(mask))
    return t[:rows].reshape(heads, n_dr, GRID_W, GRID_W)


def _block_bias(t, n_rows):
    heads = t.shape[0]
    kh = min(WIN_H, n_rows)
    masked = jnp.full((heads, GRID_W, GRID_W), NEG, F32)
    classes = []
    for q0, k0 in ((0, 0), (Q_ROWS, Q_ROWS - WIN_H // 2), (n_rows - Q_ROWS, n_rows - K_ROWS)):
        strips = []
        for qr in range(Q_ROWS):
            r = q0 + qr
            r0 = min(max(r - kh // 2, 0), n_rows - kh)
            blocks = []
            for kr in range(K_ROWS):
                kk = k0 + kr
                blocks.append(t[:, kk - r + WIN_H - 1] if r0 <= kk < r0 + kh else masked)
            strips.append(jnp.concatenate(blocks, axis=-1))
        classes.append(jnp.concatenate(strips, axis=-2))
    return (jnp.stack(classes) * LOG2E).astype(BF16)


def _na_stages(q_ref, k_ref, v_ref, kc_ref, vc_ref, bias_ref, o_ref, k_scr, v_scr, p_scr, s_new, s_old,
               *, blk_new, blk_old):
    tq = q_ref.shape[1]
    tk = bias_ref.shape[-1]
    n_ctx = kc_ref.shape[1]
    l = k_ref.shape[1]
    first_head = lax.broadcasted_iota(jnp.int32, (1, LANES), 1) < HEAD_DIM

    def window_start(blk):
        return pl.multiple_of(jnp.clip(blk * tq - (tk - tq) // 2, 0, l - tk), tq // 2)

    k_scr[:tk] = k_ref[0, pl.ds(window_start(blk_new), tk), :]
    k_scr[tk:] = kc_ref[0]
    q = q_ref[0]
    zero = jnp.zeros_like(q)
    q2 = jnp.concatenate([jnp.where(first_head, q, zero), jnp.where(first_head, zero, q)], axis=0)
    s_new[...] = lax.dot_general(q2, k_scr[...], (((1,), (1,)), ((), ())), preferred_element_type=F32)

    v_scr[:tk, :LANES] = v_ref[0, pl.ds(window_start(blk_old), tk), :]
    v_scr[tk:, :LANES] = vc_ref[0]
    v_scr[:, LANES:] = jnp.ones((tk + n_ctx, LANES), BF16)
    for r0 in range(0, tq, NA_PV_ROWS):
        heads = []
        for h in range(2):
            for r in range(r0, r0 + NA_PV_ROWS, NA_CHUNK):
                rows = slice(h * tq + r, h * tq + r + NA_CHUNK)
                s_loc = s_old[rows, :tk] + bias_ref[0, h, r:r + NA_CHUNK, :].astype(F32)
                s_ctx = s_old[rows, tk:]
                m = jnp.maximum(jnp.max(s_loc, axis=-1, keepdims=True), jnp.max(s_ctx, axis=-1, keepdims=True))
                p_scr[rows, :tk] = jnp.exp2((s_loc - m).astype(BF16))
                p_scr[rows, tk:] = jnp.exp2((s_ctx - m).astype(BF16))
            o = jnp.dot(p_scr[h * tq + r0:h * tq + r0 + NA_PV_ROWS], v_scr[...], preferred_element_type=F32)
            heads.append(o[:, :LANES] / o[:, LANES:])
        o_ref[0, r0:r0 + NA_PV_ROWS] = jnp.where(first_head, heads[0], heads[1]).astype(BF16)


def _na_kernel(q_ref, k_ref, v_ref, kc_ref, vc_ref, bias_ref, o_ref, k_scr, v_scr, p_scr, s0, s1, *, nb):
    t = pl.program_id(0)
    last = pl.num_programs(0) - 2
    blocks = dict(blk_new=jnp.minimum(t, last) % nb, blk_old=jnp.clip(t - 1, 0, last) % nb)
    refs = (q_ref, k_ref, v_ref, kc_ref, vc_ref, bias_ref, o_ref, k_scr, v_scr, p_scr)

    @pl.when(t == 0)
    def _():
        s1[...] = jnp.zeros(s1.shape, F32)

    @pl.when(t % 2 == 0)
    def _():
        _na_stages(*refs, s0, s1, **blocks)

    @pl.when(t % 2 == 1)
    def _():
        _na_stages(*refs, s1, s0, **blocks)


def _neighbourhood_attention(qkv, qkv_ctx, bias):
    b, l, d3 = qkv.shape
    d = d3 // 3
    n_ctx = qkv_ctx.shape[1]
    pairs = d // LANES
    tq, tk = bias.shape[-2:]
    nb = l // tq

    steps = b * pairs * nb

    def decode(t):
        return t // (pairs * nb), (t // nb) % pairs, t % nb

    def lagged(t, lag):
        return decode(jnp.clip(t - lag, 0, steps - 1))

    def q_map(t):
        bi, hp, i = lagged(t, 0)
        return bi, i, hp

    def kv_map(which, lag):
        def index(t):
            bi, hp, _ = lagged(t, lag)
            return bi, 0, which * pairs + hp
        return index

    def bias_map(t):
        _, hp, i = lagged(t, 1)
        return jnp.where(i == 0, 0, jnp.where(i == nb - 1, 2, 1)), hp, 0, 0

    def out_map(t):
        bi, hp, i = lagged(t, 1)
        return bi, i, hp

    n_keys = tk + n_ctx
    return pl.pallas_call(
        functools.partial(_na_kernel, nb=nb),
        out_shape=jax.ShapeDtypeStruct((b, l, d), BF16),
        grid=(steps + 1,),
        in_specs=[
            pl.BlockSpec((1, tq, LANES), q_map),
            pl.BlockSpec((1, l, LANES), kv_map(1, 0)),
            pl.BlockSpec((1, l, LANES), kv_map(2, 1)),
            pl.BlockSpec((1, n_ctx, LANES), kv_map(1, 0)),
            pl.BlockSpec((1, n_ctx, LANES), kv_map(2, 1)),
            pl.BlockSpec((1, 2, tq, tk), bias_map),
        ],
        out_specs=pl.BlockSpec((1, tq, LANES), out_map),
        scratch_shapes=[
            pltpu.VMEM((n_keys, LANES), BF16),
            pltpu.VMEM((n_keys, 2 * LANES), BF16),
            pltpu.VMEM((2 * tq, n_keys), BF16),
            pltpu.VMEM((2 * tq, n_keys), F32), pltpu.VMEM((2 * tq, n_keys), F32),
        ],
        compiler_params=_params(("arbitrary",)),
        name="na_attention",
    )(qkv, qkv, qkv, qkv_ctx, qkv_ctx, bias)


def _halo_rows(prev_ref, main_ref, next_ref):
    return jnp.concatenate([prev_ref[0], main_ref[0], next_ref[0]], axis=0)


def _zero_outside(u, axis):
    i = pl.program_id(axis)
    rows = u.shape[0]
    keep_top = jnp.where(i == 0, 0.0, 1.0).astype(F32)
    keep_bottom = jnp.where(i == pl.num_programs(axis) - 1, 0.0, 1.0).astype(F32)
    return jnp.concatenate([u[:HALO] * keep_top, u[HALO:rows - HALO], u[rows - HALO:] * keep_bottom], axis=0)


def _shift_rows(g, rows):
    return pltpu.roll(g, 1, 0), pltpu.roll(g, rows - 1, 0)


def _erf(x):
    return lax.erf(x)


def _post_kernel(mp_ref, mm_ref, mn_ref, xp_ref, xm_ref, xn_ref, wmix_ref, bmix_ref, g1_ref, n2_ref,
                 sh_ref, sc_ref, g2_ref, wup_ref, cw_ref, cb_ref, wdn_ref, o_ref, *, ff_chunk):
    rows = mm_ref.shape[1] + 2 * HALO
    d_ff = wdn_ref.shape[0]
    mix = _halo_rows(mp_ref, mm_ref, mn_ref)
    x = _halo_rows(xp_ref, xm_ref, xn_ref)
    y = jnp.dot(mix, wmix_ref[...], preferred_element_type=F32) + bmix_ref[...]
    x1 = x + g1_ref[0] * y
    h = _rms_mod(x1, n2_ref[...], sh_ref[0], sc_ref[0]).astype(BF16)
    acc = jnp.zeros((rows - 2 * HALO, x.shape[-1]), F32)
    for c in range(0, d_ff, ff_chunk):
        a = jnp.dot(h, wup_ref[:, c:c + ff_chunk], preferred_element_type=F32)
        g = _zero_outside(jnp.dot(h, wup_ref[:, d_ff + c:d_ff + c + ff_chunk], preferred_element_type=F32), 1)
        g_prev, g_next = _shift_rows(g, rows)
        cw = cw_ref[:, c:c + ff_chunk]
        gc = g_prev * cw[0:1] + g * cw[1:2] + g_next * cw[2:3] + cb_ref[:, c:c + ff_chunk]
        gc = gc[HALO:rows - HALO]
        u = a[HALO:rows - HALO] * (0.5 * gc * (1.0 + _erf(gc * (2.0 ** -0.5))))
        acc = acc + jnp.dot(u.astype(BF16), wdn_ref[c:c + ff_chunk, :], preferred_element_type=F32)
    o_ref[0] = x1[HALO:rows - HALO] + g2_ref[0] * acc


def _halo_specs(tm, l, d):
    nblk = tm // HALO
    last = l // HALO - 1
    return [
        pl.BlockSpec((1, HALO, d), lambda bi, i: (bi, jnp.maximum(i * nblk - 1, 0), 0)),
        pl.BlockSpec((1, tm, d), lambda bi, i: (bi, i, 0)),
        pl.BlockSpec((1, HALO, d), lambda bi, i: (bi, jnp.minimum((i + 1) * nblk, last), 0)),
    ]


def _post(mix, x, w_mix, b_mix, g1, norm2, sh2, sc2, g2, w_up, conv_w, conv_b, w_down, tm, ff_chunk):
    b, l, d = x.shape
    d_ff = w_down.shape[0]
    mod = pl.BlockSpec((1, 1, d), lambda bi, i: (bi, 0, 0))
    return pl.pallas_call(
        functools.partial(_post_kernel, ff_chunk=ff_chunk),
        out_shape=jax.ShapeDtypeStruct((b, l, d), F32),
        grid=(b, l // tm),
        in_specs=_halo_specs(tm, l, d) + _halo_specs(tm, l, d) + [
            _const_spec((d, d)), _const_spec((1, d)), mod, _const_spec((1, d)), mod, mod, mod,
            _const_spec((d, 2 * d_ff)), _const_spec((3, d_ff)), _const_spec((1, d_ff)), _const_spec((d_ff, d)),
        ],
        out_specs=pl.BlockSpec((1, tm, d), lambda bi, i: (bi, i, 0)),
        compiler_params=_params(("parallel", "parallel")),
        name="post_ffn",
    )(mix, mix, mix, x, x, x, w_mix.astype(BF16), b_mix.reshape(1, d), g1, norm2.reshape(1, d), sh2, sc2, g2,
      w_up.astype(BF16), conv_w, conv_b.reshape(1, d_ff), w_down.astype(BF16))


def _hy_in_kernel(xp_ref, xm_ref, xn_ref, sh_ref, sc_ref, g_ref, w_ref, b_ref, cw_ref, cb_ref, x0_ref, z_ref):
    rows = xm_ref.shape[1] + 2 * HALO
    d = xm_ref.shape[-1]
    h = _rms_mod(_halo_rows(xp_ref, xm_ref, xn_ref), g_ref[...], sh_ref[0], sc_ref[0]).astype(BF16)
    parts = []
    for j in range(3):
        cols = slice(j * d, (j + 1) * d)
        u = _zero_outside(jnp.dot(h, w_ref[:, cols], preferred_element_type=F32) + b_ref[:, cols], 1)
        u_prev, u_next = _shift_rows(u, rows)
        cw = cw_ref[:, cols]
        uc = u_prev * cw[0:1] + u * cw[1:2] + u_next * cw[2:3] + cb_ref[:, cols]
        parts.append(uc[HALO:rows - HALO])
    x0_ref[0] = parts[0].astype(BF16)
    z_ref[0] = (parts[2] * parts[1]).astype(BF16)


def _hy_in(x, shift, scale, gain, w_in, b_in, short_w, short_b, tm):
    b, l, d = x.shape
    mod = pl.BlockSpec((1, 1, d), lambda bi, i: (bi, 0, 0))
    out = pl.BlockSpec((1, tm, d), lambda bi, i: (bi, i, 0))
    return pl.pallas_call(
        _hy_in_kernel,
        out_shape=(jax.ShapeDtypeStruct((b, l, d), BF16), jax.ShapeDtypeStruct((b, l, d), BF16)),
        grid=(b, l // tm),
        in_specs=_halo_specs(tm, l, d) + [
            mod, mod, _const_spec((1, d)), _const_spec((d, 3 * d)), _const_spec((1, 3 * d)),
            _const_spec((3, 3 * d)), _const_spec((1, 3 * d)),
        ],
        out_specs=(out, out),
        compiler_params=_params(("parallel", "parallel")),
        name="hyena_in",
    )(x, x, x, shift, scale, gain.reshape(1, d), w_in.astype(BF16), b_in.reshape(1, 3 * d),
      short_w, short_b.reshape(1, 3 * d))


def _filter_kernel(bands_ref, w1t_ref, w1c_ref, w1s_ref, b1_ref, w2_ref, b2_ref, w3_ref, b3_ref, wo_ref,
                   freq_ref, delta_ref, hf_ref, hb_ref, ss_ref, *, seq_len):
    tl = hf_ref.shape[0]
    d = hf_ref.shape[1]
    pos = (pl.program_id(0) * tl + lax.broadcasted_iota(jnp.int32, (tl, 1), 0)).astype(F32)
    t = pos * (1.0 / (seq_len - 1))
    ang = bands_ref[...] * ((2.0 * math.pi / seq_len) * pos)
    freq = freq_ref[...]
    dot = functools.partial(jnp.dot, preferred_element_type=F32, precision=HIGHEST)
    pre = t * w1t_ref[...] + dot(jnp.cos(ang), w1c_ref[...]) - dot(jnp.sin(ang), w1s_ref[...])
    hdn = jnp.sin(freq * (pre + b1_ref[...]))
    hdn = jnp.sin(freq * (dot(hdn, w2_ref[...]) + b2_ref[...]))
    hdn = jnp.sin(freq * (dot(hdn, w3_ref[...]) + b3_ref[...]))
    decay = jnp.exp(-t * delta_ref[...])
    hf = dot(hdn, wo_ref[:, :d]) * decay
    hb = dot(hdn, wo_ref[:, d:]) * decay
    hf_ref[...] = hf
    hb_ref[...] = jnp.where(pos > 0.0, hb, 0.0)

    @pl.when(pl.program_id(0) == 0)
    def _():
        ss_ref[...] = jnp.zeros_like(ss_ref)

    ss_ref[...] += jnp.sum(hf * hf + hb * hb, axis=0, keepdims=True)


def _hyena_filter(seq_len, d, w1, b1, w2, b2, w3, b3, w_out, freq, tl):
    width = w2.shape[0]
    bands = np.linspace(1e-4, HY_BANDS - 1, HY_BANDS, dtype=np.float32).reshape(1, HY_BANDS)
    deltas = np.abs(np.linspace(math.log(HY_DECAY_TARGET) / HY_SHORT_DECAY_PCT,
                                math.log(HY_DECAY_TARGET) / HY_LONG_DECAY_PCT, d, dtype=np.float32)).reshape(1, d)
    small = [
        (1, HY_BANDS), (1, width), (HY_BANDS, width), (HY_BANDS, width), (1, width), (width, width), (1, width),
        (width, width), (1, width), (width, 2 * d), (1, width), (1, d),
    ]
    return pl.pallas_call(
        functools.partial(_filter_kernel, seq_len=seq_len),
        out_shape=(jax.ShapeDtypeStruct((seq_len, d), F32), jax.ShapeDtypeStruct((seq_len, d), F32),
                   jax.ShapeDtypeStruct((1, d), F32)),
        grid=(seq_len // tl,),
        in_specs=[pl.BlockSpec(s, lambda i: (0, 0)) for s in small],
        out_specs=(pl.BlockSpec((tl, d), lambda i: (i, 0)), pl.BlockSpec((tl, d), lambda i: (i, 0)),
                   pl.BlockSpec((1, d), lambda i: (0, 0))),
        compiler_params=_params(("arbitrary",)),
        name="hyena_filter",
    )(jnp.asarray(bands), w1[0:1], w1[1:1 + HY_BANDS], w1[1 + HY_BANDS:], b1.reshape(1, width), w2,
      b2.reshape(1, width), w3, b3.reshape(1, width), w_out, freq.reshape(1, width), jnp.asarray(deltas))


def _dft_tables(n1):
    n = n1 * n1
    half = n1 // 2
    idx = np.arange(n1)
    ang = 2.0 * np.pi * np.outer(idx, idx) / n1
    c, s = np.cos(ang), np.sin(ang)
    fa_pair = np.block([[c[:, :half], s[:, :half]], [-s[:, :half], c[:, :half]]])
    zero = np.zeros((n1, half))
    fa_real = np.block([[c[:, :half], zero], [-s[:, :half], zero], [zero, c[:, :half]], [zero, -s[:, :half]]])
    fa_inv = np.block([[c[:half], -s[:half]], [s[:half], c[:half]]]) / n
    k1 = idx[:, None, None]
    k2 = idx[None, :, None]
    n2 = idx[None, None, :]
    m = (n2 * (k1 + n1 * k2)) % n
    gang = 2.0 * np.pi * m / n
    gr, gi = np.cos(gang), -np.sin(gang)
    g_fwd = np.concatenate([np.concatenate([gr, -gi], axis=2), np.concatenate([gi, gr], axis=2)], axis=1)
    g_inv = np.transpose(g_fwd, (0, 2, 1))
    to = lambda a: jnp.asarray(a.astype(np.float32), BF16)
    return to(fa_pair), to(fa_real), to(fa_inv), to(g_fwd), to(g_inv)


def _stage_a_kernel(m_ref, z_ref, o_ref):
    o_ref[0] = jnp.dot(m_ref[...], z_ref[0], preferred_element_type=F32).astype(o_ref.dtype)


def _stage_a(mat, z, wb):
    p, k, w = z.shape
    m = mat.shape[0]
    return pl.pallas_call(
        _stage_a_kernel,
        out_shape=jax.ShapeDtypeStruct((p, m, w), BF16),
        grid=(p, w // wb),
        in_specs=[_const_spec((m, k)), pl.BlockSpec((1, k, wb), lambda pi, j: (pi, 0, j))],
        out_specs=pl.BlockSpec((1, m, wb), lambda pi, j: (pi, 0, j)),
        compiler_params=_params(("parallel", "parallel")),
        name="dft_stage_a",
    )(mat, z)


def _stacked(ref):
    _, _, _, n2, d = ref.shape
    return ref[0, :, 0].reshape(2 * n2, d)


def _pack_pair(hi, lo):
    hi_bits = pltpu.bitcast(hi.astype(BF16).astype(F32), U32)
    lo_bits = pltpu.bitcast(lo.astype(BF16).astype(F32), U32)
    return hi_bits | (lo_bits >> 16)


def _unpack_pair(word):
    return (pltpu.bitcast(word & jnp.uint32(0xFFFF0000), F32), pltpu.bitcast(word << 16, F32))


def _pitch(n2):
    return n2 + 8


def _filter_spectrum_kernel(hf_ref, hb_ref, ss_ref, fa_ref, g_ref, o_ref, a_scr):
    n1 = fa_ref.shape[0] // 4
    n2 = g_ref.shape[1] // 2
    half = hf_ref.shape[0] // n2
    pitch = _pitch(n2)
    kb = g_ref.shape[0]

    @pl.when(pl.program_id(1) == 0)
    def _():
        def body(j, carry):
            rows = jnp.concatenate([hf_ref[pl.ds(j, half, stride=n2), :], hb_ref[pl.ds(j, half, stride=n2), :]], axis=0)
            r = jnp.dot(fa_ref[...], rows.astype(BF16), preferred_element_type=F32)
            a_scr[0, pl.ds(j, n1, stride=pitch), :] = _pack_pair(r[:n1], r[n1:2 * n1])
            a_scr[1, pl.ds(j, n1, stride=pitch), :] = _pack_pair(r[2 * n1:3 * n1], r[3 * n1:])
            return carry

        lax.fori_loop(0, n2, body, 0)

    scale = lax.rsqrt(ss_ref[...] + NORM_EPS)
    for k in range(kb):
        row0 = pl.multiple_of((pl.program_id(1) * kb + k) * pitch, 8)
        fr, fi = _unpack_pair(a_scr[0, pl.ds(row0, n2), :])
        br, bi = _unpack_pair(a_scr[1, pl.ds(row0, n2), :])
        a = jnp.concatenate([jnp.concatenate([fr, fi], axis=0), jnp.concatenate([br, bi], axis=0)], axis=1)
        p = jnp.dot(g_ref[k], a.astype(BF16), preferred_element_type=F32)
        lanes = p.shape[1] // 2
        o_ref[k, :n2] = (p[:n2, :lanes] + p[:n2, lanes:]) * scale
        o_ref[k, n2:] = (p[n2:, :lanes] - p[n2:, lanes:]) * scale


def _filter_spectrum(fa_filt, g_fwd, hf, hb, energy, kb):
    n1 = g_fwd.shape[0]
    n2 = g_fwd.shape[1] // 2
    l, d = hf.shape
    tc = LANES
    half_spec = pl.BlockSpec((l, tc), lambda c, k: (0, c))
    return pl.pallas_call(
        _filter_spectrum_kernel,
        out_shape=jax.ShapeDtypeStruct((n1, 2 * n2, d), F32),
        grid=(d // tc, n1 // kb),
        in_specs=[
            half_spec, half_spec, pl.BlockSpec((1, tc), lambda c, k: (0, c)),
            _const_spec(fa_filt.shape),
            pl.BlockSpec((kb, 2 * n2, 2 * n2), lambda c, k: (k, 0, 0)),
        ],
        out_specs=pl.BlockSpec((kb, 2 * n2, tc), lambda c, k: (k, 0, c)),
        scratch_shapes=[pltpu.VMEM((2, n1 * _pitch(n2), tc), U32)],
        compiler_params=_params(("parallel", "arbitrary")),
        name="filter_spectrum",
    )(hf, hb, energy, fa_filt, g_fwd)


def _conv_mid_kernel(gf_ref, gi_ref, a_ref, hf_ref, t_ref):
    n2 = a_ref.shape[3]
    d = a_ref.shape[4]
    x = jnp.dot(gf_ref[0], _stacked(a_ref), preferred_element_type=F32)
    xr, xi = x[:n2], x[n2:]
    hr, hi = hf_ref[0, :n2], hf_ref[0, n2:]
    y = jnp.concatenate([xr * hr - xi * hi, xr * hi + xi * hr], axis=0).astype(BF16)
    t = jnp.dot(gi_ref[0], y, preferred_element_type=F32)
    t_ref[0, :, 0] = t.astype(BF16).reshape(2, n2, d)


def _conv_mid(g_fwd, g_inv, a, spec):
    n1 = g_fwd.shape[0]
    p, _, _, n2, d = a.shape
    blk = pl.BlockSpec((1, 2, 1, n2, d), lambda k, pi: (pi, 0, k, 0, 0))
    g_blk = pl.BlockSpec((1, 2 * n2, 2 * n2), lambda k, pi: (k, 0, 0))
    return pl.pallas_call(
        _conv_mid_kernel,
        out_shape=jax.ShapeDtypeStruct(a.shape, BF16),
        grid=(n1, p),
        in_specs=[g_blk, g_blk, blk, pl.BlockSpec((1, 2 * n2, d), lambda k, pi: (k, 0, 0))],
        out_specs=blk,
        compiler_params=_params(("parallel", "arbitrary")),
        name="conv_mid",
    )(g_fwd, g_inv, a, spec)


def _conv_out_kernel(m_ref, t_ref, z_ref, x0_ref, db_ref, o_ref):
    y = jnp.dot(m_ref[...], t_ref[0], preferred_element_type=F32)
    z = z_ref[0].astype(F32)
    o_ref[0] = (x0_ref[0].astype(F32) * (y + z * db_ref[...])).astype(BF16)


def _conv_out(mat, t, z, x0, d_bias_row, wb):
    p, k, w = t.shape
    m = mat.shape[0]
    blk = pl.BlockSpec((1, m, wb), lambda pi, j: (pi, 0, j))
    return pl.pallas_call(
        _conv_out_kernel,
        out_shape=jax.ShapeDtypeStruct((p, m, w), BF16),
        grid=(p, w // wb),
        in_specs=[_const_spec((m, k)), pl.BlockSpec((1, k, wb), lambda pi, j: (pi, 0, j)), blk, blk,
                  _const_spec((1, wb))],
        out_specs=blk,
        compiler_params=_params(("parallel", "parallel")),
        name="dft_stage_out",
    )(mat, t, z, x0, d_bias_row)


def _long_conv_gate(x0, z, hf, hb, energy, d_bias, wb):
    b, l, d = z.shape
    n1 = math.isqrt(2 * l)
    assert n1 * n1 == 2 * l and b % 2 == 0
    p = b // 2
    w = n1 * d
    fa_pair, fa_real, fa_inv, g_fwd, g_inv = _dft_tables(n1)
    spec = _filter_spectrum(fa_real, g_fwd, hf, hb, energy, min(16, n1))
    zp = z.reshape(p, n1, w)
    a = _stage_a(fa_pair, zp, wb)
    t = _conv_mid(g_fwd, g_inv, a.reshape(p, 2, n1, n1, d), spec)
    d_row = jnp.tile(d_bias, wb // d).reshape(1, wb)
    out = _conv_out(fa_inv, t.reshape(p, 2 * n1, w), zp, x0.reshape(p, n1, w), d_row, wb)
    return out.reshape(b, l, d)


def _mod_rows(mod, lo, hi, d):
    m = mod[lo:hi]
    return [m[:, None, j * d:(j + 1) * d] for j in range(N_MOD)]


def kernel(x, c, ctx, c_ctx, l0_w_mod, l0_b_mod, l0_norm1, l0_norm2, l0_na_w_qkv, l0_na_q_gain, l0_na_k_gain, l0_na_rpb, l0_na_w_o, l0_ffn_w_up, l0_ffn_conv_w, l0_ffn_conv_b, l0_ffn_w_down, l1_w_mod, l1_b_mod, l1_norm1, l1_norm2, l1_hy_w_in, l1_hy_b_in, l1_hy_short_w, l1_hy_short_b, l1_hy_f_w1, l1_hy_f_b1, l1_hy_f_w2, l1_hy_f_b2, l1_hy_f_w3, l1_hy_f_b3, l1_hy_f_wout, l1_hy_f_freq, l1_hy_d_bias, l1_hy_w_out, l1_hy_b_out, l1_ffn_w_up, l1_ffn_conv_w, l1_ffn_conv_b, l1_ffn_w_down):
    b, l, d = x.shape
    n_ctx = ctx.shape[1]
    n_rows = l // GRID_W
    assert n_rows >= K_ROWS and n_rows % Q_ROWS == 0
    tm = min(512, l)
    ff_chunk = 256
    wb = 8 * d

    cond = jnp.zeros((8, d), F32).at[:b].set(c).at[b].set(c_ctx)

    mod = _adaln(cond, l0_w_mod, l0_b_mod)
    sh1, sc1, g1, sh2, sc2, g2 = _mod_rows(mod, 0, b, d)
    csh1, csc1 = _mod_rows(mod, b, b + 1, d)[:2]
    qkv = _qkv(x, sh1, sc1, l0_norm1, l0_na_w_qkv, l0_na_q_gain, l0_na_k_gain, tm)
    qkv_ctx = _qkv(ctx, csh1, csc1, l0_norm1, l0_na_w_qkv, l0_na_q_gain, l0_na_k_gain, n_ctx)
    bias = _block_bias(_col_bias(l0_na_rpb), n_rows)
    attn = _neighbourhood_attention(qkv, qkv_ctx, bias)
    x = _post(attn, x, l0_na_w_o, jnp.zeros((d,), F32), g1, l0_norm2, sh2, sc2, g2,
              l0_ffn_w_up, l0_ffn_conv_w, l0_ffn_conv_b, l0_ffn_w_down, tm, ff_chunk)

    mod = _adaln(cond, l1_w_mod, l1_b_mod)
    sh1, sc1, g1, sh2, sc2, g2 = _mod_rows(mod, 0, b, d)
    x0, z = _hy_in(x, sh1, sc1, l1_norm1, l1_hy_w_in, l1_hy_b_in, l1_hy_short_w, l1_hy_short_b, tm)
    hf, hb, energy = _hyena_filter(l, d, l1_hy_f_w1, l1_hy_f_b1, l1_hy_f_w2, l1_hy_f_b2, l1_hy_f_w3, l1_hy_f_b3,
                                   l1_hy_f_wout, l1_hy_f_freq, min(1024, l))
    gated = _long_conv_gate(x0, z, hf, hb, energy, l1_hy_d_bias, wb)
    x = _post(gated, x, l1_hy_w_out, l1_hy_b_out, g1, l1_norm2, sh2, sc2, g2,
              l1_ffn_w_up, l1_ffn_conv_w, l1_ffn_conv_b, l1_ffn_w_down, tm, ff_chunk)
    return x
```

```python
import functools
import math

import numpy as np
import jax
import jax.numpy as jnp
from jax import lax
from jax.experimental import pallas as pl
from jax.experimental.pallas import tpu as pltpu

F32 = jnp.float32
BF16 = jnp.bfloat16
U32 = jnp.uint32
HIGHEST = lax.Precision.HIGHEST

NORM_EPS = 1e-6
N_MOD = 6
HEAD_DIM = 64
GRID_W = 64
WIN_H = 8
WIN_W = 16
HY_BANDS = 16
HY_DECAY_TARGET = 1e-2
HY_SHORT_DECAY_PCT = 0.3
HY_LONG_DECAY_PCT = 1.5

LANES = 128
HALO = 16
Q_ROWS = 8
K_ROWS = 16
NA_CHUNK = 32
NA_PV_ROWS = 256
DFT_UNROLL = 4
LOG2E = math.log2(math.e)
NEG = -1e30
VMEM_LIMIT = 56 * 1024 * 1024


def _params(sem):
    return pltpu.CompilerParams(dimension_semantics=sem, vmem_limit_bytes=VMEM_LIMIT)


def _const_spec(shape):
    return pl.BlockSpec(shape, lambda *_: (0,) * len(shape), pipeline_mode=pl.Buffered(1))


def _rms_mod(x, gain, shift, scale):
    ms = jnp.mean(x * x, axis=-1, keepdims=True)
    y = x * lax.rsqrt(ms + NORM_EPS) * gain
    return y * (1.0 + scale) + shift


def _adaln_kernel(c_ref, w_ref, b_ref, o_ref):
    c = c_ref[...]
    s = c / (1.0 + jnp.exp(-c))
    o_ref[...] = jnp.dot(s, w_ref[...], preferred_element_type=F32, precision=HIGHEST) + b_ref[...]


def _adaln(cond, w_mod, b_mod):
    rows, d = cond.shape
    n = w_mod.shape[1]
    tn = d
    return pl.pallas_call(
        _adaln_kernel,
        out_shape=jax.ShapeDtypeStruct((rows, n), F32),
        grid=(n // tn,),
        in_specs=[
            pl.BlockSpec((rows, d), lambda j: (0, 0)),
            pl.BlockSpec((d, tn), lambda j: (0, j)),
            pl.BlockSpec((1, tn), lambda j: (0, j)),
        ],
        out_specs=pl.BlockSpec((rows, tn), lambda j: (0, j)),
        compiler_params=_params(("arbitrary",)),
        name="adaln",
    )(cond, w_mod, b_mod.reshape(1, n))


def _qkv_kernel(x_ref, sh_ref, sc_ref, g_ref, w_ref, qg_ref, kg_ref, p1_ref, p2_ref, o_ref):
    d = x_ref.shape[-1]
    h = _rms_mod(x_ref[0], g_ref[...], sh_ref[0], sc_ref[0]).astype(BF16)
    for j, gain_ref in ((0, qg_ref), (1, kg_ref)):
        t = jnp.dot(h, w_ref[:, j * d:(j + 1) * d], preferred_element_type=F32)
        ms = jnp.dot((t * t).astype(BF16), p1_ref[...], preferred_element_type=F32)
        r = lax.rsqrt(ms + NORM_EPS)
        r_hi = r.astype(BF16)
        r_lo = (r - r_hi.astype(F32)).astype(BF16)
        rr = jnp.dot(jnp.concatenate([r_hi, r_lo], axis=-1), p2_ref[...], preferred_element_type=F32)
        o_ref[0, :, j * d:(j + 1) * d] = (t * rr * gain_ref[...]).astype(BF16)
    v = jnp.dot(h, w_ref[:, 2 * d:3 * d], preferred_element_type=F32)
    o_ref[0, :, 2 * d:3 * d] = v.astype(BF16)


def _qkv(x, shift, scale, gain, w_qkv, q_gain, k_gain, tm):
    b, l, d = x.shape
    heads = d // HEAD_DIM
    per_batch = shift.shape[0] == b
    mod_map = (lambda bi, i: (bi, 0, 0)) if per_batch else (lambda bi, i: (0, 0, 0))
    p1 = np.zeros((d, LANES), np.float32)
    p1[np.arange(d), np.arange(d) // HEAD_DIM] = 1.0 / HEAD_DIM
    p2 = np.zeros((2 * LANES, d), np.float32)
    p2[np.arange(d) // HEAD_DIM, np.arange(d)] = 1.0
    p2[LANES + np.arange(d) // HEAD_DIM, np.arange(d)] = 1.0
    qg = (jnp.tile(q_gain, heads) * (HEAD_DIM ** -0.5 * LOG2E)).reshape(1, d)
    kg = jnp.tile(k_gain, heads).reshape(1, d)
    return pl.pallas_call(
        _qkv_kernel,
        out_shape=jax.ShapeDtypeStruct((b, l, 3 * d), BF16),
        grid=(b, l // tm),
        in_specs=[
            pl.BlockSpec((1, tm, d), lambda bi, i: (bi, i, 0)),
            pl.BlockSpec((1, 1, d), mod_map),
            pl.BlockSpec((1, 1, d), mod_map),
            _const_spec((1, d)),
            _const_spec((d, 3 * d)),
            _const_spec((1, d)),
            _const_spec((1, d)),
            _const_spec((d, LANES)),
            _const_spec((2 * LANES, d)),
        ],
        out_specs=pl.BlockSpec((1, tm, 3 * d), lambda bi, i: (bi, i, 0)),
        compiler_params=_params(("parallel", "parallel")),
        name="qkv",
    )(x, shift, scale, gain.reshape(1, d), w_qkv.astype(BF16), qg, kg,
      jnp.asarray(p1, BF16), jnp.asarray(p2, BF16))


def _toeplitz_kernel(r_ref, oh_ref, m_ref, o_ref):
    o_ref[...] = jnp.dot(r_ref[...], oh_ref[...], preferred_element_type=F32, precision=HIGHEST) + m_ref[...]


def _col_bias(rpb):
    heads, n_dr, n_dc = rpb.shape
    k_pad = 32
    qc = np.arange(GRID_W)[:, None]
    kc = np.arange(GRID_W)[None, :]
    c0 = np.clip(qc - WIN_W // 2, 0, GRID_W - WIN_W)
    valid = (kc >= c0) & (kc < c0 + WIN_W)
    dc = kc - qc + (WIN_W - 1)
    onehot = np.zeros((k_pad, GRID_W, GRID_W), np.float32)
    for j in range(n_dc):
        onehot[j] = ((dc == j) & valid).astype(np.float32)
    onehot = onehot.reshape(k_pad, GRID_W * GRID_W)
    mask = np.where(valid, 0.0, NEG).astype(np.float32).reshape(1, GRID_W * GRID_W)
    rows = heads * n_dr
    rows_pad = -(-rows // 8) * 8
    r2 = jnp.zeros((rows_pad, k_pad), F32).at[:rows, :n_dc].set(rpb.reshape(rows, n_dc))
    t = pl.pallas_call(
        _toeplitz_kernel,
        out_shape=jax.ShapeDtypeStruct((rows_pad, GRID_W * GRID_W), F32),
        name="rpb_toeplitz",
    )(r2, jnp.asarray(onehot), jnp.asarray(mask))
    return t[:rows].reshape(heads, n_dr, GRID_W, GRID_W)


def _block_bias(t, n_rows):
    heads = t.shape[0]
    kh = min(WIN_H, n_rows)
    masked = jnp.full((heads, GRID_W, GRID_W), NEG, F32)
    classes = []
    for q0, k0 in ((0, 0), (Q_ROWS, Q_ROWS - WIN_H // 2), (n_rows - Q_ROWS, n_rows - K_ROWS)):
        strips = []
        for qr in range(Q_ROWS):
            r = q0 + qr
            r0 = min(max(r - kh // 2, 0), n_rows - kh)
            blocks = []
            for kr in range(K_ROWS):
                kk = k0 + kr
                blocks.append(t[:, kk - r + WIN_H - 1] if r0 <= kk < r0 + kh else masked)
            strips.append(jnp.concatenate(blocks, axis=-1))
        classes.append(jnp.concatenate(strips, axis=-2))
    return (jnp.stack(classes) * LOG2E).astype(BF16)


def _na_stages(q_ref, k_ref, v_ref, kc_ref, vc_ref, bias_ref, o_ref, k_scr, v_scr, p_scr, s_new, s_old,
               *, blk_new, blk_old):
    tq = q_ref.shape[1]
    tk = bias_ref.shape[-1]
    n_ctx = kc_ref.shape[1]
    l = k_ref.shape[1]
    first_head = lax.broadcasted_iota(jnp.int32, (1, LANES), 1) < HEAD_DIM

    def window_start(blk):
        return pl.multiple_of(jnp.clip(blk * tq - (tk - tq) // 2, 0, l - tk), tq // 2)

    k_scr[:tk] = k_ref[0, pl.ds(window_start(blk_new), tk), :]
    k_scr[tk:] = kc_ref[0]
    q = q_ref[0]
    zero = jnp.zeros_like(q)
    q2 = jnp.concatenate([jnp.where(first_head, q, zero), jnp.where(first_head, zero, q)], axis=0)
    s_new[...] = lax.dot_general(q2, k_scr[...], (((1,), (1,)), ((), ())), preferred_element_type=F32)

    v_scr[:tk, :LANES] = v_ref[0, pl.ds(window_start(blk_old), tk), :]
    v_scr[tk:, :LANES] = vc_ref[0]
    v_scr[:, LANES:] = jnp.ones((tk + n_ctx, LANES), BF16)
    for r0 in range(0, tq, NA_PV_ROWS):
        heads = []
        for h in range(2):
            for r in range(r0, r0 + NA_PV_ROWS, NA_CHUNK):
                rows = slice(h * tq + r, h * tq + r + NA_CHUNK)
                s_loc = s_old[rows, :tk] + bias_ref[0, h, r:r + NA_CHUNK, :].astype(F32)
                s_ctx = s_old[rows, tk:]
                m = jnp.maximum(jnp.max(s_loc, axis=-1, keepdims=True), jnp.max(s_ctx, axis=-1, keepdims=True))
                p_scr[rows, :tk] = jnp.exp2((s_loc - m).astype(BF16))
                p_scr[rows, tk:] = jnp.exp2((s_ctx - m).astype(BF16))
            o = jnp.dot(p_scr[h * tq + r0:h * tq + r0 + NA_PV_ROWS], v_scr[...], preferred_element_type=F32)
            heads.append(o[:, :LANES] / o[:, LANES:])
        o_ref[0, r0:r0 + NA_PV_ROWS] = jnp.where(first_head, heads[0], heads[1]).astype(BF16)


def _na_kernel(q_ref, k_ref, v_ref, kc_ref, vc_ref, bias_ref, o_ref, k_scr, v_scr, p_scr, s0, s1, *, nb):
    t = pl.program_id(0)
    last = pl.num_programs(0) - 2
    blocks = dict(blk_new=jnp.minimum(t, last) % nb, blk_old=jnp.clip(t - 1, 0, last) % nb)
    refs = (q_ref, k_ref, v_ref, kc_ref, vc_ref, bias_ref, o_ref, k_scr, v_scr, p_scr)

    @pl.when(t == 0)
    def _():
        s1[...] = jnp.zeros(s1.shape, F32)

    @pl.when(t % 2 == 0)
    def _():
        _na_stages(*refs, s0, s1, **blocks)

    @pl.when(t % 2 == 1)
    def _():
        _na_stages(*refs, s1, s0, **blocks)


def _neighbourhood_attention(qkv, qkv_ctx, bias):
    b, l, d3 = qkv.shape
    d = d3 // 3
    n_ctx = qkv_ctx.shape[1]
    pairs = d // LANES
    tq, tk = bias.shape[-2:]
    nb = l // tq

    steps = b * pairs * nb

    def decode(t):
        return t // (pairs * nb), (t // nb) % pairs, t % nb

    def lagged(t, lag):
        return decode(jnp.clip(t - lag, 0, steps - 1))

    def q_map(t):
        bi, hp, i = lagged(t, 0)
        return bi, i, hp

    def kv_map(which, lag):
        def index(t):
            bi, hp, _ = lagged(t, lag)
            return bi, 0, which * pairs + hp
        return index

    def bias_map(t):
        _, hp, i = lagged(t, 1)
        return jnp.where(i == 0, 0, jnp.where(i == nb - 1, 2, 1)), hp, 0, 0

    def out_map(t):
        bi, hp, i = lagged(t, 1)
        return bi, i, hp

    n_keys = tk + n_ctx
    return pl.pallas_call(
        functools.partial(_na_kernel, nb=nb),
        out_shape=jax.ShapeDtypeStruct((b, l, d), BF16),
        grid=(steps + 1,),
        in_specs=[
            pl.BlockSpec((1, tq, LANES), q_map),
            pl.BlockSpec((1, l, LANES), kv_map(1, 0)),
            pl.BlockSpec((1, l, LANES), kv_map(2, 1)),
            pl.BlockSpec((1, n_ctx, LANES), kv_map(1, 0)),
            pl.BlockSpec((1, n_ctx, LANES), kv_map(2, 1)),
            pl.BlockSpec((1, 2, tq, tk), bias_map),
        ],
        out_specs=pl.BlockSpec((1, tq, LANES), out_map),
        scratch_shapes=[
            pltpu.VMEM((n_keys, LANES), BF16),
            pltpu.VMEM((n_keys, 2 * LANES), BF16),
            pltpu.VMEM((2 * tq, n_keys), BF16),
            pltpu.VMEM((2 * tq, n_keys), F32), pltpu.VMEM((2 * tq, n_keys), F32),
        ],
        compiler_params=_params(("arbitrary",)),
        name="na_attention",
    )(qkv, qkv, qkv, qkv_ctx, qkv_ctx, bias)


def _halo_rows(prev_ref, main_ref, next_ref, s=0):
    return jnp.concatenate([prev_ref[s], main_ref[s], next_ref[s]], axis=0)


def _zero_outside(u, axis, halo):
    i = pl.program_id(axis)
    rows = u.shape[0]
    keep_top = jnp.where(i == 0, 0.0, 1.0).astype(F32)
    keep_bottom = jnp.where(i == pl.num_programs(axis) - 1, 0.0, 1.0).astype(F32)
    return jnp.concatenate([u[:halo] * keep_top, u[halo:rows - halo], u[rows - halo:] * keep_bottom], axis=0)


def _shift_rows(g, rows):
    return pltpu.roll(g, 1, 0), pltpu.roll(g, rows - 1, 0)


def _pack_pair(hi, lo):
    hi_bits = pltpu.bitcast(hi.astype(BF16).astype(F32), U32)
    lo_bits = pltpu.bitcast(lo.astype(BF16).astype(F32), U32)
    return hi_bits | (lo_bits >> 16)


def _unpack_pair(word):
    return (pltpu.bitcast(word & jnp.uint32(0xFFFF0000), F32), pltpu.bitcast(word << 16, F32))


def _pitch(n2):
    return n2 + 8


def _packed_mix_rows(prev_ref, main_ref, next_ref, n2):
    pitch = _pitch(n2)
    slabs = main_ref.shape[1] // pitch
    words = jnp.concatenate([prev_ref[0]] + [main_ref[0, j * pitch:j * pitch + n2] for j in range(slabs)]
                            + [next_ref[0]], axis=0)
    shift = ((pl.program_id(0) % 2) * 16).astype(U32)
    return pltpu.bitcast((words << shift) & jnp.uint32(0xFFFF0000), F32).astype(BF16)


def _post_kernel(mp_ref, mm_ref, mn_ref, xp_ref, xm_ref, xn_ref, wmix_ref, bmix_ref, g1_ref, n2_ref,
                 sh_ref, sc_ref, g2_ref, wup_ref, cw_ref, cb_ref, wdn_ref, o_ref, *, ff_chunk, packed_n2):
    halo = xp_ref.shape[1]
    rows = xm_ref.shape[1] + 2 * halo
    d_ff = wdn_ref.shape[0]
    if packed_n2:
        mix = _packed_mix_rows(mp_ref, mm_ref, mn_ref, packed_n2)
    else:
        mix = _halo_rows(mp_ref, mm_ref, mn_ref)
    x = _halo_rows(xp_ref, xm_ref, xn_ref)
    y = jnp.dot(mix, wmix_ref[...], preferred_element_type=F32) + bmix_ref[...]
    x1 = x + g1_ref[0] * y
    h = _rms_mod(x1, n2_ref[...], sh_ref[0], sc_ref[0]).astype(BF16)
    acc = jnp.zeros((rows - 2 * halo, x.shape[-1]), F32)
    for c in range(0, d_ff, ff_chunk):
        a = jnp.dot(h, wup_ref[:, c:c + ff_chunk], preferred_element_type=F32)
        g = _zero_outside(jnp.dot(h, wup_ref[:, d_ff + c:d_ff + c + ff_chunk], preferred_element_type=F32), 1, halo)
        g_prev, g_next = _shift_rows(g, rows)
        cw = cw_ref[:, c:c + ff_chunk]
        gc = g_prev * cw[0:1] + g * cw[1:2] + g_next * cw[2:3] + cb_ref[:, c:c + ff_chunk]
        gc = gc[halo:rows - halo]
        u = a[halo:rows - halo] * (0.5 * gc * (1.0 + lax.erf(gc * (2.0 ** -0.5))))
        acc = acc + jnp.dot(u.astype(BF16), wdn_ref[c:c + ff_chunk, :], preferred_element_type=F32)
    o_ref[0] = x1[halo:rows - halo] + g2_ref[0] * acc


def _halo_specs(tm, l, d, halo, nb=1):
    nblk = tm // halo
    last = l // halo - 1
    return [
        pl.BlockSpec((nb, halo, d), lambda bi, i: (bi, jnp.maximum(i * nblk - 1, 0), 0)),
        pl.BlockSpec((nb, tm, d), lambda bi, i: (bi, i, 0)),
        pl.BlockSpec((nb, halo, d), lambda bi, i: (bi, jnp.minimum((i + 1) * nblk, last), 0)),
    ]


def _packed_halo_specs(tm, l, d, n2):
    pitch = _pitch(n2)
    tp = tm // n2 * pitch
    last = l // n2 * pitch // 8 - 1
    return [
        pl.BlockSpec((1, 8, d), lambda bi, i: (bi // 2, jnp.maximum(i * (tp // 8) - 2, 0), 0)),
        pl.BlockSpec((1, tp, d), lambda bi, i: (bi // 2, i, 0)),
        pl.BlockSpec((1, 8, d), lambda bi, i: (bi // 2, jnp.minimum((i + 1) * (tp // 8), last), 0)),
    ]


def _post(mix, x, w_mix, b_mix, g1, norm2, sh2, sc2, g2, w_up, conv_w, conv_b, w_down, tm, ff_chunk, packed_n2=0):
    b, l, d = x.shape
    d_ff = w_down.shape[0]
    mod = pl.BlockSpec((1, 1, d), lambda bi, i: (bi, 0, 0))
    halo = 8 if packed_n2 else HALO
    mix_specs = _packed_halo_specs(tm, l, d, packed_n2) if packed_n2 else _halo_specs(tm, l, d, halo)
    return pl.pallas_call(
        functools.partial(_post_kernel, ff_chunk=ff_chunk, packed_n2=packed_n2),
        out_shape=jax.ShapeDtypeStruct((b, l, d), F32),
        grid=(b, l // tm),
        in_specs=mix_specs + _halo_specs(tm, l, d, halo) + [
            _const_spec((d, d)), _const_spec((1, d)), mod, _const_spec((1, d)), mod, mod, mod,
            _const_spec((d, 2 * d_ff)), _const_spec((3, d_ff)), _const_spec((1, d_ff)), _const_spec((d_ff, d)),
        ],
        out_specs=pl.BlockSpec((1, tm, d), lambda bi, i: (bi, i, 0)),
        compiler_params=_params(("parallel", "parallel")),
        name="post_ffn",
    )(mix, mix, mix, x, x, x, w_mix.astype(BF16), b_mix.reshape(1, d), g1, norm2.reshape(1, d), sh2, sc2, g2,
      w_up.astype(BF16), conv_w, conv_b.reshape(1, d_ff), w_down.astype(BF16))


def _hy_in_kernel(xp_ref, xm_ref, xn_ref, sh_ref, sc_ref, g_ref, w_ref, b_ref, cw_ref, cb_ref, x0_ref, z_ref, *, n2):
    halo = xp_ref.shape[1]
    tm = xm_ref.shape[1]
    rows = tm + 2 * halo
    d = xm_ref.shape[-1]
    pitch = _pitch(n2)
    x0s, zs = [], []
    for s in range(2):
        h = _rms_mod(_halo_rows(xp_ref, xm_ref, xn_ref, s), g_ref[...], sh_ref[s], sc_ref[s]).astype(BF16)
        parts = []
        for j in range(3):
            cols = slice(j * d, (j + 1) * d)
            u = _zero_outside(jnp.dot(h, w_ref[:, cols], preferred_element_type=F32) + b_ref[:, cols], 1, halo)
            u_prev, u_next = _shift_rows(u, rows)
            cw = cw_ref[:, cols]
            uc = u_prev * cw[0:1] + u * cw[1:2] + u_next * cw[2:3] + cb_ref[:, cols]
            parts.append(uc[halo:rows - halo])
        x0s.append(parts[0])
        zs.append(parts[2] * parts[1])
    pad = jnp.zeros((pitch - n2, d), U32)
    for ref, pair in ((x0_ref, x0s), (z_ref, zs)):
        words = _pack_pair(pair[0], pair[1])
        for j in range(tm // n2):
            ref[0, j * pitch:j * pitch + n2] = words[j * n2:(j + 1) * n2]
            ref[0, j * pitch + n2:(j + 1) * pitch] = pad


def _hy_in(x, shift, scale, gain, w_in, b_in, short_w, short_b, tm, n2):
    b, l, d = x.shape
    pitch = _pitch(n2)
    tp = tm // n2 * pitch
    mod = pl.BlockSpec((2, 1, d), lambda pi, i: (pi, 0, 0))
    out = pl.BlockSpec((1, tp, d), lambda pi, i: (pi, i, 0))
    shape = jax.ShapeDtypeStruct((b // 2, l // n2 * pitch, d), U32)
    return pl.pallas_call(
        functools.partial(_hy_in_kernel, n2=n2),
        out_shape=(shape, shape),
        grid=(b // 2, l // tm),
        in_specs=_halo_specs(tm, l, d, HALO, 2) + [
            mod, mod, _const_spec((1, d)), _const_spec((d, 3 * d)), _const_spec((1, 3 * d)),
            _const_spec((3, 3 * d)), _const_spec((1, 3 * d)),
        ],
        out_specs=(out, out),
        compiler_params=_params(("parallel", "parallel")),
        name="hyena_in",
    )(x, x, x, shift, scale, gain.reshape(1, d), w_in.astype(BF16), b_in.reshape(1, 3 * d),
      short_w, short_b.reshape(1, 3 * d))


def _filter_kernel(bands_ref, w1t_ref, w1c_ref, w1s_ref, b1_ref, w2_ref, b2_ref, w3_ref, b3_ref, wo_ref,
                   freq_ref, delta_ref, hf_ref, hb_ref, ss_ref, *, seq_len):
    tl = hf_ref.shape[0]
    d = hf_ref.shape[1]
    pos = (pl.program_id(0) * tl + lax.broadcasted_iota(jnp.int32, (tl, 1), 0)).astype(F32)
    t = pos * (1.0 / (seq_len - 1))
    ang = bands_ref[...] * ((2.0 * math.pi / seq_len) * pos)
    freq = freq_ref[...]
    dot = functools.partial(jnp.dot, preferred_element_type=F32, precision=HIGHEST)
    pre = t * w1t_ref[...] + dot(jnp.cos(ang), w1c_ref[...]) - dot(jnp.sin(ang), w1s_ref[...])
    hdn = jnp.sin(freq * (pre + b1_ref[...]))
    hdn = jnp.sin(freq * (dot(hdn, w2_ref[...]) + b2_ref[...]))
    hdn = jnp.sin(freq * (dot(hdn, w3_ref[...]) + b3_ref[...]))
    decay = jnp.exp(-t * delta_ref[...])
    hf = dot(hdn, wo_ref[:, :d]) * decay
    hb = dot(hdn, wo_ref[:, d:]) * decay
    hf_ref[...] = hf
    hb_ref[...] = jnp.where(pos > 0.0, hb, 0.0)

    @pl.when(pl.program_id(0) == 0)
    def _():
        ss_ref[...] = jnp.zeros_like(ss_ref)

    ss_ref[...] += jnp.sum(hf * hf + hb * hb, axis=0, keepdims=True)


def _hyena_filter(seq_len, d, w1, b1, w2, b2, w3, b3, w_out, freq, tl):
    width = w2.shape[0]
    bands = np.linspace(1e-4, HY_BANDS - 1, HY_BANDS, dtype=np.float32).reshape(1, HY_BANDS)
    deltas = np.abs(np.linspace(math.log(HY_DECAY_TARGET) / HY_SHORT_DECAY_PCT,
                                math.log(HY_DECAY_TARGET) / HY_LONG_DECAY_PCT, d, dtype=np.float32)).reshape(1, d)
    small = [
        (1, HY_BANDS), (1, width), (HY_BANDS, width), (HY_BANDS, width), (1, width), (width, width), (1, width),
        (width, width), (1, width), (width, 2 * d), (1, width), (1, d),
    ]
    return pl.pallas_call(
        functools.partial(_filter_kernel, seq_len=seq_len),
        out_shape=(jax.ShapeDtypeStruct((seq_len, d), F32), jax.ShapeDtypeStruct((seq_len, d), F32),
                   jax.ShapeDtypeStruct((1, d), F32)),
        grid=(seq_len // tl,),
        in_specs=[pl.BlockSpec(s, lambda i: (0, 0)) for s in small],
        out_specs=(pl.BlockSpec((tl, d), lambda i: (i, 0)), pl.BlockSpec((tl, d), lambda i: (i, 0)),
                   pl.BlockSpec((1, d), lambda i: (0, 0))),
        compiler_params=_params(("arbitrary",)),
        name="hyena_filter",
    )(jnp.asarray(bands), w1[0:1], w1[1:1 + HY_BANDS], w1[1 + HY_BANDS:], b1.reshape(1, width), w2,
      b2.reshape(1, width), w3, b3.reshape(1, width), w_out, freq.reshape(1, width), jnp.asarray(deltas))


def _dft_tables(n1):
    n = n1 * n1
    half = n1 // 2
    idx = np.arange(n1)
    ang = 2.0 * np.pi * np.outer(idx, idx) / n1
    c, s = np.cos(ang), np.sin(ang)
    fa_pair = np.block([[c[:, :half], s[:, :half]], [-s[:, :half], c[:, :half]]])
    zero = np.zeros((n1, half))
    fa_real = np.block([[c[:, :half], zero], [-s[:, :half], zero], [zero, c[:, :half]], [zero, -s[:, :half]]])
    fa_inv = np.block([[c[:half], -s[:half]], [s[:half], c[:half]]]) / n
    k1 = idx[:, None, None]
    k2 = idx[None, :, None]
    n2 = idx[None, None, :]
    m = (n2 * (k1 + n1 * k2)) % n
    gang = 2.0 * np.pi * m / n
    gr, gi = np.cos(gang), -np.sin(gang)
    g_fwd = np.concatenate([np.concatenate([gr, -gi], axis=2), np.concatenate([gi, gr], axis=2)], axis=1)
    g_inv = np.transpose(g_fwd, (0, 2, 1))
    to = lambda a: jnp.asarray(a.astype(np.float32), BF16)
    return to(fa_pair), to(fa_real), to(fa_inv), to(g_fwd), to(g_inv)


def _filter_spectrum_kernel(hf_ref, hb_ref, ss_ref, fa_ref, g_ref, o_ref, a_scr):
    n1 = fa_ref.shape[0] // 4
    n2 = g_ref.shape[1] // 2
    half = hf_ref.shape[0] // n2
    pitch = _pitch(n2)
    kb = g_ref.shape[0]

    @pl.when(pl.program_id(1) == 0)
    def _():
        def body(j, carry):
            rows = jnp.concatenate([hf_ref[pl.ds(j, half, stride=n2), :], hb_ref[pl.ds(j, half, stride=n2), :]], axis=0)
            r = jnp.dot(fa_ref[...], rows.astype(BF16), preferred_element_type=F32)
            a_scr[0, pl.ds(j, n1, stride=pitch), :] = _pack_pair(r[:n1], r[n1:2 * n1])
            a_scr[1, pl.ds(j, n1, stride=pitch), :] = _pack_pair(r[2 * n1:3 * n1], r[3 * n1:])
            return carry

        lax.fori_loop(0, n2, body, 0, unroll=DFT_UNROLL)

    scale = lax.rsqrt(ss_ref[...] + NORM_EPS)
    for k in range(kb):
        row0 = pl.multiple_of((pl.program_id(1) * kb + k) * pitch, 8)
        fr, fi = _unpack_pair(a_scr[0, pl.ds(row0, n2), :])
        br, bi = _unpack_pair(a_scr[1, pl.ds(row0, n2), :])
        a = jnp.concatenate([jnp.concatenate([fr, fi], axis=0), jnp.concatenate([br, bi], axis=0)], axis=1)
        p = jnp.dot(g_ref[k], a.astype(BF16), preferred_element_type=F32)
        lanes = p.shape[1] // 2
        o_ref[k, :n2] = (p[:n2, :lanes] + p[:n2, lanes:]) * scale
        o_ref[k, n2:] = (p[n2:, :lanes] - p[n2:, lanes:]) * scale


def _filter_spectrum(fa_filt, g_fwd, hf, hb, energy, kb):
    n1 = g_fwd.shape[0]
    n2 = g_fwd.shape[1] // 2
    l, d = hf.shape
    tc = LANES
    half_spec = pl.BlockSpec((l, tc), lambda c, k: (0, c))
    return pl.pallas_call(
        _filter_spectrum_kernel,
        out_shape=jax.ShapeDtypeStruct((n1, 2 * n2, d), F32),
        grid=(d // tc, n1 // kb),
        in_specs=[
            half_spec, half_spec, pl.BlockSpec((1, tc), lambda c, k: (0, c)),
            _const_spec(fa_filt.shape),
            pl.BlockSpec((kb, 2 * n2, 2 * n2), lambda c, k: (k, 0, 0)),
        ],
        out_specs=pl.BlockSpec((kb, 2 * n2, tc), lambda c, k: (k, 0, c)),
        scratch_shapes=[pltpu.VMEM((2, n1 * _pitch(n2), tc), U32)],
        compiler_params=_params(("parallel", "arbitrary")),
        name="filter_spectrum",
    )(hf, hb, energy, fa_filt, g_fwd)


def _long_conv_kernel(z_ref, x0_ref, hf_ref, gf_ref, gi_ref, fa_ref, fi_ref, db_ref, o_ref, a_scr):
    n1 = fa_ref.shape[0] // 2
    n2 = gf_ref.shape[1] // 2
    half = n1 // 2
    pitch = _pitch(n2)
    kb = gf_ref.shape[0]
    step = pl.program_id(2)

    @pl.when(step == 0)
    def _():
        def body(j, carry):
            zr, zi = _unpack_pair(z_ref[0, pl.ds(j, half, stride=pitch), :])
            rows = jnp.concatenate([zr, zi], axis=0).astype(BF16)
            r = jnp.dot(fa_ref[...], rows, preferred_element_type=F32)
            a_scr[pl.ds(j, n1, stride=pitch), :] = _pack_pair(r[:n1], r[n1:])
            return carry

        lax.fori_loop(0, n2, body, 0, unroll=DFT_UNROLL)

    for k in range(kb):
        row0 = pl.multiple_of((step * kb + k) * pitch, 8)
        ar, ai = _unpack_pair(a_scr[pl.ds(row0, n2), :])
        x = jnp.dot(gf_ref[k], jnp.concatenate([ar, ai], axis=0).astype(BF16), preferred_element_type=F32)
        xr, xi = x[:n2], x[n2:]
        hr, hi = hf_ref[k, :n2], hf_ref[k, n2:]
        y = jnp.concatenate([xr * hr - xi * hi, xr * hi + xi * hr], axis=0).astype(BF16)
        t = jnp.dot(gi_ref[k], y, preferred_element_type=F32)
        a_scr[pl.ds(row0, n2), :] = _pack_pair(t[:n2], t[n2:])

    @pl.when(step == pl.num_programs(2) - 1)
    def _():
        def body(j, carry):
            tr, ti = _unpack_pair(a_scr[pl.ds(j, n1, stride=pitch), :])
            y = jnp.dot(fi_ref[...], jnp.concatenate([tr, ti], axis=0).astype(BF16), preferred_element_type=F32)
            z0, z1 = _unpack_pair(z_ref[0, pl.ds(j, half, stride=pitch), :])
            g0, g1 = _unpack_pair(x0_ref[0, pl.ds(j, half, stride=pitch), :])
            db = db_ref[...]
            o_ref[0, pl.ds(j, half, stride=pitch), :] = _pack_pair(g0 * (y[:half] + z0 * db), g1 * (y[half:] + z1 * db))
            return carry

        lax.fori_loop(0, n2, body, 0, unroll=DFT_UNROLL)
        pad = jnp.zeros((pitch - n2, o_ref.shape[2]), U32)
        for s in range(half):
            o_ref[0, s * pitch + n2:(s + 1) * pitch] = pad


def _long_conv_gate(x0p, zp, hf, hb, energy, d_bias, l):
    p, lp, d = zp.shape
    n1 = math.isqrt(2 * l)
    assert n1 * n1 == 2 * l and lp == l // n1 * _pitch(n1)
    kb = min(16, n1)
    tc = LANES
    fa_pair, fa_real, fa_inv, g_fwd, g_inv = _dft_tables(n1)
    spec = _filter_spectrum(fa_real, g_fwd, hf, hb, energy, kb)
    seq = pl.BlockSpec((1, lp, tc), lambda c, pi, k: (pi, 0, c))
    g_blk = pl.BlockSpec((kb, 2 * n1, 2 * n1), lambda c, pi, k: (k, 0, 0))
    return pl.pallas_call(
        _long_conv_kernel,
        out_shape=jax.ShapeDtypeStruct((p, lp, d), U32),
        grid=(d // tc, p, n1 // kb),
        in_specs=[
            seq, seq, pl.BlockSpec((kb, 2 * n1, tc), lambda c, pi, k: (k, 0, c)), g_blk, g_blk,
            _const_spec(fa_pair.shape), _const_spec(fa_inv.shape), pl.BlockSpec((1, tc), lambda c, pi, k: (0, c)),
        ],
        out_specs=seq,
        scratch_shapes=[pltpu.VMEM((n1 * _pitch(n1), tc), U32)],
        compiler_params=_params(("parallel", "parallel", "arbitrary")),
        name="long_conv",
    )(zp, x0p, spec, g_fwd, g_inv, fa_pair, fa_inv, d_bias.reshape(1, d))


def _mod_rows(mod, lo, hi, d):
    m = mod[lo:hi]
    return [m[:, None, j * d:(j + 1) * d] for j in range(N_MOD)]


def kernel(x, c, ctx, c_ctx, l0_w_mod, l0_b_mod, l0_norm1, l0_norm2, l0_na_w_qkv, l0_na_q_gain, l0_na_k_gain, l0_na_rpb, l0_na_w_o, l0_ffn_w_up, l0_ffn_conv_w, l0_ffn_conv_b, l0_ffn_w_down, l1_w_mod, l1_b_mod, l1_norm1, l1_norm2, l1_hy_w_in, l1_hy_b_in, l1_hy_short_w, l1_hy_short_b, l1_hy_f_w1, l1_hy_f_b1, l1_hy_f_w2, l1_hy_f_b2, l1_hy_f_w3, l1_hy_f_b3, l1_hy_f_wout, l1_hy_f_freq, l1_hy_d_bias, l1_hy_w_out, l1_hy_b_out, l1_ffn_w_up, l1_ffn_conv_w, l1_ffn_conv_b, l1_ffn_w_down):
    b, l, d = x.shape
    n_ctx = ctx.shape[1]
    n_rows = l // GRID_W
    assert n_rows >= K_ROWS and n_rows % Q_ROWS == 0 and b % 2 == 0
    tm = min(512, l)
    ff_chunk = 256
    n_dft = math.isqrt(2 * l)

    cond = jnp.zeros((8, d), F32).at[:b].set(c).at[b].set(c_ctx)

    mod = _adaln(cond, l0_w_mod, l0_b_mod)
    sh1, sc1, g1, sh2, sc2, g2 = _mod_rows(mod, 0, b, d)
    csh1, csc1 = _mod_rows(mod, b, b + 1, d)[:2]
    qkv = _qkv(x, sh1, sc1, l0_norm1, l0_na_w_qkv, l0_na_q_gain, l0_na_k_gain, tm)
    qkv_ctx = _qkv(ctx, csh1, csc1, l0_norm1, l0_na_w_qkv, l0_na_q_gain, l0_na_k_gain, n_ctx)
    bias = _block_bias(_col_bias(l0_na_rpb), n_rows)
    attn = _neighbourhood_attention(qkv, qkv_ctx, bias)
    x = _post(attn, x, l0_na_w_o, jnp.zeros((d,), F32), g1, l0_norm2, sh2, sc2, g2,
              l0_ffn_w_up, l0_ffn_conv_w, l0_ffn_conv_b, l0_ffn_w_down, tm, ff_chunk)

    mod = _adaln(cond, l1_w_mod, l1_b_mod)
    sh1, sc1, g1, sh2, sc2, g2 = _mod_rows(mod, 0, b, d)
    x0p, zp = _hy_in(x, sh1, sc1, l1_norm1, l1_hy_w_in, l1_hy_b_in, l1_hy_short_w, l1_hy_short_b, tm, n_dft)
    hf, hb, energy = _hyena_filter(l, d, l1_hy_f_w1, l1_hy_f_b1, l1_hy_f_w2, l1_hy_f_b2, l1_hy_f_w3, l1_hy_f_b3,
                                   l1_hy_f_wout, l1_hy_f_freq, min(1024, l))
    gated = _long_conv_gate(x0p, zp, hf, hb, energy, l1_hy_d_bias, l)
    x = _post(gated, x, l1_hy_w_out, l1_hy_b_out, g1, l1_norm2, sh2, sc2, g2,
              l1_ffn_w_up, l1_ffn_conv_w, l1_ffn_conv_b, l1_ffn_w_down, tm, ff_chunk, packed_n2=n_dft)
    return x
```

```python
import functools
import math

import numpy as np
import jax
import jax.numpy as jnp
from jax import lax
from jax.experimental import pallas as pl
from jax.experimental.pallas import tpu as pltpu

F32 = jnp.float32
BF16 = jnp.bfloat16
U32 = jnp.uint32
HIGHEST = lax.Precision.HIGHEST

NORM_EPS = 1e-6
N_MOD = 6
HEAD_DIM = 64
GRID_W = 64
WIN_H = 8
WIN_W = 16
HY_BANDS = 16
HY_DECAY_TARGET = 1e-2
HY_SHORT_DECAY_PCT = 0.3
HY_LONG_DECAY_PCT = 1.5

LANES = 128
HALO = 16
Q_ROWS = 4
K_ROWS = Q_ROWS + WIN_H
NA_SUB = 4
NA_CHUNK = 32
DFT_UNROLL = 4
LOG2E = math.log2(math.e)
NEG = -1e30
VMEM_LIMIT = 56 * 1024 * 1024


def _params(sem):
    return pltpu.CompilerParams(dimension_semantics=sem, vmem_limit_bytes=VMEM_LIMIT)


def _const_spec(shape):
    return pl.BlockSpec(shape, lambda *_: (0,) * len(shape), pipeline_mode=pl.Buffered(1))


def _rms_mod(x, gain, shift, scale):
    ms = jnp.mean(x * x, axis=-1, keepdims=True)
    y = x * lax.rsqrt(ms + NORM_EPS) * gain
    return y * (1.0 + scale) + shift


def _adaln_kernel(c_ref, w_ref, b_ref, o_ref):
    c = c_ref[...]
    s = c / (1.0 + jnp.exp(-c))
    o_ref[...] = jnp.dot(s, w_ref[...], preferred_element_type=F32, precision=HIGHEST) + b_ref[...]


def _adaln(cond, w_mod, b_mod):
    rows, d = cond.shape
    n = w_mod.shape[1]
    tn = d
    return pl.pallas_call(
        _adaln_kernel,
        out_shape=jax.ShapeDtypeStruct((rows, n), F32),
        grid=(n // tn,),
        in_specs=[
            pl.BlockSpec((rows, d), lambda j: (0, 0)),
            pl.BlockSpec((d, tn), lambda j: (0, j)),
            pl.BlockSpec((1, tn), lambda j: (0, j)),
        ],
        out_specs=pl.BlockSpec((rows, tn), lambda j: (0, j)),
        compiler_params=_params(("arbitrary",)),
        name="adaln",
    )(cond, w_mod, b_mod.reshape(1, n))


def _qkv_kernel(x_ref, sh_ref, sc_ref, g_ref, w_ref, qg_ref, kg_ref, p1_ref, p2_ref, o_ref):
    d = x_ref.shape[-1]
    h = _rms_mod(x_ref[0], g_ref[...], sh_ref[0], sc_ref[0]).astype(BF16)
    for j, gain_ref in ((0, qg_ref), (1, kg_ref)):
        t = jnp.dot(h, w_ref[:, j * d:(j + 1) * d], preferred_element_type=F32)
        ms = jnp.dot((t * t).astype(BF16), p1_ref[...], preferred_element_type=F32)
        r = lax.rsqrt(ms + NORM_EPS)
        r_hi = r.astype(BF16)
        r_lo = (r - r_hi.astype(F32)).astype(BF16)
        rr = jnp.dot(jnp.concatenate([r_hi, r_lo], axis=-1), p2_ref[...], preferred_element_type=F32)
        o_ref[0, :, j * d:(j + 1) * d] = (t * rr * gain_ref[...]).astype(BF16)
    v = jnp.dot(h, w_ref[:, 2 * d:3 * d], preferred_element_type=F32)
    o_ref[0, :, 2 * d:3 * d] = v.astype(BF16)


def _qkv(x, shift, scale, gain, w_qkv, q_gain, k_gain, tm):
    b, l, d = x.shape
    heads = d // HEAD_DIM
    per_batch = shift.shape[0] == b
    mod_map = (lambda bi, i: (bi, 0, 0)) if per_batch else (lambda bi, i: (0, 0, 0))
    p1 = np.zeros((d, LANES), np.float32)
    p1[np.arange(d), np.arange(d) // HEAD_DIM] = 1.0 / HEAD_DIM
    p2 = np.zeros((2 * LANES, d), np.float32)
    p2[np.arange(d) // HEAD_DIM, np.arange(d)] = 1.0
    p2[LANES + np.arange(d) // HEAD_DIM, np.arange(d)] = 1.0
    qg = (jnp.tile(q_gain, heads) * (HEAD_DIM ** -0.5 * LOG2E)).reshape(1, d)
    kg = jnp.tile(k_gain, heads).reshape(1, d)
    return pl.pallas_call(
        _qkv_kernel,
        out_shape=jax.ShapeDtypeStruct((b, l, 3 * d), BF16),
        grid=(b, l // tm),
        in_specs=[
            pl.BlockSpec((1, tm, d), lambda bi, i: (bi, i, 0)),
            pl.BlockSpec((1, 1, d), mod_map),
            pl.BlockSpec((1, 1, d), mod_map),
            _const_spec((1, d)),
            _const_spec((d, 3 * d)),
            _const_spec((1, d)),
            _const_spec((1, d)),
            _const_spec((d, LANES)),
            _const_spec((2 * LANES, d)),
        ],
        out_specs=pl.BlockSpec((1, tm, 3 * d), lambda bi, i: (bi, i, 0)),
        compiler_params=_params(("parallel", "parallel")),
        name="qkv",
    )(x, shift, scale, gain.reshape(1, d), w_qkv.astype(BF16), qg, kg,
      jnp.asarray(p1, BF16), jnp.asarray(p2, BF16))


def _toeplitz_kernel(r_ref, oh_ref, m_ref, o_ref):
    o_ref[...] = jnp.dot(r_ref[...], oh_ref[...], preferred_element_type=F32, precision=HIGHEST) + m_ref[...]


def _col_bias(rpb):
    heads, n_dr, n_dc = rpb.shape
    k_pad = 32
    qc = np.arange(GRID_W)[:, None]
    kc = np.arange(GRID_W)[None, :]
    c0 = np.clip(qc - WIN_W // 2, 0, GRID_W - WIN_W)
    valid = (kc >= c0) & (kc < c0 + WIN_W)
    dc = kc - qc + (WIN_W - 1)
    onehot = np.zeros((k_pad, GRID_W, GRID_W), np.float32)
    for j in range(n_dc):
        onehot[j] = ((dc == j) & valid).astype(np.float32)
    onehot = onehot.reshape(k_pad, GRID_W * GRID_W)
    mask = np.where(valid, 0.0, NEG).astype(np.float32).reshape(1, GRID_W * GRID_W)
    rows = heads * n_dr
    rows_pad = -(-rows // 8) * 8
    r2 = jnp.zeros((rows_pad, k_pad), F32).at[:rows, :n_dc].set(rpb.reshape(rows, n_dc))
    t = pl.pallas_call(
        _toeplitz_kernel,
        out_shape=jax.ShapeDtypeStruct((rows_pad, GRID_W * GRID_W), F32),
        name="rpb_toeplitz",
    )(r2, jnp.asarray(onehot), jnp.asarray(mask))
    return t[:rows].reshape(heads, n_dr, GRID_W, GRID_W)


def _block_bias(t, n_rows):
    heads = t.shape[0]
    kh = min(WIN_H, n_rows)
    masked = jnp.full((heads, GRID_W, GRID_W), NEG, F32)
    classes = []
    for q0, k0 in ((0, 0), (Q_ROWS, Q_ROWS - WIN_H // 2), (n_rows - Q_ROWS, n_rows - K_ROWS)):
        strips = []
        for qr in range(Q_ROWS):
            r = q0 + qr
            r0 = min(max(r - kh // 2, 0), n_rows - kh)
            blocks = []
            for kr in range(K_ROWS):
                kk = k0 + kr
                blocks.append(t[:, kk - r + WIN_H - 1] if r0 <= kk < r0 + kh else masked)
            strips.append(jnp.concatenate(blocks, axis=-1))
        classes.append(jnp.concatenate(strips, axis=-2))
    return (jnp.stack(classes) * LOG2E).astype(BF16)


def _na_stages(q_ref, k_ref, v_ref, kc_ref, vc_ref, bias_ref, o_ref, k_scr, v_scr, p_scr, s_new, s_old,
               *, sub_new, sub_old, n_sub):
    tq = q_ref.shape[1] // NA_SUB
    tk = bias_ref.shape[-1]
    n_ctx = kc_ref.shape[1]
    l = k_ref.shape[1]
    first_head = lax.broadcasted_iota(jnp.int32, (1, LANES), 1) < HEAD_DIM

    def window_start(sub):
        return pl.multiple_of(jnp.clip(sub * tq - (tk - tq) // 2, 0, l - tk), tq)

    for j in range(NA_SUB):
        k_scr[j, :tk] = k_ref[0, pl.ds(window_start(sub_new + j), tk), :]
        k_scr[j, tk:] = kc_ref[0]
        q = q_ref[0, j * tq:(j + 1) * tq]
        zero = jnp.zeros_like(q)
        q2 = jnp.concatenate([jnp.where(first_head, q, zero), jnp.where(first_head, zero, q)], axis=0)
        s_new[j] = lax.dot_general(q2, k_scr[j], (((1,), (1,)), ((), ())), preferred_element_type=F32)

    for j in range(NA_SUB):
        sub = sub_old + j
        v_scr[j, :tk, :LANES] = v_ref[0, pl.ds(window_start(sub), tk), :]
        v_scr[j, tk:, :LANES] = vc_ref[0]
        v_scr[j, :, LANES:] = jnp.ones((tk + n_ctx, LANES), BF16)
        edge = jnp.where(sub == 0, 0, jnp.where(sub == n_sub - 1, 2, 1))
        heads = []
        for h in range(2):
            for r in range(0, tq, NA_CHUNK):
                rows = slice(h * tq + r, h * tq + r + NA_CHUNK)
                s_loc = s_old[j, rows, :tk] + bias_ref[edge, h, r:r + NA_CHUNK, :].astype(F32)
                s_ctx = s_old[j, rows, tk:]
                m = jnp.maximum(jnp.max(s_loc, axis=-1, keepdims=True), jnp.max(s_ctx, axis=-1, keepdims=True))
                p_scr[j, rows, :tk] = jnp.exp2((s_loc - m).astype(BF16))
                p_scr[j, rows, tk:] = jnp.exp2((s_ctx - m).astype(BF16))
            o = jnp.dot(p_scr[j, h * tq:(h + 1) * tq], v_scr[j], preferred_element_type=F32)
            heads.append(o[:, :LANES] / o[:, LANES:])
        o_ref[0, j * tq:(j + 1) * tq] = jnp.where(first_head, heads[0], heads[1]).astype(BF16)


def _na_kernel(q_ref, k_ref, v_ref, kc_ref, vc_ref, bias_ref, o_ref, k_scr, v_scr, p_scr, s0, s1, *, nb):
    t = pl.program_id(0)
    last = pl.num_programs(0) - 2
    subs = dict(sub_new=(jnp.minimum(t, last) % nb) * NA_SUB, sub_old=(jnp.clip(t - 1, 0, last) % nb) * NA_SUB,
                n_sub=nb * NA_SUB)
    refs = (q_ref, k_ref, v_ref, kc_ref, vc_ref, bias_ref, o_ref, k_scr, v_scr, p_scr)

    @pl.when(t == 0)
    def _():
        s1[...] = jnp.zeros(s1.shape, F32)

    @pl.when(t % 2 == 0)
    def _():
        _na_stages(*refs, s0, s1, **subs)

    @pl.when(t % 2 == 1)
    def _():
        _na_stages(*refs, s1, s0, **subs)


def _neighbourhood_attention(qkv, qkv_ctx, bias):
    b, l, d3 = qkv.shape
    d = d3 // 3
    n_ctx = qkv_ctx.shape[1]
    pairs = d // LANES
    tq, tk = bias.shape[-2:]
    tb = NA_SUB * tq
    nb = l // tb
    steps = b * pairs * nb

    def decode(t):
        return t // (pairs * nb), (t // nb) % pairs, t % nb

    def lagged(t, lag):
        return decode(jnp.clip(t - lag, 0, steps - 1))

    def q_map(t):
        bi, hp, i = lagged(t, 0)
        return bi, i, hp

    def kv_map(which, lag):
        def index(t):
            bi, hp, _ = lagged(t, lag)
            return bi, 0, which * pairs + hp
        return index

    def bias_map(t):
        return 0, lagged(t, 1)[1], 0, 0

    def out_map(t):
        bi, hp, i = lagged(t, 1)
        return bi, i, hp

    n_keys = tk + n_ctx
    return pl.pallas_call(
        functools.partial(_na_kernel, nb=nb),
        out_shape=jax.ShapeDtypeStruct((b, l, d), BF16),
        grid=(steps + 1,),
        in_specs=[
            pl.BlockSpec((1, tb, LANES), q_map),
            pl.BlockSpec((1, l, LANES), kv_map(1, 0)),
            pl.BlockSpec((1, l, LANES), kv_map(2, 1)),
            pl.BlockSpec((1, n_ctx, LANES), kv_map(1, 0)),
            pl.BlockSpec((1, n_ctx, LANES), kv_map(2, 1)),
            pl.BlockSpec((3, 2, tq, tk), bias_map),
        ],
        out_specs=pl.BlockSpec((1, tb, LANES), out_map),
        scratch_shapes=[
            pltpu.VMEM((NA_SUB, n_keys, LANES), BF16),
            pltpu.VMEM((NA_SUB, n_keys, 2 * LANES), BF16),
            pltpu.VMEM((NA_SUB, 2 * tq, n_keys), BF16),
            pltpu.VMEM((NA_SUB, 2 * tq, n_keys), F32), pltpu.VMEM((NA_SUB, 2 * tq, n_keys), F32),
        ],
        compiler_params=_params(("arbitrary",)),
        name="na_attention",
    )(qkv, qkv, qkv, qkv_ctx, qkv_ctx, bias)


def _halo_rows(prev_ref, main_ref, next_ref, s=0):
    return jnp.concatenate([prev_ref[s], main_ref[s], next_ref[s]], axis=0)


def _zero_outside(u, axis, halo):
    i = pl.program_id(axis)
    rows = u.shape[0]
    keep_top = jnp.where(i == 0, 0.0, 1.0).astype(F32)
    keep_bottom = jnp.where(i == pl.num_programs(axis) - 1, 0.0, 1.0).astype(F32)
    return jnp.concatenate([u[:halo] * keep_top, u[halo:rows - halo], u[rows - halo:] * keep_bottom], axis=0)


def _shift_rows(g, rows):
    return pltpu.roll(g, 1, 0), pltpu.roll(g, rows - 1, 0)


def _pack_pair(hi, lo):
    hi_bits = pltpu.bitcast(hi.astype(BF16).astype(F32), U32)
    lo_bits = pltpu.bitcast(lo.astype(BF16).astype(F32), U32)
    return hi_bits | (lo_bits >> 16)


def _unpack_pair(word):
    return (pltpu.bitcast(word & jnp.uint32(0xFFFF0000), F32), pltpu.bitcast(word << 16, F32))


def _pitch(n2):
    return n2 + 8


def _packed_mix_rows(prev_ref, main_ref, next_ref, n2):
    pitch = _pitch(n2)
    slabs = main_ref.shape[1] // pitch
    words = jnp.concatenate([prev_ref[0]] + [main_ref[0, j * pitch:j * pitch + n2] for j in range(slabs)]
                            + [next_ref[0]], axis=0)
    shift = ((pl.program_id(0) % 2) * 16).astype(U32)
    return pltpu.bitcast((words << shift) & jnp.uint32(0xFFFF0000), F32).astype(BF16)


def _post_kernel(mp_ref, mm_ref, mn_ref, xp_ref, xm_ref, xn_ref, wmix_ref, bmix_ref, g1_ref, n2_ref,
                 sh_ref, sc_ref, g2_ref, wup_ref, cw_ref, cb_ref, wdn_ref, o_ref, *, ff_chunk, packed_n2):
    halo = xp_ref.shape[1]
    rows = xm_ref.shape[1] + 2 * halo
    d_ff = wdn_ref.shape[0]
    if packed_n2:
        mix = _packed_mix_rows(mp_ref, mm_ref, mn_ref, packed_n2)
    else:
        mix = _halo_rows(mp_ref, mm_ref, mn_ref)
    x = _halo_rows(xp_ref, xm_ref, xn_ref)
    y = jnp.dot(mix, wmix_ref[...], preferred_element_type=F32) + bmix_ref[...]
    x1 = x + g1_ref[0] * y
    h = _rms_mod(x1, n2_ref[...], sh_ref[0], sc_ref[0]).astype(BF16)
    acc = jnp.zeros((rows - 2 * halo, x.shape[-1]), F32)
    for c in range(0, d_ff, ff_chunk):
        a = jnp.dot(h, wup_ref[:, c:c + ff_chunk], preferred_element_type=F32)
        g = _zero_outside(jnp.dot(h, wup_ref[:, d_ff + c:d_ff + c + ff_chunk], preferred_element_type=F32), 1, halo)
        g_prev, g_next = _shift_rows(g, rows)
        cw = cw_ref[:, c:c + ff_chunk]
        gc = g_prev * cw[0:1] + g * cw[1:2] + g_next * cw[2:3] + cb_ref[:, c:c + ff_chunk]
        gc = gc[halo:rows - halo]
        u = a[halo:rows - halo] * (0.5 * gc * (1.0 + lax.erf(gc * (2.0 ** -0.5))))
        acc = acc + jnp.dot(u.astype(BF16), wdn_ref[c:c + ff_chunk, :], preferred_element_type=F32)
    o_ref[0] = x1[halo:rows - halo] + g2_ref[0] * acc


def _halo_specs(tm, l, d, halo, nb=1):
    nblk = tm // halo
    last = l // halo - 1
    return [
        pl.BlockSpec((nb, halo, d), lambda bi, i: (bi, jnp.maximum(i * nblk - 1, 0), 0)),
        pl.BlockSpec((nb, tm, d), lambda bi, i: (bi, i, 0)),
        pl.BlockSpec((nb, halo, d), lambda bi, i: (bi, jnp.minimum((i + 1) * nblk, last), 0)),
    ]


def _packed_halo_specs(tm, l, d, n2):
    pitch = _pitch(n2)
    tp = tm // n2 * pitch
    last = l // n2 * pitch // 8 - 1
    return [
        pl.BlockSpec((1, 8, d), lambda bi, i: (bi // 2, jnp.maximum(i * (tp // 8) - 2, 0), 0)),
        pl.BlockSpec((1, tp, d), lambda bi, i: (bi // 2, i, 0)),
        pl.BlockSpec((1, 8, d), lambda bi, i: (bi // 2, jnp.minimum((i + 1) * (tp // 8), last), 0)),
    ]


def _post(mix, x, w_mix, b_mix, g1, norm2, sh2, sc2, g2, w_up, conv_w, conv_b, w_down, tm, ff_chunk, packed_n2=0):
    b, l, d = x.shape
    d_ff = w_down.shape[0]
    mod = pl.BlockSpec((1, 1, d), lambda bi, i: (bi, 0, 0))
    halo = 8 if packed_n2 else HALO
    mix_specs = _packed_halo_specs(tm, l, d, packed_n2) if packed_n2 else _halo_specs(tm, l, d, halo)
    return pl.pallas_call(
        functools.partial(_post_kernel, ff_chunk=ff_chunk, packed_n2=packed_n2),
        out_shape=jax.ShapeDtypeStruct((b, l, d), F32),
        grid=(b, l // tm),
        in_specs=mix_specs + _halo_specs(tm, l, d, halo) + [
            _const_spec((d, d)), _const_spec((1, d)), mod, _const_spec((1, d)), mod, mod, mod,
            _const_spec((d, 2 * d_ff)), _const_spec((3, d_ff)), _const_spec((1, d_ff)), _const_spec((d_ff, d)),
        ],
        out_specs=pl.BlockSpec((1, tm, d), lambda bi, i: (bi, i, 0)),
        compiler_params=_params(("parallel", "parallel")),
        name="post_ffn",
    )(mix, mix, mix, x, x, x, w_mix.astype(BF16), b_mix.reshape(1, d), g1, norm2.reshape(1, d), sh2, sc2, g2,
      w_up.astype(BF16), conv_w, conv_b.reshape(1, d_ff), w_down.astype(BF16))


def _hy_in_kernel(xp_ref, xm_ref, xn_ref, sh_ref, sc_ref, g_ref, w_ref, b_ref, cw_ref, cb_ref, x0_ref, z_ref, *, n2):
    halo = xp_ref.shape[1]
    tm = xm_ref.shape[1]
    rows = tm + 2 * halo
    d = xm_ref.shape[-1]
    pitch = _pitch(n2)
    x0s, zs = [], []
    for s in range(2):
        h = _rms_mod(_halo_rows(xp_ref, xm_ref, xn_ref, s), g_ref[...], sh_ref[s], sc_ref[s]).astype(BF16)
        parts = []
        for j in range(3):
            cols = slice(j * d, (j + 1) * d)
            u = _zero_outside(jnp.dot(h, w_ref[:, cols], preferred_element_type=F32) + b_ref[:, cols], 1, halo)
            u_prev, u_next = _shift_rows(u, rows)
            cw = cw_ref[:, cols]
            uc = u_prev * cw[0:1] + u * cw[1:2] + u_next * cw[2:3] + cb_ref[:, cols]
            parts.append(uc[halo:rows - halo])
        x0s.append(parts[0])
        zs.append(parts[2] * parts[1])
    pad = jnp.zeros((pitch - n2, d), U32)
    for ref, pair in ((x0_ref, x0s), (z_ref, zs)):
        words = _pack_pair(pair[0], pair[1])
        for j in range(tm // n2):
            ref[0, j * pitch:j * pitch + n2] = words[j * n2:(j + 1) * n2]
            ref[0, j * pitch + n2:(j + 1) * pitch] = pad


def _hy_in(x, shift, scale, gain, w_in, b_in, short_w, short_b, tm, n2):
    b, l, d = x.shape
    pitch = _pitch(n2)
    tp = tm // n2 * pitch
    mod = pl.BlockSpec((2, 1, d), lambda pi, i: (pi, 0, 0))
    out = pl.BlockSpec((1, tp, d), lambda pi, i: (pi, i, 0))
    shape = jax.ShapeDtypeStruct((b // 2, l // n2 * pitch, d), U32)
    return pl.pallas_call(
        functools.partial(_hy_in_kernel, n2=n2),
        out_shape=(shape, shape),
        grid=(b // 2, l // tm),
        in_specs=_halo_specs(tm, l, d, HALO, 2) + [
            mod, mod, _const_spec((1, d)), _const_spec((d, 3 * d)), _const_spec((1, 3 * d)),
            _const_spec((3, 3 * d)), _const_spec((1, 3 * d)),
        ],
        out_specs=(out, out),
        compiler_params=_params(("parallel", "parallel")),
        name="hyena_in",
    )(x, x, x, shift, scale, gain.reshape(1, d), w_in.astype(BF16), b_in.reshape(1, 3 * d),
      short_w, short_b.reshape(1, 3 * d))


def _filter_kernel(bands_ref, w1t_ref, w1c_ref, w1s_ref, b1_ref, w2_ref, b2_ref, w3_ref, b3_ref, wo_ref,
                   freq_ref, delta_ref, hf_ref, hb_ref, ss_ref, *, seq_len):
    tl = hf_ref.shape[0]
    d = hf_ref.shape[1]
    pos = (pl.program_id(0) * tl + lax.broadcasted_iota(jnp.int32, (tl, 1), 0)).astype(F32)
    t = pos * (1.0 / (seq_len - 1))
    ang = bands_ref[...] * ((2.0 * math.pi / seq_len) * pos)
    freq = freq_ref[...]
    dot = functools.partial(jnp.dot, preferred_element_type=F32, precision=HIGHEST)
    pre = t * w1t_ref[...] + dot(jnp.cos(ang), w1c_ref[...]) - dot(jnp.sin(ang), w1s_ref[...])
    hdn = jnp.sin(freq * (pre + b1_ref[...]))
    hdn = jnp.sin(freq * (dot(hdn, w2_ref[...]) + b2_ref[...]))
    hdn = jnp.sin(freq * (dot(hdn, w3_ref[...]) + b3_ref[...]))
    decay = jnp.exp(-t * delta_ref[...])
    hf = dot(hdn, wo_ref[:, :d]) * decay
    hb = dot(hdn, wo_ref[:, d:]) * decay
    hf_ref[...] = hf
    hb_ref[...] = jnp.where(pos > 0.0, hb, 0.0)

    @pl.when(pl.program_id(0) == 0)
    def _():
        ss_ref[...] = jnp.zeros_like(ss_ref)

    ss_ref[...] += jnp.sum(hf * hf + hb * hb, axis=0, keepdims=True)


def _hyena_filter(seq_len, d, w1, b1, w2, b2, w3, b3, w_out, freq, tl):
    width = w2.shape[0]
    bands = np.linspace(1e-4, HY_BANDS - 1, HY_BANDS, dtype=np.float32).reshape(1, HY_BANDS)
    deltas = np.abs(np.linspace(math.log(HY_DECAY_TARGET) / HY_SHORT_DECAY_PCT,
                                math.log(HY_DECAY_TARGET) / HY_LONG_DECAY_PCT, d, dtype=np.float32)).reshape(1, d)
    small = [
        (1, HY_BANDS), (1, width), (HY_BANDS, width), (HY_BANDS, width), (1, width), (width, width), (1, width),
        (width, width), (1, width), (width, 2 * d), (1, width), (1, d),
    ]
    return pl.pallas_call(
        functools.partial(_filter_kernel, seq_len=seq_len),
        out_shape=(jax.ShapeDtypeStruct((seq_len, d), F32), jax.ShapeDtypeStruct((seq_len, d), F32),
                   jax.ShapeDtypeStruct((1, d), F32)),
        grid=(seq_len // tl,),
        in_specs=[pl.BlockSpec(s, lambda i: (0, 0)) for s in small],
        out_specs=(pl.BlockSpec((tl, d), lambda i: (i, 0)), pl.BlockSpec((tl, d), lambda i: (i, 0)),
                   pl.BlockSpec((1, d), lambda i: (0, 0))),
        compiler_params=_params(("arbitrary",)),
        name="hyena_filter",
    )(jnp.asarray(bands), w1[0:1], w1[1:1 + HY_BANDS], w1[1 + HY_BANDS:], b1.reshape(1, width), w2,
      b2.reshape(1, width), w3, b3.reshape(1, width), w_out, freq.reshape(1, width), jnp.asarray(deltas))


def _dft_tables(n1):
    n = n1 * n1
    half = n1 // 2
    idx = np.arange(n1)
    ang = 2.0 * np.pi * np.outer(idx, idx) / n1
    c, s = np.cos(ang), np.sin(ang)
    fa_pair = np.block([[c[:, :half], s[:, :half]], [-s[:, :half], c[:, :half]]])
    zero = np.zeros((n1, half))
    fa_real = np.block([[c[:, :half], zero], [-s[:, :half], zero], [zero, c[:, :half]], [zero, -s[:, :half]]])
    fa_inv = np.block([[c[:half], -s[:half]], [s[:half], c[:half]]]) / n
    k1 = idx[:, None, None]
    k2 = idx[None, :, None]
    n2 = idx[None, None, :]
    m = (n2 * (k1 + n1 * k2)) % n
    gang = 2.0 * np.pi * m / n
    gr, gi = np.cos(gang), -np.sin(gang)
    g_fwd = np.concatenate([np.concatenate([gr, -gi], axis=2), np.concatenate([gi, gr], axis=2)], axis=1)
    g_inv = np.transpose(g_fwd, (0, 2, 1))
    to = lambda a: jnp.asarray(a.astype(np.float32), BF16)
    return to(fa_pair), to(fa_real), to(fa_inv), to(g_fwd), to(g_inv)


def _filter_spectrum_kernel(hf_ref, hb_ref, ss_ref, fa_ref, g_ref, o_ref, a_scr):
    n1 = fa_ref.shape[0] // 4
    n2 = g_ref.shape[1] // 2
    half = hf_ref.shape[0] // n2
    pitch = _pitch(n2)
    kb = g_ref.shape[0]

    @pl.when(pl.program_id(1) == 0)
    def _():
        def body(j, carry):
            rows = jnp.concatenate([hf_ref[pl.ds(j, half, stride=n2), :], hb_ref[pl.ds(j, half, stride=n2), :]], axis=0)
            r = jnp.dot(fa_ref[...], rows.astype(BF16), preferred_element_type=F32)
            a_scr[0, pl.ds(j, n1, stride=pitch), :] = _pack_pair(r[:n1], r[n1:2 * n1])
            a_scr[1, pl.ds(j, n1, stride=pitch), :] = _pack_pair(r[2 * n1:3 * n1], r[3 * n1:])
            return carry

        lax.fori_loop(0, n2, body, 0, unroll=DFT_UNROLL)

    scale = lax.rsqrt(ss_ref[...] + NORM_EPS)
    for k in range(kb):
        row0 = pl.multiple_of((pl.program_id(1) * kb + k) * pitch, 8)
        fr, fi = _unpack_pair(a_scr[0, pl.ds(row0, n2), :])
        br, bi = _unpack_pair(a_scr[1, pl.ds(row0, n2), :])
        a = jnp.concatenate([jnp.concatenate([fr, fi], axis=0), jnp.concatenate([br, bi], axis=0)], axis=1)
        p = jnp.dot(g_ref[k], a.astype(BF16), preferred_element_type=F32)
        lanes = p.shape[1] // 2
        o_ref[k, :n2] = (p[:n2, :lanes] + p[:n2, lanes:]) * scale
        o_ref[k, n2:] = (p[n2:, :lanes] - p[n2:, lanes:]) * scale


def _filter_spectrum(fa_filt, g_fwd, hf, hb, energy, kb):
    n1 = g_fwd.shape[0]
    n2 = g_fwd.shape[1] // 2
    l, d = hf.shape
    tc = LANES
    half_spec = pl.BlockSpec((l, tc), lambda c, k: (0, c))
    return pl.pallas_call(
        _filter_spectrum_kernel,
        out_shape=jax.ShapeDtypeStruct((n1, 2 * n2, d), F32),
        grid=(d // tc, n1 // kb),
        in_specs=[
            half_spec, half_spec, pl.BlockSpec((1, tc), lambda c, k: (0, c)),
            _const_spec(fa_filt.shape),
            pl.BlockSpec((kb, 2 * n2, 2 * n2), lambda c, k: (k, 0, 0)),
        ],
        out_specs=pl.BlockSpec((kb, 2 * n2, tc), lambda c, k: (k, 0, c)),
        scratch_shapes=[pltpu.VMEM((2, n1 * _pitch(n2), tc), U32)],
        compiler_params=_params(("parallel", "arbitrary")),
        name="filter_spectrum",
    )(hf, hb, energy, fa_filt, g_fwd)


def _long_conv_kernel(z_ref, x0_ref, hf_ref, gf_ref, gi_ref, fa_ref, fi_ref, db_ref, o_ref, a_scr):
    n1 = fa_ref.shape[0] // 2
    n2 = gf_ref.shape[1] // 2
    half = n1 // 2
    pitch = _pitch(n2)
    kb = gf_ref.shape[0]
    step = pl.program_id(2)

    @pl.when(step == 0)
    def _():
        def body(j, carry):
            zr, zi = _unpack_pair(z_ref[0, pl.ds(j, half, stride=pitch), :])
            rows = jnp.concatenate([zr, zi], axis=0).astype(BF16)
            r = jnp.dot(fa_ref[...], rows, preferred_element_type=F32)
            a_scr[pl.ds(j, n1, stride=pitch), :] = _pack_pair(r[:n1], r[n1:])
            return carry

        lax.fori_loop(0, n2, body, 0, unroll=DFT_UNROLL)

    for k in range(kb):
        row0 = pl.multiple_of((step * kb + k) * pitch, 8)
        ar, ai = _unpack_pair(a_scr[pl.ds(row0, n2), :])
        x = jnp.dot(gf_ref[k], jnp.concatenate([ar, ai], axis=0).astype(BF16), preferred_element_type=F32)
        xr, xi = x[:n2], x[n2:]
        hr, hi = hf_ref[k, :n2], hf_ref[k, n2:]
        y = jnp.concatenate([xr * hr - xi * hi, xr * hi + xi * hr], axis=0).astype(BF16)
        t = jnp.dot(gi_ref[k], y, preferred_element_type=F32)
        a_scr[pl.ds(row0, n2), :] = _pack_pair(t[:n2], t[n2:])

    @pl.when(step == pl.num_programs(2) - 1)
    def _():
        def body(j, carry):
            tr, ti = _unpack_pair(a_scr[pl.ds(j, n1, stride=pitch), :])
            y = jnp.dot(fi_ref[...], jnp.concatenate([tr, ti], axis=0).astype(BF16), preferred_element_type=F32)
            z0, z1 = _unpack_pair(z_ref[0, pl.ds(j, half, stride=pitch), :])
            g0, g1 = _unpack_pair(x0_ref[0, pl.ds(j, half, stride=pitch), :])
            db = db_ref[...]
            o_ref[0, pl.ds(j, half, stride=pitch), :] = _pack_pair(g0 * (y[:half] + z0 * db), g1 * (y[half:] + z1 * db))
            return carry

        lax.fori_loop(0, n2, body, 0, unroll=DFT_UNROLL)
        pad = jnp.zeros((pitch - n2, o_ref.shape[2]), U32)
        for s in range(half):
            o_ref[0, s * pitch + n2:(s + 1) * pitch] = pad


def _long_conv_gate(x0p, zp, hf, hb, energy, d_bias, l):
    p, lp, d = zp.shape
    n1 = math.isqrt(2 * l)
    assert n1 * n1 == 2 * l and lp == l // n1 * _pitch(n1)
    kb = min(16, n1)
    tc = LANES
    fa_pair, fa_real, fa_inv, g_fwd, g_inv = _dft_tables(n1)
    spec = _filter_spectrum(fa_real, g_fwd, hf, hb, energy, kb)
    seq = pl.BlockSpec((1, lp, tc), lambda c, pi, k: (pi, 0, c))
    g_blk = pl.BlockSpec((kb, 2 * n1, 2 * n1), lambda c, pi, k: (k, 0, 0))
    return pl.pallas_call(
        _long_conv_kernel,
        out_shape=jax.ShapeDtypeStruct((p, lp, d), U32),
        grid=(d // tc, p, n1 // kb),
        in_specs=[
            seq, seq, pl.BlockSpec((kb, 2 * n1, tc), lambda c, pi, k: (k, 0, c)), g_blk, g_blk,
            _const_spec(fa_pair.shape), _const_spec(fa_inv.shape), pl.BlockSpec((1, tc), lambda c, pi, k: (0, c)),
        ],
        out_specs=seq,
        scratch_shapes=[pltpu.VMEM((n1 * _pitch(n1), tc), U32)],
        compiler_params=_params(("parallel", "parallel", "arbitrary")),
        name="long_conv",
    )(zp, x0p, spec, g_fwd, g_inv, fa_pair, fa_inv, d_bias.reshape(1, d))


def _mod_rows(mod, lo, hi, d):
    m = mod[lo:hi]
    return [m[:, None, j * d:(j + 1) * d] for j in range(N_MOD)]


def kernel(x, c, ctx, c_ctx, l0_w_mod, l0_b_mod, l0_norm1, l0_norm2, l0_na_w_qkv, l0_na_q_gain, l0_na_k_gain, l0_na_rpb, l0_na_w_o, l0_ffn_w_up, l0_ffn_conv_w, l0_ffn_conv_b, l0_ffn_w_down, l1_w_mod, l1_b_mod, l1_norm1, l1_norm2, l1_hy_w_in, l1_hy_b_in, l1_hy_short_w, l1_hy_short_b, l1_hy_f_w1, l1_hy_f_b1, l1_hy_f_w2, l1_hy_f_b2, l1_hy_f_w3, l1_hy_f_b3, l1_hy_f_wout, l1_hy_f_freq, l1_hy_d_bias, l1_hy_w_out, l1_hy_b_out, l1_ffn_w_up, l1_ffn_conv_w, l1_ffn_conv_b, l1_ffn_w_down):
    b, l, d = x.shape
    n_ctx = ctx.shape[1]
    n_rows = l // GRID_W
    assert n_rows >= K_ROWS and n_rows % (Q_ROWS * NA_SUB) == 0 and b % 2 == 0
    tm = min(512, l)
    ff_chunk = 256
    n_dft = math.isqrt(2 * l)

    cond = jnp.zeros((8, d), F32).at[:b].set(c).at[b].set(c_ctx)

    mod = _adaln(cond, l0_w_mod, l0_b_mod)
    sh1, sc1, g1, sh2, sc2, g2 = _mod_rows(mod, 0, b, d)
    csh1, csc1 = _mod_rows(mod, b, b + 1, d)[:2]
    qkv = _qkv(x, sh1, sc1, l0_norm1, l0_na_w_qkv, l0_na_q_gain, l0_na_k_gain, tm)
    qkv_ctx = _qkv(ctx, csh1, csc1, l0_norm1, l0_na_w_qkv, l0_na_q_gain, l0_na_k_gain, n_ctx)
    bias = _block_bias(_col_bias(l0_na_rpb), n_rows)
    attn = _neighbourhood_attention(qkv, qkv_ctx, bias)
    x = _post(attn, x, l0_na_w_o, jnp.zeros((d,), F32), g1, l0_norm2, sh2, sc2, g2,
              l0_ffn_w_up, l0_ffn_conv_w, l0_ffn_conv_b, l0_ffn_w_down, tm, ff_chunk)

    mod = _adaln(cond, l1_w_mod, l1_b_mod)
    sh1, sc1, g1, sh2, sc2, g2 = _mod_rows(mod, 0, b, d)
    x0p, zp = _hy_in(x, sh1, sc1, l1_norm1, l1_hy_w_in, l1_hy_b_in, l1_hy_short_w, l1_hy_short_b, tm, n_dft)
    hf, hb, energy = _hyena_filter(l, d, l1_hy_f_w1, l1_hy_f_b1, l1_hy_f_w2, l1_hy_f_b2, l1_hy_f_w3, l1_hy_f_b3,
                                   l1_hy_f_wout, l1_hy_f_freq, min(1024, l))
    gated = _long_conv_gate(x0p, zp, hf, hb, energy, l1_hy_d_bias, l)
    x = _post(gated, x, l1_hy_w_out, l1_hy_b_out, g1, l1_norm2, sh2, sc2, g2,
              l1_ffn_w_up, l1_ffn_conv_w, l1_ffn_conv_b, l1_ffn_w_down, tm, ff_chunk, packed_n2=n_dft)
    return x
```

```python
import functools
import math

import numpy as np
import jax
import jax.numpy as jnp
from jax import lax
from jax.experimental import pallas as pl
from jax.experimental.pallas import tpu as pltpu

F32 = jnp.float32
BF16 = jnp.bfloat16
U32 = jnp.uint32
HIGHEST = lax.Precision.HIGHEST

NORM_EPS = 1e-6
N_MOD = 6
HEAD_DIM = 64
GRID_W = 64
WIN_H = 8
WIN_W = 16
HY_BANDS = 16
HY_DECAY_TARGET = 1e-2
HY_SHORT_DECAY_PCT = 0.3
HY_LONG_DECAY_PCT = 1.5

LANES = 128
HALO = 16
Q_ROWS = 4
K_ROWS = Q_ROWS + WIN_H
NA_SUB = 4
NA_CHUNK = 32
DFT_UNROLL = 16
DFT_UNROLL_OUT = 8
LOG2E = math.log2(math.e)
NEG = -1e30
VMEM_LIMIT = 56 * 1024 * 1024


def _params(sem):
    return pltpu.CompilerParams(dimension_semantics=sem, vmem_limit_bytes=VMEM_LIMIT)


def _const_spec(shape):
    return pl.BlockSpec(shape, lambda *_: (0,) * len(shape), pipeline_mode=pl.Buffered(1))


def _rms_mod(x, gain, shift, scale):
    ms = jnp.mean(x * x, axis=-1, keepdims=True)
    y = x * lax.rsqrt(ms + NORM_EPS) * gain
    return y * (1.0 + scale) + shift


def _adaln_kernel(c_ref, w_ref, b_ref, o_ref):
    c = c_ref[...]
    s = c / (1.0 + jnp.exp(-c))
    o_ref[...] = jnp.dot(s, w_ref[...], preferred_element_type=F32, precision=HIGHEST) + b_ref[...]


def _adaln(cond, w_mod, b_mod):
    rows, d = cond.shape
    n = w_mod.shape[1]
    tn = d
    return pl.pallas_call(
        _adaln_kernel,
        out_shape=jax.ShapeDtypeStruct((rows, n), F32),
        grid=(n // tn,),
        in_specs=[
            pl.BlockSpec((rows, d), lambda j: (0, 0)),
            pl.BlockSpec((d, tn), lambda j: (0, j)),
            pl.BlockSpec((1, tn), lambda j: (0, j)),
        ],
        out_specs=pl.BlockSpec((rows, tn), lambda j: (0, j)),
        compiler_params=_params(("arbitrary",)),
        name="adaln",
    )(cond, w_mod, b_mod.reshape(1, n))


def _qkv_kernel(x_ref, sh_ref, sc_ref, g_ref, w_ref, qg_ref, kg_ref, p1_ref, p2_ref, o_ref):
    d = x_ref.shape[-1]
    h = _rms_mod(x_ref[0], g_ref[...], sh_ref[0], sc_ref[0]).astype(BF16)
    for j, gain_ref in ((0, qg_ref), (1, kg_ref)):
        t = jnp.dot(h, w_ref[:, j * d:(j + 1) * d], preferred_element_type=F32)
        ms = jnp.dot((t * t).astype(BF16), p1_ref[...], preferred_element_type=F32)
        r = lax.rsqrt(ms + NORM_EPS)
        r_hi = r.astype(BF16)
        r_lo = (r - r_hi.astype(F32)).astype(BF16)
        rr = jnp.dot(jnp.concatenate([r_hi, r_lo], axis=-1), p2_ref[...], preferred_element_type=F32)
        o_ref[0, :, j * d:(j + 1) * d] = (t * rr * gain_ref[...]).astype(BF16)
    v = jnp.dot(h, w_ref[:, 2 * d:3 * d], preferred_element_type=F32)
    o_ref[0, :, 2 * d:3 * d] = v.astype(BF16)


def _qkv(x, shift, scale, gain, w_qkv, q_gain, k_gain, tm):
    b, l, d = x.shape
    heads = d // HEAD_DIM
    per_batch = shift.shape[0] == b
    mod_map = (lambda bi, i: (bi, 0, 0)) if per_batch else (lambda bi, i: (0, 0, 0))
    p1 = np.zeros((d, LANES), np.float32)
    p1[np.arange(d), np.arange(d) // HEAD_DIM] = 1.0 / HEAD_DIM
    p2 = np.zeros((2 * LANES, d), np.float32)
    p2[np.arange(d) // HEAD_DIM, np.arange(d)] = 1.0
    p2[LANES + np.arange(d) // HEAD_DIM, np.arange(d)] = 1.0
    qg = (jnp.tile(q_gain, heads) * (HEAD_DIM ** -0.5 * LOG2E)).reshape(1, d)
    kg = jnp.tile(k_gain, heads).reshape(1, d)
    return pl.pallas_call(
        _qkv_kernel,
        out_shape=jax.ShapeDtypeStruct((b, l, 3 * d), BF16),
        grid=(b, l // tm),
        in_specs=[
            pl.BlockSpec((1, tm, d), lambda bi, i: (bi, i, 0)),
            pl.BlockSpec((1, 1, d), mod_map),
            pl.BlockSpec((1, 1, d), mod_map),
            _const_spec((1, d)),
            _const_spec((d, 3 * d)),
            _const_spec((1, d)),
            _const_spec((1, d)),
            _const_spec((d, LANES)),
            _const_spec((2 * LANES, d)),
        ],
        out_specs=pl.BlockSpec((1, tm, 3 * d), lambda bi, i: (bi, i, 0)),
        compiler_params=_params(("parallel", "parallel")),
        name="qkv",
    )(x, shift, scale, gain.reshape(1, d), w_qkv.astype(BF16), qg, kg,
      jnp.asarray(p1, BF16), jnp.asarray(p2, BF16))


def _toeplitz_kernel(r_ref, oh_ref, m_ref, o_ref):
    o_ref[...] = jnp.dot(r_ref[...], oh_ref[...], preferred_element_type=F32, precision=HIGHEST) + m_ref[...]


def _col_bias(rpb):
    heads, n_dr, n_dc = rpb.shape
    k_pad = 32
    qc = np.arange(GRID_W)[:, None]
    kc = np.arange(GRID_W)[None, :]
    c0 = np.clip(qc - WIN_W // 2, 0, GRID_W - WIN_W)
    valid = (kc >= c0) & (kc < c0 + WIN_W)
    dc = kc - qc + (WIN_W - 1)
    onehot = np.zeros((k_pad, GRID_W, GRID_W), np.float32)
    for j in range(n_dc):
        onehot[j] = ((dc == j) & valid).astype(np.float32)
    onehot = onehot.reshape(k_pad, GRID_W * GRID_W)
    mask = np.where(valid, 0.0, NEG).astype(np.float32).reshape(1, GRID_W * GRID_W)
    rows = heads * n_dr
    rows_pad = -(-rows // 8) * 8
    r2 = jnp.zeros((rows_pad, k_pad), F32).at[:rows, :n_dc].set(rpb.reshape(rows, n_dc))
    t = pl.pallas_call(
        _toeplitz_kernel,
        out_shape=jax.ShapeDtypeStruct((rows_pad, GRID_W * GRID_W), F32),
        name="rpb_toeplitz",
    )(r2, jnp.asarray(onehot), jnp.asarray(mask))
    return t[:rows].reshape(heads, n_dr, GRID_W, GRID_W)


def _block_bias(t, n_rows):
    heads = t.shape[0]
    kh = min(WIN_H, n_rows)
    masked = jnp.full((heads, GRID_W, GRID_W), NEG, F32)
    classes = []
    for q0, k0 in ((0, 0), (Q_ROWS, Q_ROWS - WIN_H // 2), (n_rows - Q_ROWS, n_rows - K_ROWS)):
        strips = []
        for qr in range(Q_ROWS):
            r = q0 + qr
            r0 = min(max(r - kh // 2, 0), n_rows - kh)
            blocks = []
            for kr in range(K_ROWS):
                kk = k0 + kr
                blocks.append(t[:, kk - r + WIN_H - 1] if r0 <= kk < r0 + kh else masked)
            strips.append(jnp.concatenate(blocks, axis=-1))
        classes.append(jnp.concatenate(strips, axis=-2))
    return (jnp.stack(classes) * LOG2E).astype(BF16)


def _na_stages(q_ref, k_ref, v_ref, kc_ref, vc_ref, bias_ref, o_ref, k_scr, v_scr, p_scr, s_new, s_old,
               *, sub_new, sub_old, n_sub):
    tq = q_ref.shape[1] // NA_SUB
    tk = bias_ref.shape[-1]
    n_ctx = kc_ref.shape[1]
    l = k_ref.shape[1]
    first_head = lax.broadcasted_iota(jnp.int32, (1, LANES), 1) < HEAD_DIM

    def window_start(sub):
        return pl.multiple_of(jnp.clip(sub * tq - (tk - tq) // 2, 0, l - tk), tq)

    for j in range(NA_SUB):
        k_scr[j, :tk] = k_ref[0, pl.ds(window_start(sub_new + j), tk), :]
        k_scr[j, tk:] = kc_ref[0]
        q = q_ref[0, j * tq:(j + 1) * tq]
        zero = jnp.zeros_like(q)
        q2 = jnp.concatenate([jnp.where(first_head, q, zero), jnp.where(first_head, zero, q)], axis=0)
        s_new[j] = lax.dot_general(q2, k_scr[j], (((1,), (1,)), ((), ())), preferred_element_type=F32)

    for j in range(NA_SUB):
        sub = sub_old + j
        v_scr[j, :tk, :LANES] = v_ref[0, pl.ds(window_start(sub), tk), :]
        v_scr[j, tk:, :LANES] = vc_ref[0]
        v_scr[j, :, LANES:] = jnp.ones((tk + n_ctx, LANES), BF16)
        edge = jnp.where(sub == 0, 0, jnp.where(sub == n_sub - 1, 2, 1))
        heads = []
        for h in range(2):
            for r in range(0, tq, NA_CHUNK):
                rows = slice(h * tq + r, h * tq + r + NA_CHUNK)
                s_loc = s_old[j, rows, :tk] + bias_ref[edge, h, r:r + NA_CHUNK, :].astype(F32)
                s_ctx = s_old[j, rows, tk:]
                m = jnp.maximum(jnp.max(s_loc, axis=-1, keepdims=True), jnp.max(s_ctx, axis=-1, keepdims=True))
                p_scr[j, rows, :tk] = jnp.exp2((s_loc - m).astype(BF16))
                p_scr[j, rows, tk:] = jnp.exp2((s_ctx - m).astype(BF16))
            o = jnp.dot(p_scr[j, h * tq:(h + 1) * tq], v_scr[j], preferred_element_type=F32)
            heads.append(o[:, :LANES] / o[:, LANES:])
        o_ref[0, j * tq:(j + 1) * tq] = jnp.where(first_head, heads[0], heads[1]).astype(BF16)


def _na_kernel(q_ref, k_ref, v_ref, kc_ref, vc_ref, bias_ref, o_ref, k_scr, v_scr, p_scr, s0, s1, *, nb):
    t = pl.program_id(0)
    last = pl.num_programs(0) - 2
    subs = dict(sub_new=(jnp.minimum(t, last) % nb) * NA_SUB, sub_old=(jnp.clip(t - 1, 0, last) % nb) * NA_SUB,
                n_sub=nb * NA_SUB)
    refs = (q_ref, k_ref, v_ref, kc_ref, vc_ref, bias_ref, o_ref, k_scr, v_scr, p_scr)

    @pl.when(t == 0)
    def _():
        s1[...] = jnp.zeros(s1.shape, F32)

    @pl.when(t % 2 == 0)
    def _():
        _na_stages(*refs, s0, s1, **subs)

    @pl.when(t % 2 == 1)
    def _():
        _na_stages(*refs, s1, s0, **subs)


def _neighbourhood_attention(qkv, qkv_ctx, bias):
    b, l, d3 = qkv.shape
    d = d3 // 3
    n_ctx = qkv_ctx.shape[1]
    pairs = d // LANES
    tq, tk = bias.shape[-2:]
    tb = NA_SUB * tq
    nb = l // tb
    steps = b * pairs * nb

    def decode(t):
        return t // (pairs * nb), (t // nb) % pairs, t % nb

    def lagged(t, lag):
        return decode(jnp.clip(t - lag, 0, steps - 1))

    def q_map(t):
        bi, hp, i = lagged(t, 0)
        return bi, i, hp

    def kv_map(which, lag):
        def index(t):
            bi, hp, _ = lagged(t, lag)
            return bi, 0, which * pairs + hp
        return index

    def bias_map(t):
        return 0, lagged(t, 1)[1], 0, 0

    def out_map(t):
        bi, hp, i = lagged(t, 1)
        return bi, i, hp

    n_keys = tk + n_ctx
    return pl.pallas_call(
        functools.partial(_na_kernel, nb=nb),
        out_shape=jax.ShapeDtypeStruct((b, l, d), BF16),
        grid=(steps + 1,),
        in_specs=[
            pl.BlockSpec((1, tb, LANES), q_map),
            pl.BlockSpec((1, l, LANES), kv_map(1, 0)),
            pl.BlockSpec((1, l, LANES), kv_map(2, 1)),
            pl.BlockSpec((1, n_ctx, LANES), kv_map(1, 0)),
            pl.BlockSpec((1, n_ctx, LANES), kv_map(2, 1)),
            pl.BlockSpec((3, 2, tq, tk), bias_map),
        ],
        out_specs=pl.BlockSpec((1, tb, LANES), out_map),
        scratch_shapes=[
            pltpu.VMEM((NA_SUB, n_keys, LANES), BF16),
            pltpu.VMEM((NA_SUB, n_keys, 2 * LANES), BF16),
            pltpu.VMEM((NA_SUB, 2 * tq, n_keys), BF16),
            pltpu.VMEM((NA_SUB, 2 * tq, n_keys), F32), pltpu.VMEM((NA_SUB, 2 * tq, n_keys), F32),
        ],
        compiler_params=_params(("arbitrary",)),
        name="na_attention",
    )(qkv, qkv, qkv, qkv_ctx, qkv_ctx, bias)


def _halo_rows(prev_ref, main_ref, next_ref, s=0):
    return jnp.concatenate([prev_ref[s], main_ref[s], next_ref[s]], axis=0)


def _zero_outside(u, axis, halo):
    i = pl.program_id(axis)
    rows = u.shape[0]
    keep_top = jnp.where(i == 0, 0.0, 1.0).astype(F32)
    keep_bottom = jnp.where(i == pl.num_programs(axis) - 1, 0.0, 1.0).astype(F32)
    return jnp.concatenate([u[:halo] * keep_top, u[halo:rows - halo], u[rows - halo:] * keep_bottom], axis=0)


def _shift_rows(g, rows):
    return pltpu.roll(g, 1, 0), pltpu.roll(g, rows - 1, 0)


def _pack_pair(hi, lo):
    hi_bits = pltpu.bitcast(hi.astype(BF16).astype(F32), U32)
    lo_bits = pltpu.bitcast(lo.astype(BF16).astype(F32), U32)
    return hi_bits | (lo_bits >> 16)


def _unpack_pair(word):
    return (pltpu.bitcast(word & jnp.uint32(0xFFFF0000), F32), pltpu.bitcast(word << 16, F32))


def _pitch(n2):
    return n2 + 8


def _packed_mix_rows(prev_ref, main_ref, next_ref, n2):
    pitch = _pitch(n2)
    slabs = main_ref.shape[1] // pitch
    words = jnp.concatenate([prev_ref[0]] + [main_ref[0, j * pitch:j * pitch + n2] for j in range(slabs)]
                            + [next_ref[0]], axis=0)
    shift = ((pl.program_id(0) % 2) * 16).astype(U32)
    return pltpu.bitcast((words << shift) & jnp.uint32(0xFFFF0000), F32).astype(BF16)


def _post_kernel(mp_ref, mm_ref, mn_ref, xp_ref, xm_ref, xn_ref, wmix_ref, bmix_ref, g1_ref, n2_ref,
                 sh_ref, sc_ref, g2_ref, wup_ref, cw_ref, cb_ref, wdn_ref, o_ref, *, packed_n2):
    halo = xp_ref.shape[1]
    rows = xm_ref.shape[1] + 2 * halo
    d_ff = wdn_ref.shape[0]
    if packed_n2:
        mix = _packed_mix_rows(mp_ref, mm_ref, mn_ref, packed_n2)
    else:
        mix = _halo_rows(mp_ref, mm_ref, mn_ref)
    x = _halo_rows(xp_ref, xm_ref, xn_ref)
    y = jnp.dot(mix, wmix_ref[...], preferred_element_type=F32) + bmix_ref[...]
    x1 = x + g1_ref[0] * y
    h = _rms_mod(x1, n2_ref[...], sh_ref[0], sc_ref[0]).astype(BF16)
    a = jnp.dot(h, wup_ref[:, :d_ff], preferred_element_type=F32)
    g = _zero_outside(jnp.dot(h, wup_ref[:, d_ff:], preferred_element_type=F32), 1, halo)
    g_prev, g_next = _shift_rows(g, rows)
    gc = (g_prev * cw_ref[0:1] + g * cw_ref[1:2] + g_next * cw_ref[2:3] + cb_ref[...])[halo:rows - halo]
    u = a[halo:rows - halo] * (0.5 * gc * (1.0 + lax.erf(gc * (2.0 ** -0.5))))
    ffn = jnp.dot(u.astype(BF16), wdn_ref[...], preferred_element_type=F32)
    o_ref[0] = x1[halo:rows - halo] + g2_ref[0] * ffn


def _halo_specs(tm, l, d, halo, nb=1):
    nblk = tm // halo
    last = l // halo - 1
    return [
        pl.BlockSpec((nb, halo, d), lambda bi, i: (bi, jnp.maximum(i * nblk - 1, 0), 0)),
        pl.BlockSpec((nb, tm, d), lambda bi, i: (bi, i, 0)),
        pl.BlockSpec((nb, halo, d), lambda bi, i: (bi, jnp.minimum((i + 1) * nblk, last), 0)),
    ]


def _packed_halo_specs(tm, l, d, n2):
    pitch = _pitch(n2)
    tp = tm // n2 * pitch
    last = l // n2 * pitch // 8 - 1
    return [
        pl.BlockSpec((1, 8, d), lambda bi, i: (bi // 2, jnp.maximum(i * (tp // 8) - 2, 0), 0)),
        pl.BlockSpec((1, tp, d), lambda bi, i: (bi // 2, i, 0)),
        pl.BlockSpec((1, 8, d), lambda bi, i: (bi // 2, jnp.minimum((i + 1) * (tp // 8), last), 0)),
    ]


def _post(mix, x, w_mix, b_mix, g1, norm2, sh2, sc2, g2, w_up, conv_w, conv_b, w_down, tm, packed_n2=0):
    b, l, d = x.shape
    d_ff = w_down.shape[0]
    mod = pl.BlockSpec((1, 1, d), lambda bi, i: (bi, 0, 0))
    halo = 8 if packed_n2 else HALO
    mix_specs = _packed_halo_specs(tm, l, d, packed_n2) if packed_n2 else _halo_specs(tm, l, d, halo)
    return pl.pallas_call(
        functools.partial(_post_kernel, packed_n2=packed_n2),
        out_shape=jax.ShapeDtypeStruct((b, l, d), F32),
        grid=(b, l // tm),
        in_specs=mix_specs + _halo_specs(tm, l, d, halo) + [
            _const_spec((d, d)), _const_spec((1, d)), mod, _const_spec((1, d)), mod, mod, mod,
            _const_spec((d, 2 * d_ff)), _const_spec((3, d_ff)), _const_spec((1, d_ff)), _const_spec((d_ff, d)),
        ],
        out_specs=pl.BlockSpec((1, tm, d), lambda bi, i: (bi, i, 0)),
        compiler_params=_params(("parallel", "parallel")),
        name="post_ffn",
    )(mix, mix, mix, x, x, x, w_mix.astype(BF16), b_mix.reshape(1, d), g1, norm2.reshape(1, d), sh2, sc2, g2,
      w_up.astype(BF16), conv_w, conv_b.reshape(1, d_ff), w_down.astype(BF16))


def _hy_in_kernel(xp_ref, xm_ref, xn_ref, sh_ref, sc_ref, g_ref, w_ref, b_ref, cw_ref, cb_ref, x0_ref, z_ref, *, n2):
    halo = xp_ref.shape[1]
    tm = xm_ref.shape[1]
    rows = tm + 2 * halo
    d = xm_ref.shape[-1]
    pitch = _pitch(n2)
    x0s, zs = [], []
    for s in range(2):
        h = _rms_mod(_halo_rows(xp_ref, xm_ref, xn_ref, s), g_ref[...], sh_ref[s], sc_ref[s]).astype(BF16)
        parts = []
        for j in range(3):
            cols = slice(j * d, (j + 1) * d)
            u = _zero_outside(jnp.dot(h, w_ref[:, cols], preferred_element_type=F32) + b_ref[:, cols], 1, halo)
            u_prev, u_next = _shift_rows(u, rows)
            cw = cw_ref[:, cols]
            uc = u_prev * cw[0:1] + u * cw[1:2] + u_next * cw[2:3] + cb_ref[:, cols]
            parts.append(uc[halo:rows - halo])
        x0s.append(parts[0])
        zs.append(parts[2] * parts[1])
    pad = jnp.zeros((pitch - n2, d), U32)
    for ref, pair in ((x0_ref, x0s), (z_ref, zs)):
        words = _pack_pair(pair[0], pair[1])
        for j in range(tm // n2):
            ref[0, j * pitch:j * pitch + n2] = words[j * n2:(j + 1) * n2]
            ref[0, j * pitch + n2:(j + 1) * pitch] = pad


def _hy_in(x, shift, scale, gain, w_in, b_in, short_w, short_b, tm, n2):
    b, l, d = x.shape
    pitch = _pitch(n2)
    tp = tm // n2 * pitch
    mod = pl.BlockSpec((2, 1, d), lambda pi, i: (pi, 0, 0))
    out = pl.BlockSpec((1, tp, d), lambda pi, i: (pi, i, 0))
    shape = jax.ShapeDtypeStruct((b // 2, l // n2 * pitch, d), U32)
    return pl.pallas_call(
        functools.partial(_hy_in_kernel, n2=n2),
        out_shape=(shape, shape),
        grid=(b // 2, l // tm),
        in_specs=_halo_specs(tm, l, d, HALO, 2) + [
            mod, mod, _const_spec((1, d)), _const_spec((d, 3 * d)), _const_spec((1, 3 * d)),
            _const_spec((3, 3 * d)), _const_spec((1, 3 * d)),
        ],
        out_specs=(out, out),
        compiler_params=_params(("parallel", "parallel")),
        name="hyena_in",
    )(x, x, x, shift, scale, gain.reshape(1, d), w_in.astype(BF16), b_in.reshape(1, 3 * d),
      short_w, short_b.reshape(1, 3 * d))


def _filter_kernel(bands_ref, w1t_ref, w1c_ref, w1s_ref, b1_ref, w2_ref, b2_ref, w3_ref, b3_ref, wo_ref,
                   freq_ref, delta_ref, hf_ref, hb_ref, ss_ref, *, seq_len):
    tl = hf_ref.shape[0]
    d = hf_ref.shape[1]
    pos = (pl.program_id(0) * tl + lax.broadcasted_iota(jnp.int32, (tl, 1), 0)).astype(F32)
    t = pos * (1.0 / (seq_len - 1))
    ang = bands_ref[...] * ((2.0 * math.pi / seq_len) * pos)
    freq = freq_ref[...]
    dot = functools.partial(jnp.dot, preferred_element_type=F32, precision=HIGHEST)
    pre = t * w1t_ref[...] + dot(jnp.cos(ang), w1c_ref[...]) - dot(jnp.sin(ang), w1s_ref[...])
    hdn = jnp.sin(freq * (pre + b1_ref[...]))
    hdn = jnp.sin(freq * (dot(hdn, w2_ref[...]) + b2_ref[...]))
    hdn = jnp.sin(freq * (dot(hdn, w3_ref[...]) + b3_ref[...]))
    decay = jnp.exp(-t * delta_ref[...])
    hf = dot(hdn, wo_ref[:, :d]) * decay
    hb = dot(hdn, wo_ref[:, d:]) * decay
    hf_ref[...] = hf
    hb_ref[...] = jnp.where(pos > 0.0, hb, 0.0)

    @pl.when(pl.program_id(0) == 0)
    def _():
        ss_ref[...] = jnp.zeros_like(ss_ref)

    ss_ref[...] += jnp.sum(hf * hf + hb * hb, axis=0, keepdims=True)


def _hyena_filter(seq_len, d, w1, b1, w2, b2, w3, b3, w_out, freq, tl):
    width = w2.shape[0]
    bands = np.linspace(1e-4, HY_BANDS - 1, HY_BANDS, dtype=np.float32).reshape(1, HY_BANDS)
    deltas = np.abs(np.linspace(math.log(HY_DECAY_TARGET) / HY_SHORT_DECAY_PCT,
                                math.log(HY_DECAY_TARGET) / HY_LONG_DECAY_PCT, d, dtype=np.float32)).reshape(1, d)
    small = [
        (1, HY_BANDS), (1, width), (HY_BANDS, width), (HY_BANDS, width), (1, width), (width, width), (1, width),
        (width, width), (1, width), (width, 2 * d), (1, width), (1, d),
    ]
    return pl.pallas_call(
        functools.partial(_filter_kernel, seq_len=seq_len),
        out_shape=(jax.ShapeDtypeStruct((seq_len, d), F32), jax.ShapeDtypeStruct((seq_len, d), F32),
                   jax.ShapeDtypeStruct((1, d), F32)),
        grid=(seq_len // tl,),
        in_specs=[pl.BlockSpec(s, lambda i: (0, 0)) for s in small],
        out_specs=(pl.BlockSpec((tl, d), lambda i: (i, 0)), pl.BlockSpec((tl, d), lambda i: (i, 0)),
                   pl.BlockSpec((1, d), lambda i: (0, 0))),
        compiler_params=_params(("arbitrary",)),
        name="hyena_filter",
    )(jnp.asarray(bands), w1[0:1], w1[1:1 + HY_BANDS], w1[1 + HY_BANDS:], b1.reshape(1, width), w2,
      b2.reshape(1, width), w3, b3.reshape(1, width), w_out, freq.reshape(1, width), jnp.asarray(deltas))


def _dft_tables(n1):
    n = n1 * n1
    half = n1 // 2
    idx = np.arange(n1)
    ang = 2.0 * np.pi * np.outer(idx, idx) / n1
    c, s = np.cos(ang), np.sin(ang)
    fa_pair = np.block([[c[:, :half], s[:, :half]], [-s[:, :half], c[:, :half]]])
    zero = np.zeros((n1, half))
    fa_real = np.block([[c[:, :half], zero], [-s[:, :half], zero], [zero, c[:, :half]], [zero, -s[:, :half]]])
    fa_inv = np.block([[c[:half], -s[:half]], [s[:half], c[:half]]]) / n
    k1 = idx[:, None, None]
    k2 = idx[None, :, None]
    n2 = idx[None, None, :]
    m = (n2 * (k1 + n1 * k2)) % n
    gang = 2.0 * np.pi * m / n
    gr, gi = np.cos(gang), -np.sin(gang)
    g_fwd = np.concatenate([np.concatenate([gr, -gi], axis=2), np.concatenate([gi, gr], axis=2)], axis=1)
    g_inv = np.transpose(g_fwd, (0, 2, 1))
    to = lambda a: jnp.asarray(a.astype(np.float32), BF16)
    return to(fa_pair), to(fa_real), to(fa_inv), to(g_fwd), to(g_inv)


def _filter_spectrum_kernel(hf_ref, hb_ref, ss_ref, fa_ref, g_ref, o_ref, a_scr):
    n1 = fa_ref.shape[0] // 4
    n2 = g_ref.shape[1] // 2
    half = hf_ref.shape[0] // n2
    pitch = _pitch(n2)
    kb = g_ref.shape[0]

    @pl.when(pl.program_id(1) == 0)
    def _():
        def body(j, carry):
            rows = jnp.concatenate([hf_ref[pl.ds(j, half, stride=n2), :], hb_ref[pl.ds(j, half, stride=n2), :]], axis=0)
            r = jnp.dot(fa_ref[...], rows.astype(BF16), preferred_element_type=F32)
            a_scr[0, pl.ds(j, n1, stride=pitch), :] = _pack_pair(r[:n1], r[n1:2 * n1])
            a_scr[1, pl.ds(j, n1, stride=pitch), :] = _pack_pair(r[2 * n1:3 * n1], r[3 * n1:])
            return carry

        lax.fori_loop(0, n2, body, 0, unroll=DFT_UNROLL)

    scale = lax.rsqrt(ss_ref[...] + NORM_EPS)
    for k in range(kb):
        row0 = pl.multiple_of((pl.program_id(1) * kb + k) * pitch, 8)
        fr, fi = _unpack_pair(a_scr[0, pl.ds(row0, n2), :])
        br, bi = _unpack_pair(a_scr[1, pl.ds(row0, n2), :])
        a = jnp.concatenate([jnp.concatenate([fr, fi], axis=0), jnp.concatenate([br, bi], axis=0)], axis=1)
        p = jnp.dot(g_ref[k], a.astype(BF16), preferred_element_type=F32)
        lanes = p.shape[1] // 2
        o_ref[k, :n2] = (p[:n2, :lanes] + p[:n2, lanes:]) * scale
        o_ref[k, n2:] = (p[n2:, :lanes] - p[n2:, lanes:]) * scale


def _filter_spectrum(fa_filt, g_fwd, hf, hb, energy, kb):
    n1 = g_fwd.shape[0]
    n2 = g_fwd.shape[1] // 2
    l, d = hf.shape
    tc = LANES
    half_spec = pl.BlockSpec((l, tc), lambda c, k: (0, c))
    return pl.pallas_call(
        _filter_spectrum_kernel,
        out_shape=jax.ShapeDtypeStruct((n1, 2 * n2, d), F32),
        grid=(d // tc, n1 // kb),
        in_specs=[
            half_spec, half_spec, pl.BlockSpec((1, tc), lambda c, k: (0, c)),
            _const_spec(fa_filt.shape),
            pl.BlockSpec((kb, 2 * n2, 2 * n2), lambda c, k: (k, 0, 0)),
        ],
        out_specs=pl.BlockSpec((kb, 2 * n2, tc), lambda c, k: (k, 0, c)),
        scratch_shapes=[pltpu.VMEM((2, n1 * _pitch(n2), tc), U32)],
        compiler_params=_params(("parallel", "arbitrary")),
        name="filter_spectrum",
    )(hf, hb, energy, fa_filt, g_fwd)


def _long_conv_kernel(z_ref, x0_ref, hf_ref, gf_ref, gi_ref, fa_ref, fi_ref, db_ref, o_ref, a_scr):
    n1 = fa_ref.shape[0] // 2
    n2 = gf_ref.shape[1] // 2
    half = n1 // 2
    pitch = _pitch(n2)
    kb = gf_ref.shape[0]
    step = pl.program_id(2)

    @pl.when(step == 0)
    def _():
        def body(j, carry):
            zr, zi = _unpack_pair(z_ref[0, pl.ds(j, half, stride=pitch), :])
            rows = jnp.concatenate([zr, zi], axis=0).astype(BF16)
            r = jnp.dot(fa_ref[...], rows, preferred_element_type=F32)
            a_scr[pl.ds(j, n1, stride=pitch), :] = _pack_pair(r[:n1], r[n1:])
            return carry

        lax.fori_loop(0, n2, body, 0, unroll=DFT_UNROLL)

    for k in range(kb):
        row0 = pl.multiple_of((step * kb + k) * pitch, 8)
        ar, ai = _unpack_pair(a_scr[pl.ds(row0, n2), :])
        x = jnp.dot(gf_ref[k], jnp.concatenate([ar, ai], axis=0).astype(BF16), preferred_element_type=F32)
        xr, xi = x[:n2], x[n2:]
        hr, hi = hf_ref[k, :n2], hf_ref[k, n2:]
        y = jnp.concatenate([xr * hr - xi * hi, xr * hi + xi * hr], axis=0).astype(BF16)
        t = jnp.dot(gi_ref[k], y, preferred_element_type=F32)
        a_scr[pl.ds(row0, n2), :] = _pack_pair(t[:n2], t[n2:])

    @pl.when(step == pl.num_programs(2) - 1)
    def _():
        def body(j, carry):
            tr, ti = _unpack_pair(a_scr[pl.ds(j, n1, stride=pitch), :])
            y = jnp.dot(fi_ref[...], jnp.concatenate([tr, ti], axis=0).astype(BF16), preferred_element_type=F32)
            z0, z1 = _unpack_pair(z_ref[0, pl.ds(j, half, stride=pitch), :])
            g0, g1 = _unpack_pair(x0_ref[0, pl.ds(j, half, stride=pitch), :])
            db = db_ref[...]
            o_ref[0, pl.ds(j, half, stride=pitch), :] = _pack_pair(g0 * (y[:half] + z0 * db), g1 * (y[half:] + z1 * db))
            return carry

        lax.fori_loop(0, n2, body, 0, unroll=DFT_UNROLL_OUT)
        pad = jnp.zeros((pitch - n2, o_ref.shape[2]), U32)
        for s in range(half):
            o_ref[0, s * pitch + n2:(s + 1) * pitch] = pad


def _long_conv_gate(x0p, zp, hf, hb, energy, d_bias, l):
    p, lp, d = zp.shape
    n1 = math.isqrt(2 * l)
    assert n1 * n1 == 2 * l and lp == l // n1 * _pitch(n1)
    kb = min(16, n1)
    tc = LANES
    fa_pair, fa_real, fa_inv, g_fwd, g_inv = _dft_tables(n1)
    spec = _filter_spectrum(fa_real, g_fwd, hf, hb, energy, kb)
    seq = pl.BlockSpec((1, lp, tc), lambda c, pi, k: (pi, 0, c))
    g_blk = pl.BlockSpec((kb, 2 * n1, 2 * n1), lambda c, pi, k: (k, 0, 0))
    return pl.pallas_call(
        _long_conv_kernel,
        out_shape=jax.ShapeDtypeStruct((p, lp, d), U32),
        grid=(d // tc, p, n1 // kb),
        in_specs=[
            seq, seq, pl.BlockSpec((kb, 2 * n1, tc), lambda c, pi, k: (k, 0, c)), g_blk, g_blk,
            _const_spec(fa_pair.shape), _const_spec(fa_inv.shape), pl.BlockSpec((1, tc), lambda c, pi, k: (0, c)),
        ],
        out_specs=seq,
        scratch_shapes=[pltpu.VMEM((n1 * _pitch(n1), tc), U32)],
        compiler_params=_params(("parallel", "parallel", "arbitrary")),
        name="long_conv",
    )(zp, x0p, spec, g_fwd, g_inv, fa_pair, fa_inv, d_bias.reshape(1, d))


def _mod_rows(mod, lo, hi, d):
    m = mod[lo:hi]
    return [m[:, None, j * d:(j + 1) * d] for j in range(N_MOD)]


def kernel(x, c, ctx, c_ctx, l0_w_mod, l0_b_mod, l0_norm1, l0_norm2, l0_na_w_qkv, l0_na_q_gain, l0_na_k_gain, l0_na_rpb, l0_na_w_o, l0_ffn_w_up, l0_ffn_conv_w, l0_ffn_conv_b, l0_ffn_w_down, l1_w_mod, l1_b_mod, l1_norm1, l1_norm2, l1_hy_w_in, l1_hy_b_in, l1_hy_short_w, l1_hy_short_b, l1_hy_f_w1, l1_hy_f_b1, l1_hy_f_w2, l1_hy_f_b2, l1_hy_f_w3, l1_hy_f_b3, l1_hy_f_wout, l1_hy_f_freq, l1_hy_d_bias, l1_hy_w_out, l1_hy_b_out, l1_ffn_w_up, l1_ffn_conv_w, l1_ffn_conv_b, l1_ffn_w_down):
    b, l, d = x.shape
    n_ctx = ctx.shape[1]
    n_rows = l // GRID_W
    assert n_rows >= K_ROWS and n_rows % (Q_ROWS * NA_SUB) == 0 and b % 2 == 0
    tm = min(512, l)
    tm_qkv = min(1024, l)
    n_dft = math.isqrt(2 * l)

    cond = jnp.zeros((8, d), F32).at[:b].set(c).at[b].set(c_ctx)

    mod = _adaln(cond, l0_w_mod, l0_b_mod)
    sh1, sc1, g1, sh2, sc2, g2 = _mod_rows(mod, 0, b, d)
    csh1, csc1 = _mod_rows(mod, b, b + 1, d)[:2]
    qkv = _qkv(x, sh1, sc1, l0_norm1, l0_na_w_qkv, l0_na_q_gain, l0_na_k_gain, tm_qkv)
    qkv_ctx = _qkv(ctx, csh1, csc1, l0_norm1, l0_na_w_qkv, l0_na_q_gain, l0_na_k_gain, n_ctx)
    bias = _block_bias(_col_bias(l0_na_rpb), n_rows)
    attn = _neighbourhood_attention(qkv, qkv_ctx, bias)
    x = _post(attn, x, l0_na_w_o, jnp.zeros((d,), F32), g1, l0_norm2, sh2, sc2, g2,
              l0_ffn_w_up, l0_ffn_conv_w, l0_ffn_conv_b, l0_ffn_w_down, tm)

    mod = _adaln(cond, l1_w_mod, l1_b_mod)
    sh1, sc1, g1, sh2, sc2, g2 = _mod_rows(mod, 0, b, d)
    x0p, zp = _hy_in(x, sh1, sc1, l1_norm1, l1_hy_w_in, l1_hy_b_in, l1_hy_short_w, l1_hy_short_b, tm, n_dft)
    hf, hb, energy = _hyena_filter(l, d, l1_hy_f_w1, l1_hy_f_b1, l1_hy_f_w2, l1_hy_f_b2, l1_hy_f_w3, l1_hy_f_b3,
                                   l1_hy_f_wout, l1_hy_f_freq, min(1024, l))
    gated = _long_conv_gate(x0p, zp, hf, hb, energy, l1_hy_d_bias, l)
    x = _post(gated, x, l1_hy_w_out, l1_hy_b_out, g1, l1_norm2, sh2, sc2, g2,
              l1_ffn_w_up, l1_ffn_conv_w, l1_ffn_conv_b, l1_ffn_w_down, tm, packed_n2=n_dft)
    return x
```

```python
import functools
import math

import numpy as np
import jax
import jax.numpy as jnp
from jax import lax
from jax.experimental import pallas as pl
from jax.experimental.pallas import tpu as pltpu

F32 = jnp.float32
BF16 = jnp.bfloat16
U32 = jnp.uint32
HIGHEST = lax.Precision.HIGHEST

NORM_EPS = 1e-6
N_MOD = 6
HEAD_DIM = 64
GRID_W = 64
WIN_H = 8
WIN_W = 16
HY_BANDS = 16
HY_DECAY_TARGET = 1e-2
HY_SHORT_DECAY_PCT = 0.3
HY_LONG_DECAY_PCT = 1.5

LANES = 128
HALO = 16
Q_ROWS = 4
K_ROWS = Q_ROWS + WIN_H
NA_SUB = 4
NA_CHUNK = 32
DFT_UNROLL = 16
DFT_UNROLL_OUT = 8
LOG2E = math.log2(math.e)
NEG = -1e30
VMEM_LIMIT = 56 * 1024 * 1024


def _params(sem):
    return pltpu.CompilerParams(dimension_semantics=sem, vmem_limit_bytes=VMEM_LIMIT)


def _const_spec(shape):
    return pl.BlockSpec(shape, lambda *_: (0,) * len(shape), pipeline_mode=pl.Buffered(1))


def _rms_mod(x, gain, shift, scale):
    ms = jnp.mean(x * x, axis=-1, keepdims=True)
    y = x * lax.rsqrt(ms + NORM_EPS) * gain
    return y * (1.0 + scale) + shift


def _adaln_kernel(c_ref, w_ref, b_ref, o_ref):
    c = c_ref[...]
    s = c / (1.0 + jnp.exp(-c))
    o_ref[...] = jnp.dot(s, w_ref[...], preferred_element_type=F32, precision=HIGHEST) + b_ref[...]


def _adaln(cond, w_mod, b_mod):
    rows, d = cond.shape
    n = w_mod.shape[1]
    tn = d
    return pl.pallas_call(
        _adaln_kernel,
        out_shape=jax.ShapeDtypeStruct((rows, n), F32),
        grid=(n // tn,),
        in_specs=[
            pl.BlockSpec((rows, d), lambda j: (0, 0)),
            pl.BlockSpec((d, tn), lambda j: (0, j)),
            pl.BlockSpec((1, tn), lambda j: (0, j)),
        ],
        out_specs=pl.BlockSpec((rows, tn), lambda j: (0, j)),
        compiler_params=_params(("arbitrary",)),
        name="adaln",
    )(cond, w_mod, b_mod.reshape(1, n))


def _qkv_kernel(x_ref, sh_ref, sc_ref, g_ref, w_ref, qg_ref, kg_ref, p1_ref, p2_ref, o_ref):
    d = x_ref.shape[-1]
    h = _rms_mod(x_ref[0], g_ref[...], sh_ref[0], sc_ref[0]).astype(BF16)
    for j, gain_ref in ((0, qg_ref), (1, kg_ref)):
        t = jnp.dot(h, w_ref[:, j * d:(j + 1) * d], preferred_element_type=F32)
        ms = jnp.dot((t * t).astype(BF16), p1_ref[...], preferred_element_type=F32)
        r = lax.rsqrt(ms + NORM_EPS)
        r_hi = r.astype(BF16)
        r_lo = (r - r_hi.astype(F32)).astype(BF16)
        rr = jnp.dot(jnp.concatenate([r_hi, r_lo], axis=-1), p2_ref[...], preferred_element_type=F32)
        o_ref[0, :, j * d:(j + 1) * d] = (t * rr * gain_ref[...]).astype(BF16)
    v = jnp.dot(h, w_ref[:, 2 * d:3 * d], preferred_element_type=F32)
    o_ref[0, :, 2 * d:3 * d] = v.astype(BF16)


def _qkv(x, shift, scale, gain, w_qkv, q_gain, k_gain, tm):
    b, l, d = x.shape
    heads = d // HEAD_DIM
    per_batch = shift.shape[0] == b
    mod_map = (lambda bi, i: (bi, 0, 0)) if per_batch else (lambda bi, i: (0, 0, 0))
    p1 = np.zeros((d, LANES), np.float32)
    p1[np.arange(d), np.arange(d) // HEAD_DIM] = 1.0 / HEAD_DIM
    p2 = np.zeros((2 * LANES, d), np.float32)
    p2[np.arange(d) // HEAD_DIM, np.arange(d)] = 1.0
    p2[LANES + np.arange(d) // HEAD_DIM, np.arange(d)] = 1.0
    qg = (jnp.tile(q_gain, heads) * (HEAD_DIM ** -0.5 * LOG2E)).reshape(1, d)
    kg = jnp.tile(k_gain, heads).reshape(1, d)
    return pl.pallas_call(
        _qkv_kernel,
        out_shape=jax.ShapeDtypeStruct((b, l, 3 * d), BF16),
        grid=(b, l // tm),
        in_specs=[
            pl.BlockSpec((1, tm, d), lambda bi, i: (bi, i, 0)),
            pl.BlockSpec((1, 1, d), mod_map),
            pl.BlockSpec((1, 1, d), mod_map),
            _const_spec((1, d)),
            _const_spec((d, 3 * d)),
            _const_spec((1, d)),
            _const_spec((1, d)),
            _const_spec((d, LANES)),
            _const_spec((2 * LANES, d)),
        ],
        out_specs=pl.BlockSpec((1, tm, 3 * d), lambda bi, i: (bi, i, 0)),
        compiler_params=_params(("parallel", "parallel")),
        name="qkv",
    )(x, shift, scale, gain.reshape(1, d), w_qkv.astype(BF16), qg, kg,
      jnp.asarray(p1, BF16), jnp.asarray(p2, BF16))


def _toeplitz_kernel(r_ref, oh_ref, m_ref, o_ref):
    o_ref[...] = jnp.dot(r_ref[...], oh_ref[...], preferred_element_type=F32, precision=HIGHEST) + m_ref[...]


def _col_bias(rpb):
    heads, n_dr, n_dc = rpb.shape
    k_pad = 32
    qc = np.arange(GRID_W)[:, None]
    kc = np.arange(GRID_W)[None, :]
    c0 = np.clip(qc - WIN_W // 2, 0, GRID_W - WIN_W)
    valid = (kc >= c0) & (kc < c0 + WIN_W)
    dc = kc - qc + (WIN_W - 1)
    onehot = np.zeros((k_pad, GRID_W, GRID_W), np.float32)
    for j in range(n_dc):
        onehot[j] = ((dc == j) & valid).astype(np.float32)
    onehot = onehot.reshape(k_pad, GRID_W * GRID_W)
    mask = np.where(valid, 0.0, NEG).astype(np.float32).reshape(1, GRID_W * GRID_W)
    rows = heads * n_dr
    rows_pad = -(-rows // 8) * 8
    r2 = jnp.zeros((rows_pad, k_pad), F32).at[:rows, :n_dc].set(rpb.reshape(rows, n_dc))
    t = pl.pallas_call(
        _toeplitz_kernel,
        out_shape=jax.ShapeDtypeStruct((rows_pad, GRID_W * GRID_W), F32),
        name="rpb_toeplitz",
    )(r2, jnp.asarray(onehot), jnp.asarray(mask))
    return t[:rows].reshape(heads, n_dr, GRID_W, GRID_W)


def _block_bias(t, n_rows):
    heads = t.shape[0]
    kh = min(WIN_H, n_rows)
    masked = jnp.full((heads, GRID_W, GRID_W), NEG, F32)
    classes = []
    for q0, k0 in ((0, 0), (Q_ROWS, Q_ROWS - WIN_H // 2), (n_rows - Q_ROWS, n_rows - K_ROWS)):
        strips = []
        for qr in range(Q_ROWS):
            r = q0 + qr
            r0 = min(max(r - kh // 2, 0), n_rows - kh)
            blocks = []
            for kr in range(K_ROWS):
                kk = k0 + kr
                blocks.append(t[:, kk - r + WIN_H - 1] if r0 <= kk < r0 + kh else masked)
            strips.append(jnp.concatenate(blocks, axis=-1))
        classes.append(jnp.concatenate(strips, axis=-2))
    return (jnp.stack(classes) * LOG2E).astype(BF16)


def _na_stages(q_ref, k_ref, v_ref, kc_ref, vc_ref, bias_ref, o_ref, k_scr, v_scr, p_scr, s_new, s_old,
               *, sub_new, sub_old, n_sub):
    tq = q_ref.shape[1] // NA_SUB
    tk = bias_ref.shape[-1]
    n_ctx = kc_ref.shape[1]
    l = k_ref.shape[1]
    first_head = lax.broadcasted_iota(jnp.int32, (1, LANES), 1) < HEAD_DIM

    def window_start(sub):
        return pl.multiple_of(jnp.clip(sub * tq - (tk - tq) // 2, 0, l - tk), tq)

    for j in range(NA_SUB):
        k_scr[j, :tk] = k_ref[0, pl.ds(window_start(sub_new + j), tk), :]
        k_scr[j, tk:] = kc_ref[0]
        q = q_ref[0, j * tq:(j + 1) * tq]
        zero = jnp.zeros_like(q)
        q2 = jnp.concatenate([jnp.where(first_head, q, zero), jnp.where(first_head, zero, q)], axis=0)
        s_new[j] = lax.dot_general(q2, k_scr[j], (((1,), (1,)), ((), ())), preferred_element_type=F32)

    for j in range(NA_SUB):
        sub = sub_old + j
        v_scr[j, :tk, :LANES] = v_ref[0, pl.ds(window_start(sub), tk), :]
        v_scr[j, tk:, :LANES] = vc_ref[0]
        v_scr[j, :, LANES:] = jnp.ones((tk + n_ctx, LANES), BF16)
        edge = jnp.where(sub == 0, 0, jnp.where(sub == n_sub - 1, 2, 1))
        heads = []
        for h in range(2):
            for r in range(0, tq, NA_CHUNK):
                rows = slice(h * tq + r, h * tq + r + NA_CHUNK)
                s_loc = s_old[j, rows, :tk] + bias_ref[edge, h, r:r + NA_CHUNK, :].astype(F32)
                s_ctx = s_old[j, rows, tk:]
                m = jnp.maximum(jnp.max(s_loc, axis=-1, keepdims=True), jnp.max(s_ctx, axis=-1, keepdims=True))
                p_scr[j, rows, :tk] = jnp.exp2((s_loc - m).astype(BF16))
                p_scr[j, rows, tk:] = jnp.exp2((s_ctx - m).astype(BF16))
            o = jnp.dot(p_scr[j, h * tq:(h + 1) * tq], v_scr[j], preferred_element_type=F32)
            heads.append(o[:, :LANES] / o[:, LANES:])
        o_ref[0, j * tq:(j + 1) * tq] = jnp.where(first_head, heads[0], heads[1]).astype(BF16)


def _na_kernel(q_ref, k_ref, v_ref, kc_ref, vc_ref, bias_ref, o_ref, k_scr, v_scr, p_scr, s0, s1, *, nb):
    t = pl.program_id(0)
    last = pl.num_programs(0) - 2
    subs = dict(sub_new=(jnp.minimum(t, last) % nb) * NA_SUB, sub_old=(jnp.clip(t - 1, 0, last) % nb) * NA_SUB,
                n_sub=nb * NA_SUB)
    refs = (q_ref, k_ref, v_ref, kc_ref, vc_ref, bias_ref, o_ref, k_scr, v_scr, p_scr)

    @pl.when(t == 0)
    def _():
        s1[...] = jnp.zeros(s1.shape, F32)

    @pl.when(t % 2 == 0)
    def _():
        _na_stages(*refs, s0, s1, **subs)

    @pl.when(t % 2 == 1)
    def _():
        _na_stages(*refs, s1, s0, **subs)


def _neighbourhood_attention(qkv, qkv_ctx, bias):
    b, l, d3 = qkv.shape
    d = d3 // 3
    n_ctx = qkv_ctx.shape[1]
    pairs = d // LANES
    tq, tk = bias.shape[-2:]
    tb = NA_SUB * tq
    nb = l // tb
    steps = b * pairs * nb

    def decode(t):
        return t // (pairs * nb), (t // nb) % pairs, t % nb

    def lagged(t, lag):
        return decode(jnp.clip(t - lag, 0, steps - 1))

    def q_map(t):
        bi, hp, i = lagged(t, 0)
        return bi, i, hp

    def kv_map(which, lag):
        def index(t):
            bi, hp, _ = lagged(t, lag)
            return bi, 0, which * pairs + hp
        return index

    def bias_map(t):
        return 0, lagged(t, 1)[1], 0, 0

    def out_map(t):
        bi, hp, i = lagged(t, 1)
        return bi, i, hp

    n_keys = tk + n_ctx
    return pl.pallas_call(
        functools.partial(_na_kernel, nb=nb),
        out_shape=jax.ShapeDtypeStruct((b, l, d), BF16),
        grid=(steps + 1,),
        in_specs=[
            pl.BlockSpec((1, tb, LANES), q_map),
            pl.BlockSpec((1, l, LANES), kv_map(1, 0)),
            pl.BlockSpec((1, l, LANES), kv_map(2, 1)),
            pl.BlockSpec((1, n_ctx, LANES), kv_map(1, 0)),
            pl.BlockSpec((1, n_ctx, LANES), kv_map(2, 1)),
            pl.BlockSpec((3, 2, tq, tk), bias_map),
        ],
        out_specs=pl.BlockSpec((1, tb, LANES), out_map),
        scratch_shapes=[
            pltpu.VMEM((NA_SUB, n_keys, LANES), BF16),
            pltpu.VMEM((NA_SUB, n_keys, 2 * LANES), BF16),
            pltpu.VMEM((NA_SUB, 2 * tq, n_keys), BF16),
            pltpu.VMEM((NA_SUB, 2 * tq, n_keys), F32), pltpu.VMEM((NA_SUB, 2 * tq, n_keys), F32),
        ],
        compiler_params=_params(("arbitrary",)),
        name="na_attention",
    )(qkv, qkv, qkv, qkv_ctx, qkv_ctx, bias)


def _halo_rows(prev_ref, main_ref, next_ref, s=0):
    return jnp.concatenate([prev_ref[s], main_ref[s], next_ref[s]], axis=0)


def _zero_outside(u, axis, halo):
    i = pl.program_id(axis)
    rows = u.shape[0]
    keep_top = jnp.where(i == 0, 0.0, 1.0).astype(F32)
    keep_bottom = jnp.where(i == pl.num_programs(axis) - 1, 0.0, 1.0).astype(F32)
    return jnp.concatenate([u[:halo] * keep_top, u[halo:rows - halo], u[rows - halo:] * keep_bottom], axis=0)


def _shift_rows(g, rows):
    return pltpu.roll(g, 1, 0), pltpu.roll(g, rows - 1, 0)


def _pack_pair(hi, lo):
    hi_bits = pltpu.bitcast(hi.astype(BF16).astype(F32), U32)
    lo_bits = pltpu.bitcast(lo.astype(BF16).astype(F32), U32)
    return hi_bits | (lo_bits >> 16)


def _unpack_pair(word):
    return (pltpu.bitcast(word & jnp.uint32(0xFFFF0000), F32), pltpu.bitcast(word << 16, F32))


def _pitch(n2):
    return n2 + 8


def _packed_mix_rows(prev_ref, main_ref, next_ref, n2):
    pitch = _pitch(n2)
    slabs = main_ref.shape[1] // pitch
    words = jnp.concatenate([prev_ref[0]] + [main_ref[0, j * pitch:j * pitch + n2] for j in range(slabs)]
                            + [next_ref[0]], axis=0)
    shift = ((pl.program_id(0) % 2) * 16).astype(U32)
    return pltpu.bitcast((words << shift) & jnp.uint32(0xFFFF0000), F32).astype(BF16)


def _post_kernel(mp_ref, mm_ref, mn_ref, xp_ref, xm_ref, xn_ref, wmix_ref, bmix_ref, g1_ref, n2_ref,
                 sh_ref, sc_ref, g2_ref, wup_ref, cw_ref, cb_ref, wdn_ref, o_ref, *, packed_n2):
    halo = xp_ref.shape[1]
    rows = xm_ref.shape[1] + 2 * halo
    d_ff = wdn_ref.shape[0]
    if packed_n2:
        mix = _packed_mix_rows(mp_ref, mm_ref, mn_ref, packed_n2)
    else:
        mix = _halo_rows(mp_ref, mm_ref, mn_ref)
    x = _halo_rows(xp_ref, xm_ref, xn_ref)
    y = jnp.dot(mix, wmix_ref[...], preferred_element_type=F32) + bmix_ref[...]
    x1 = x + g1_ref[0] * y
    h = _rms_mod(x1, n2_ref[...], sh_ref[0], sc_ref[0]).astype(BF16)
    a = jnp.dot(h, wup_ref[:, :d_ff], preferred_element_type=F32)
    g = _zero_outside(jnp.dot(h, wup_ref[:, d_ff:], preferred_element_type=F32), 1, halo)
    g_prev, g_next = _shift_rows(g, rows)
    gc = (g_prev * cw_ref[0:1] + g * cw_ref[1:2] + g_next * cw_ref[2:3] + cb_ref[...])[halo:rows - halo]
    u = a[halo:rows - halo] * (0.5 * gc * (1.0 + lax.erf(gc * (2.0 ** -0.5))))
    ffn = jnp.dot(u.astype(BF16), wdn_ref[...], preferred_element_type=F32)
    o_ref[0] = x1[halo:rows - halo] + g2_ref[0] * ffn


def _halo_specs(tm, l, d, halo, nb=1):
    nblk = tm // halo
    last = l // halo - 1
    return [
        pl.BlockSpec((nb, halo, d), lambda bi, i: (bi, jnp.maximum(i * nblk - 1, 0), 0)),
        pl.BlockSpec((nb, tm, d), lambda bi, i: (bi, i, 0)),
        pl.BlockSpec((nb, halo, d), lambda bi, i: (bi, jnp.minimum((i + 1) * nblk, last), 0)),
    ]


def _packed_halo_specs(tm, l, d, n2):
    pitch = _pitch(n2)
    tp = tm // n2 * pitch
    last = l // n2 * pitch // 8 - 1
    return [
        pl.BlockSpec((1, 8, d), lambda bi, i: (bi // 2, jnp.maximum(i * (tp // 8) - 2, 0), 0)),
        pl.BlockSpec((1, tp, d), lambda bi, i: (bi // 2, i, 0)),
        pl.BlockSpec((1, 8, d), lambda bi, i: (bi // 2, jnp.minimum((i + 1) * (tp // 8), last), 0)),
    ]


def _post(mix, x, w_mix, b_mix, g1, norm2, sh2, sc2, g2, w_up, conv_w, conv_b, w_down, tm, packed_n2=0):
    b, l, d = x.shape
    d_ff = w_down.shape[0]
    mod = pl.BlockSpec((1, 1, d), lambda bi, i: (bi, 0, 0))
    halo = 8 if packed_n2 else HALO
    mix_specs = _packed_halo_specs(tm, l, d, packed_n2) if packed_n2 else _halo_specs(tm, l, d, halo)
    return pl.pallas_call(
        functools.partial(_post_kernel, packed_n2=packed_n2),
        out_shape=jax.ShapeDtypeStruct((b, l, d), F32),
        grid=(b, l // tm),
        in_specs=mix_specs + _halo_specs(tm, l, d, halo) + [
            _const_spec((d, d)), _const_spec((1, d)), mod, _const_spec((1, d)), mod, mod, mod,
            _const_spec((d, 2 * d_ff)), _const_spec((3, d_ff)), _const_spec((1, d_ff)), _const_spec((d_ff, d)),
        ],
        out_specs=pl.BlockSpec((1, tm, d), lambda bi, i: (bi, i, 0)),
        compiler_params=_params(("parallel", "parallel")),
        name="post_ffn",
    )(mix, mix, mix, x, x, x, w_mix.astype(BF16), b_mix.reshape(1, d), g1, norm2.reshape(1, d), sh2, sc2, g2,
      w_up.astype(BF16), conv_w, conv_b.reshape(1, d_ff), w_down.astype(BF16))


def _hy_in_kernel(xp_ref, xm_ref, xn_ref, sh_ref, sc_ref, g_ref, w_ref, b_ref, cw_ref, cb_ref, x0_ref, z_ref, *, n2):
    halo = xp_ref.shape[1]
    tm = xm_ref.shape[1]
    rows = tm + 2 * halo
    d = xm_ref.shape[-1]
    pitch = _pitch(n2)
    x0s, zs = [], []
    for s in range(2):
        h = _rms_mod(_halo_rows(xp_ref, xm_ref, xn_ref, s), g_ref[...], sh_ref[s], sc_ref[s]).astype(BF16)
        parts = []
        for j in range(3):
            cols = slice(j * d, (j + 1) * d)
            u = _zero_outside(jnp.dot(h, w_ref[:, cols], preferred_element_type=F32) + b_ref[:, cols], 1, halo)
            u_prev, u_next = _shift_rows(u, rows)
            cw = cw_ref[:, cols]
            uc = u_prev * cw[0:1] + u * cw[1:2] + u_next * cw[2:3] + cb_ref[:, cols]
            parts.append(uc[halo:rows - halo])
        x0s.append(parts[0])
        zs.append(parts[2] * parts[1])
    pad = jnp.zeros((pitch - n2, d), U32)
    for ref, pair in ((x0_ref, x0s), (z_ref, zs)):
        words = _pack_pair(pair[0], pair[1])
        for j in range(tm // n2):
            ref[0, j * pitch:j * pitch + n2] = words[j * n2:(j + 1) * n2]
            ref[0, j * pitch + n2:(j + 1) * pitch] = pad


def _hy_in(x, shift, scale, gain, w_in, b_in, short_w, short_b, tm, n2):
    b, l, d = x.shape
    pitch = _pitch(n2)
    tp = tm // n2 * pitch
    mod = pl.BlockSpec((2, 1, d), lambda pi, i: (pi, 0, 0))
    out = pl.BlockSpec((1, tp, d), lambda pi, i: (pi, i, 0))
    shape = jax.ShapeDtypeStruct((b // 2, l // n2 * pitch, d), U32)
    return pl.pallas_call(
        functools.partial(_hy_in_kernel, n2=n2),
        out_shape=(shape, shape),
        grid=(b // 2, l // tm),
        in_specs=_halo_specs(tm, l, d, HALO, 2) + [
            mod, mod, _const_spec((1, d)), _const_spec((d, 3 * d)), _const_spec((1, 3 * d)),
            _const_spec((3, 3 * d)), _const_spec((1, 3 * d)),
        ],
        out_specs=(out, out),
        compiler_params=_params(("parallel", "parallel")),
        name="hyena_in",
    )(x, x, x, shift, scale, gain.reshape(1, d), w_in.astype(BF16), b_in.reshape(1, 3 * d),
      short_w, short_b.reshape(1, 3 * d))


def _filter_kernel(bands_ref, w1t_ref, w1c_ref, w1s_ref, b1_ref, w2_ref, b2_ref, w3_ref, b3_ref, wo_ref,
                   freq_ref, delta_ref, hf_ref, hb_ref, ss_ref, *, seq_len):
    tl = hf_ref.shape[0]
    d = hf_ref.shape[1]
    pos = (pl.program_id(0) * tl + lax.broadcasted_iota(jnp.int32, (tl, 1), 0)).astype(F32)
    t = pos * (1.0 / (seq_len - 1))
    ang = bands_ref[...] * ((2.0 * math.pi / seq_len) * pos)
    freq = freq_ref[...]
    dot = functools.partial(jnp.dot, preferred_element_type=F32, precision=HIGHEST)
    pre = t * w1t_ref[...] + dot(jnp.cos(ang), w1c_ref[...]) - dot(jnp.sin(ang), w1s_ref[...])
    hdn = jnp.sin(freq * (pre + b1_ref[...]))
    hdn = jnp.sin(freq * (dot(hdn, w2_ref[...]) + b2_ref[...]))
    hdn = jnp.sin(freq * (dot(hdn, w3_ref[...]) + b3_ref[...]))
    decay = jnp.exp(-t * delta_ref[...])
    hf = dot(hdn, wo_ref[:, :d]) * decay
    hb = dot(hdn, wo_ref[:, d:]) * decay
    hf_ref[...] = hf
    hb_ref[...] = jnp.where(pos > 0.0, hb, 0.0)

    @pl.when(pl.program_id(0) == 0)
    def _():
        ss_ref[...] = jnp.zeros_like(ss_ref)

    ss_ref[...] += jnp.sum(hf * hf + hb * hb, axis=0, keepdims=True)


def _hyena_filter(seq_len, d, w1, b1, w2, b2, w3, b3, w_out, freq, tl):
    width = w2.shape[0]
    bands = np.linspace(1e-4, HY_BANDS - 1, HY_BANDS, dtype=np.float32).reshape(1, HY_BANDS)
    deltas = np.abs(np.linspace(math.log(HY_DECAY_TARGET) / HY_SHORT_DECAY_PCT,
                                math.log(HY_DECAY_TARGET) / HY_LONG_DECAY_PCT, d, dtype=np.float32)).reshape(1, d)
    small = [
        (1, HY_BANDS), (1, width), (HY_BANDS, width), (HY_BANDS, width), (1, width), (width, width), (1, width),
        (width, width), (1, width), (width, 2 * d), (1, width), (1, d),
    ]
    return pl.pallas_call(
        functools.partial(_filter_kernel, seq_len=seq_len),
        out_shape=(jax.ShapeDtypeStruct((seq_len, d), F32), jax.ShapeDtypeStruct((seq_len, d), F32),
                   jax.ShapeDtypeStruct((1, d), F32)),
        grid=(seq_len // tl,),
        in_specs=[pl.BlockSpec(s, lambda i: (0, 0)) for s in small],
        out_specs=(pl.BlockSpec((tl, d), lambda i: (i, 0)), pl.BlockSpec((tl, d), lambda i: (i, 0)),
                   pl.BlockSpec((1, d), lambda i: (0, 0))),
        compiler_params=_params(("arbitrary",)),
        name="hyena_filter",
    )(jnp.asarray(bands), w1[0:1], w1[1:1 + HY_BANDS], w1[1 + HY_BANDS:], b1.reshape(1, width), w2,
      b2.reshape(1, width), w3, b3.reshape(1, width), w_out, freq.reshape(1, width), jnp.asarray(deltas))


def _dft_tables(n1):
    n = n1 * n1
    half = n1 // 2
    idx = np.arange(n1)
    ang = 2.0 * np.pi * np.outer(idx, idx) / n1
    c, s = np.cos(ang), np.sin(ang)
    fa_pair = np.block([[c[:, :half], s[:, :half]], [-s[:, :half], c[:, :half]]])
    zero = np.zeros((n1, half))
    fa_real = np.block([[c[:, :half], zero], [-s[:, :half], zero], [zero, c[:, :half]], [zero, -s[:, :half]]])
    fa_inv = np.block([[c[:half], -s[:half]], [s[:half], c[:half]]]) / n
    k1 = idx[:, None, None]
    k2 = idx[None, :, None]
    n2 = idx[None, None, :]
    m = (n2 * (k1 + n1 * k2)) % n
    gang = 2.0 * np.pi * m / n
    gr, gi = np.cos(gang), -np.sin(gang)
    g_fwd = np.concatenate([np.concatenate([gr, -gi], axis=2), np.concatenate([gi, gr], axis=2)], axis=1)
    to = lambda a: jnp.asarray(a.astype(np.float32), BF16)
    return to(fa_pair), to(fa_real), to(fa_inv), to(g_fwd)


def _filter_spectrum_kernel(hf_ref, hb_ref, ss_ref, fa_ref, g_ref, o_ref, a_scr):
    n1 = fa_ref.shape[0] // 4
    n2 = g_ref.shape[1] // 2
    half = hf_ref.shape[0] // n2
    pitch = _pitch(n2)
    kb = g_ref.shape[0]

    @pl.when(pl.program_id(1) == 0)
    def _():
        def body(j, carry):
            rows = jnp.concatenate([hf_ref[pl.ds(j, half, stride=n2), :], hb_ref[pl.ds(j, half, stride=n2), :]], axis=0)
            r = jnp.dot(fa_ref[...], rows.astype(BF16), preferred_element_type=F32)
            a_scr[0, pl.ds(j, n1, stride=pitch), :] = _pack_pair(r[:n1], r[n1:2 * n1])
            a_scr[1, pl.ds(j, n1, stride=pitch), :] = _pack_pair(r[2 * n1:3 * n1], r[3 * n1:])
            return carry

        lax.fori_loop(0, n2, body, 0, unroll=DFT_UNROLL)

    scale = lax.rsqrt(ss_ref[...] + NORM_EPS)
    for k in range(kb):
        row0 = pl.multiple_of((pl.program_id(1) * kb + k) * pitch, 8)
        fr, fi = _unpack_pair(a_scr[0, pl.ds(row0, n2), :])
        br, bi = _unpack_pair(a_scr[1, pl.ds(row0, n2), :])
        a = jnp.concatenate([jnp.concatenate([fr, fi], axis=0), jnp.concatenate([br, bi], axis=0)], axis=1)
        p = jnp.dot(g_ref[k], a.astype(BF16), preferred_element_type=F32)
        lanes = p.shape[1] // 2
        o_ref[k] = _pack_pair((p[:n2, :lanes] + p[:n2, lanes:]) * scale, (p[n2:, :lanes] - p[n2:, lanes:]) * scale)


def _filter_spectrum(fa_filt, g_fwd, hf, hb, energy, kb):
    n1 = g_fwd.shape[0]
    n2 = g_fwd.shape[1] // 2
    l, d = hf.shape
    tc = LANES
    half_spec = pl.BlockSpec((l, tc), lambda c, k: (0, c))
    return pl.pallas_call(
        _filter_spectrum_kernel,
        out_shape=jax.ShapeDtypeStruct((n1, n2, d), U32),
        grid=(d // tc, n1 // kb),
        in_specs=[
            half_spec, half_spec, pl.BlockSpec((1, tc), lambda c, k: (0, c)),
            _const_spec(fa_filt.shape),
            pl.BlockSpec((kb, 2 * n2, 2 * n2), lambda c, k: (k, 0, 0)),
        ],
        out_specs=pl.BlockSpec((kb, n2, tc), lambda c, k: (k, 0, c)),
        scratch_shapes=[pltpu.VMEM((2, n1 * _pitch(n2), tc), U32)],
        compiler_params=_params(("parallel", "arbitrary")),
        name="filter_spectrum",
    )(hf, hb, energy, fa_filt, g_fwd)


def _long_conv_kernel(z_ref, x0_ref, hf_ref, gf_ref, fa_ref, fi_ref, db_ref, o_ref, a_scr):
    n1 = fa_ref.shape[0] // 2
    n2 = gf_ref.shape[1] // 2
    half = n1 // 2
    pitch = _pitch(n2)
    kb = gf_ref.shape[0]
    step = pl.program_id(2)

    @pl.when(step == 0)
    def _():
        def body(j, carry):
            zr, zi = _unpack_pair(z_ref[0, pl.ds(j, half, stride=pitch), :])
            rows = jnp.concatenate([zr, zi], axis=0).astype(BF16)
            r = jnp.dot(fa_ref[...], rows, preferred_element_type=F32)
            a_scr[pl.ds(j, n1, stride=pitch), :] = _pack_pair(r[:n1], r[n1:])
            return carry

        lax.fori_loop(0, n2, body, 0, unroll=DFT_UNROLL)

    for k in range(kb):
        row0 = pl.multiple_of((step * kb + k) * pitch, 8)
        ar, ai = _unpack_pair(a_scr[pl.ds(row0, n2), :])
        x = jnp.dot(gf_ref[k], jnp.concatenate([ar, ai], axis=0).astype(BF16), preferred_element_type=F32)
        xr, xi = x[:n2], x[n2:]
        hr, hi = _unpack_pair(hf_ref[k])
        y = jnp.concatenate([xr * hr - xi * hi, xr * hi + xi * hr], axis=0).astype(BF16)
        t = lax.dot_general(gf_ref[k], y, (((0,), (0,)), ((), ())), preferred_element_type=F32)
        a_scr[pl.ds(row0, n2), :] = _pack_pair(t[:n2], t[n2:])

    @pl.when(step == pl.num_programs(2) - 1)
    def _():
        def body(j, carry):
            tr, ti = _unpack_pair(a_scr[pl.ds(j, n1, stride=pitch), :])
            y = jnp.dot(fi_ref[...], jnp.concatenate([tr, ti], axis=0).astype(BF16), preferred_element_type=F32)
            z0, z1 = _unpack_pair(z_ref[0, pl.ds(j, half, stride=pitch), :])
            g0, g1 = _unpack_pair(x0_ref[0, pl.ds(j, half, stride=pitch), :])
            db = db_ref[...]
            o_ref[0, pl.ds(j, half, stride=pitch), :] = _pack_pair(g0 * (y[:half] + z0 * db), g1 * (y[half:] + z1 * db))
            return carry

        lax.fori_loop(0, n2, body, 0, unroll=DFT_UNROLL_OUT)
        pad = jnp.zeros((pitch - n2, o_ref.shape[2]), U32)
        for s in range(half):
            o_ref[0, s * pitch + n2:(s + 1) * pitch] = pad


def _long_conv_gate(x0p, zp, hf, hb, energy, d_bias, l):
    p, lp, d = zp.shape
    n1 = math.isqrt(2 * l)
    assert n1 * n1 == 2 * l and lp == l // n1 * _pitch(n1)
    kb = min(32, n1)
    tc = LANES
    fa_pair, fa_real, fa_inv, g_fwd = _dft_tables(n1)
    spec = _filter_spectrum(fa_real, g_fwd, hf, hb, energy, kb)
    seq = pl.BlockSpec((1, lp, tc), lambda c, pi, k: (pi, 0, c))
    g_blk = pl.BlockSpec((kb, 2 * n1, 2 * n1), lambda c, pi, k: (k, 0, 0))
    return pl.pallas_call(
        _long_conv_kernel,
        out_shape=jax.ShapeDtypeStruct((p, lp, d), U32),
        grid=(d // tc, p, n1 // kb),
        in_specs=[
            seq, seq, pl.BlockSpec((kb, n1, tc), lambda c, pi, k: (k, 0, c)), g_blk,
            _const_spec(fa_pair.shape), _const_spec(fa_inv.shape), pl.BlockSpec((1, tc), lambda c, pi, k: (0, c)),
        ],
        out_specs=seq,
        scratch_shapes=[pltpu.VMEM((n1 * _pitch(n1), tc), U32)],
        compiler_params=_params(("parallel", "parallel", "arbitrary")),
        name="long_conv",
    )(zp, x0p, spec, g_fwd, fa_pair, fa_inv, d_bias.reshape(1, d))


def _mod_rows(mod, lo, hi, d):
    m = mod[lo:hi]
    return [m[:, None, j * d:(j + 1) * d] for j in range(N_MOD)]


def kernel(x, c, ctx, c_ctx, l0_w_mod, l0_b_mod, l0_norm1, l0_norm2, l0_na_w_qkv, l0_na_q_gain, l0_na_k_gain, l0_na_rpb, l0_na_w_o, l0_ffn_w_up, l0_ffn_conv_w, l0_ffn_conv_b, l0_ffn_w_down, l1_w_mod, l1_b_mod, l1_norm1, l1_norm2, l1_hy_w_in, l1_hy_b_in, l1_hy_short_w, l1_hy_short_b, l1_hy_f_w1, l1_hy_f_b1, l1_hy_f_w2, l1_hy_f_b2, l1_hy_f_w3, l1_hy_f_b3, l1_hy_f_wout, l1_hy_f_freq, l1_hy_d_bias, l1_hy_w_out, l1_hy_b_out, l1_ffn_w_up, l1_ffn_conv_w, l1_ffn_conv_b, l1_ffn_w_down):
    b, l, d = x.shape
    n_ctx = ctx.shape[1]
    n_rows = l // GRID_W
    assert n_rows >= K_ROWS and n_rows % (Q_ROWS * NA_SUB) == 0 and b % 2 == 0
    tm = min(512, l)
    tm_qkv = min(1024, l)
    n_dft = math.isqrt(2 * l)

    cond = jnp.zeros((8, d), F32).at[:b].set(c).at[b].set(c_ctx)

    mod = _adaln(cond, l0_w_mod, l0_b_mod)
    sh1, sc1, g1, sh2, sc2, g2 = _mod_rows(mod, 0, b, d)
    csh1, csc1 = _mod_rows(mod, b, b + 1, d)[:2]
    qkv = _qkv(x, sh1, sc1, l0_norm1, l0_na_w_qkv, l0_na_q_gain, l0_na_k_gain, tm_qkv)
    qkv_ctx = _qkv(ctx, csh1, csc1, l0_norm1, l0_na_w_qkv, l0_na_q_gain, l0_na_k_gain, n_ctx)
    bias = _block_bias(_col_bias(l0_na_rpb), n_rows)
    attn = _neighbourhood_attention(qkv, qkv_ctx, bias)
    x = _post(attn, x, l0_na_w_o, jnp.zeros((d,), F32), g1, l0_norm2, sh2, sc2, g2,
              l0_ffn_w_up, l0_ffn_conv_w, l0_ffn_conv_b, l0_ffn_w_down, tm)

    mod = _adaln(cond, l1_w_mod, l1_b_mod)
    sh1, sc1, g1, sh2, sc2, g2 = _mod_rows(mod, 0, b, d)
    x0p, zp = _hy_in(x, sh1, sc1, l1_norm1, l1_hy_w_in, l1_hy_b_in, l1_hy_short_w, l1_hy_short_b, tm, n_dft)
    hf, hb, energy = _hyena_filter(l, d, l1_hy_f_w1, l1_hy_f_b1, l1_hy_f_w2, l1_hy_f_b2, l1_hy_f_w3, l1_hy_f_b3,
                                   l1_hy_f_wout, l1_hy_f_freq, min(1024, l))
    gated = _long_conv_gate(x0p, zp, hf, hb, energy, l1_hy_d_bias, l)
    x = _post(gated, x, l1_hy_w_out, l1_hy_b_out, g1, l1_norm2, sh2, sc2, g2,
              l1_ffn_w_up, l1_ffn_conv_w, l1_ffn_conv_b, l1_ffn_w_down, tm, packed_n2=n_dft)
    return x
```

```python
import functools
import math

import numpy as np
import jax
import jax.numpy as jnp
from jax import lax
from jax.experimental import pallas as pl
from jax.experimental.pallas import tpu as pltpu

F32 = jnp.float32
BF16 = jnp.bfloat16
U32 = jnp.uint32
HIGHEST = lax.Precision.HIGHEST

NORM_EPS = 1e-6
N_MOD = 6
HEAD_DIM = 64
GRID_W = 64
WIN_H = 8
WIN_W = 16
HY_BANDS = 16
HY_DECAY_TARGET = 1e-2
HY_SHORT_DECAY_PCT = 0.3
HY_LONG_DECAY_PCT = 1.5

LANES = 128
HALO = 16
Q_ROWS = 4
K_ROWS = Q_ROWS + WIN_H
NA_SUB = 4
NA_CHUNK = 32
DFT_UNROLL = 16
DFT_UNROLL_OUT = 8
LOG2E = math.log2(math.e)
NEG = -1e30
VMEM_LIMIT = 56 * 1024 * 1024


def _params(sem):
    return pltpu.CompilerParams(dimension_semantics=sem, vmem_limit_bytes=VMEM_LIMIT)


def _const_spec(shape):
    return pl.BlockSpec(shape, lambda *_: (0,) * len(shape), pipeline_mode=pl.Buffered(1))


def _rms_mod(x, gain, shift, scale):
    ms = jnp.mean(x * x, axis=-1, keepdims=True)
    y = x * lax.rsqrt(ms + NORM_EPS) * gain
    return y * (1.0 + scale) + shift


def _adaln_kernel(c_ref, w_ref, b_ref, o_ref):
    c = c_ref[...]
    s = c / (1.0 + jnp.exp(-c))
    o_ref[...] = jnp.dot(s, w_ref[...], preferred_element_type=F32, precision=HIGHEST) + b_ref[...]


def _adaln(cond, w_mod, b_mod):
    rows, d = cond.shape
    n = w_mod.shape[1]
    tn = d
    return pl.pallas_call(
        _adaln_kernel,
        out_shape=jax.ShapeDtypeStruct((rows, n), F32),
        grid=(n // tn,),
        in_specs=[
            pl.BlockSpec((rows, d), lambda j: (0, 0)),
            pl.BlockSpec((d, tn), lambda j: (0, j)),
            pl.BlockSpec((1, tn), lambda j: (0, j)),
        ],
        out_specs=pl.BlockSpec((rows, tn), lambda j: (0, j)),
        compiler_params=_params(("arbitrary",)),
        name="adaln",
    )(cond, w_mod, b_mod.reshape(1, n))


def _qkv_kernel(x_ref, sh_ref, sc_ref, g_ref, w_ref, qg_ref, kg_ref, p1_ref, p2_ref, o_ref):
    d = x_ref.shape[-1]
    h = _rms_mod(x_ref[0], g_ref[...], sh_ref[0], sc_ref[0]).astype(BF16)
    for j, gain_ref in ((0, qg_ref), (1, kg_ref)):
        t = jnp.dot(h, w_ref[:, j * d:(j + 1) * d], preferred_element_type=F32)
        ms = jnp.dot((t * t).astype(BF16), p1_ref[...], preferred_element_type=F32)
        r = lax.rsqrt(ms + NORM_EPS)
        r_hi = r.astype(BF16)
        r_lo = (r - r_hi.astype(F32)).astype(BF16)
        rr = jnp.dot(jnp.concatenate([r_hi, r_lo], axis=-1), p2_ref[...], preferred_element_type=F32)
        o_ref[0, :, j * d:(j + 1) * d] = (t * rr * gain_ref[...]).astype(BF16)
    v = jnp.dot(h, w_ref[:, 2 * d:3 * d], preferred_element_type=F32)
    o_ref[0, :, 2 * d:3 * d] = v.astype(BF16)


def _qkv(x, shift, scale, gain, w_qkv, q_gain, k_gain, tm):
    b, l, d = x.shape
    heads = d // HEAD_DIM
    per_batch = shift.shape[0] == b
    mod_map = (lambda bi, i: (bi, 0, 0)) if per_batch else (lambda bi, i: (0, 0, 0))
    p1 = np.zeros((d, LANES), np.float32)
    p1[np.arange(d), np.arange(d) // HEAD_DIM] = 1.0 / HEAD_DIM
    p2 = np.zeros((2 * LANES, d), np.float32)
    p2[np.arange(d) // HEAD_DIM, np.arange(d)] = 1.0
    p2[LANES + np.arange(d) // HEAD_DIM, np.arange(d)] = 1.0
    qg = (jnp.tile(q_gain, heads) * (HEAD_DIM ** -0.5 * LOG2E)).reshape(1, d)
    kg = jnp.tile(k_gain, heads).reshape(1, d)
    return pl.pallas_call(
        _qkv_kernel,
        out_shape=jax.ShapeDtypeStruct((b, l, 3 * d), BF16),
        grid=(b, l // tm),
        in_specs=[
            pl.BlockSpec((1, tm, d), lambda bi, i: (bi, i, 0)),
            pl.BlockSpec((1, 1, d), mod_map),
            pl.BlockSpec((1, 1, d), mod_map),
            _const_spec((1, d)),
            _const_spec((d, 3 * d)),
            _const_spec((1, d)),
            _const_spec((1, d)),
            _const_spec((d, LANES)),
            _const_spec((2 * LANES, d)),
        ],
        out_specs=pl.BlockSpec((1, tm, 3 * d), lambda bi, i: (bi, i, 0)),
        compiler_params=_params(("parallel", "parallel")),
        name="qkv",
    )(x, shift, scale, gain.reshape(1, d), w_qkv.astype(BF16), qg, kg,
      jnp.asarray(p1, BF16), jnp.asarray(p2, BF16))


def _toeplitz_kernel(r_ref, oh_ref, m_ref, o_ref):
    o_ref[...] = jnp.dot(r_ref[...], oh_ref[...], preferred_element_type=F32, precision=HIGHEST) + m_ref[...]


def _col_bias(rpb):
    heads, n_dr, n_dc = rpb.shape
    k_pad = 32
    qc = np.arange(GRID_W)[:, None]
    kc = np.arange(GRID_W)[None, :]
    c0 = np.clip(qc - WIN_W // 2, 0, GRID_W - WIN_W)
    valid = (kc >= c0) & (kc < c0 + WIN_W)
    dc = kc - qc + (WIN_W - 1)
    onehot = np.zeros((k_pad, GRID_W, GRID_W), np.float32)
    for j in range(n_dc):
        onehot[j] = ((dc == j) & valid).astype(np.float32)
    onehot = onehot.reshape(k_pad, GRID_W * GRID_W)
    mask = np.where(valid, 0.0, NEG).astype(np.float32).reshape(1, GRID_W * GRID_W)
    rows = heads * n_dr
    rows_pad = -(-rows // 8) * 8
    r2 = jnp.zeros((rows_pad, k_pad), F32).at[:rows, :n_dc].set(rpb.reshape(rows, n_dc))
    t = pl.pallas_call(
        _toeplitz_kernel,
        out_shape=jax.ShapeDtypeStruct((rows_pad, GRID_W * GRID_W), F32),
        name="rpb_toeplitz",
    )(r2, jnp.asarray(onehot), jnp.asarray(mask))
    return t[:rows].reshape(heads, n_dr, GRID_W, GRID_W)


def _block_bias(t, n_rows):
    heads = t.shape[0]
    kh = min(WIN_H, n_rows)
    masked = jnp.full((heads, GRID_W, GRID_W), NEG, F32)
    classes = []
    for q0, k0 in ((0, 0), (Q_ROWS, Q_ROWS - WIN_H // 2), (n_rows - Q_ROWS, n_rows - K_ROWS)):
        strips = []
        for qr in range(Q_ROWS):
            r = q0 + qr
            r0 = min(max(r - kh // 2, 0), n_rows - kh)
            blocks = []
            for kr in range(K_ROWS):
                kk = k0 + kr
                blocks.append(t[:, kk - r + WIN_H - 1] if r0 <= kk < r0 + kh else masked)
            strips.append(jnp.concatenate(blocks, axis=-1))
        classes.append(jnp.concatenate(strips, axis=-2))
    return (jnp.stack(classes) * LOG2E).astype(BF16)


def _na_stages(q_ref, k_ref, v_ref, kc_ref, vc_ref, bias_ref, o_ref, k_scr, v_scr, p_scr, s_new, s_old,
               *, sub_new, sub_old, n_sub):
    tq = q_ref.shape[1] // NA_SUB
    tk = bias_ref.shape[-1]
    n_ctx = kc_ref.shape[1]
    l = k_ref.shape[1]
    first_head = lax.broadcasted_iota(jnp.int32, (1, LANES), 1) < HEAD_DIM

    def window_start(sub):
        return pl.multiple_of(jnp.clip(sub * tq - (tk - tq) // 2, 0, l - tk), tq)

    for j in range(NA_SUB):
        k_scr[j, :tk] = k_ref[0, pl.ds(window_start(sub_new + j), tk), :]
        k_scr[j, tk:] = kc_ref[0]
        q = q_ref[0, j * tq:(j + 1) * tq]
        zero = jnp.zeros_like(q)
        q2 = jnp.concatenate([jnp.where(first_head, q, zero), jnp.where(first_head, zero, q)], axis=0)
        s_new[j] = lax.dot_general(q2, k_scr[j], (((1,), (1,)), ((), ())), preferred_element_type=F32)

    for j in range(NA_SUB):
        sub = sub_old + j
        v_scr[j, :tk, :LANES] = v_ref[0, pl.ds(window_start(sub), tk), :]
        v_scr[j, tk:, :LANES] = vc_ref[0]
        v_scr[j, :, LANES:] = jnp.ones((tk + n_ctx, LANES), BF16)
        edge = jnp.where(sub == 0, 0, jnp.where(sub == n_sub - 1, 2, 1))
        heads = []
        for h in range(2):
            for r in range(0, tq, NA_CHUNK):
                rows = slice(h * tq + r, h * tq + r + NA_CHUNK)
                s_loc = s_old[j, rows, :tk] + bias_ref[edge, h, r:r + NA_CHUNK, :].astype(F32)
                s_ctx = s_old[j, rows, tk:]
                m = jnp.maximum(jnp.max(s_loc, axis=-1, keepdims=True), jnp.max(s_ctx, axis=-1, keepdims=True))
                p_scr[j, rows, :tk] = jnp.exp2((s_loc - m).astype(BF16))
                p_scr[j, rows, tk:] = jnp.exp2((s_ctx - m).astype(BF16))
            o = jnp.dot(p_scr[j, h * tq:(h + 1) * tq], v_scr[j], preferred_element_type=F32)
            heads.append(o[:, :LANES] / o[:, LANES:])
        o_ref[0, j * tq:(j + 1) * tq] = jnp.where(first_head, heads[0], heads[1]).astype(BF16)


def _na_kernel(q_ref, k_ref, v_ref, kc_ref, vc_ref, bias_ref, o_ref, k_scr, v_scr, p_scr, s0, s1, *, nb):
    t = pl.program_id(0)
    last = pl.num_programs(0) - 2
    subs = dict(sub_new=(jnp.minimum(t, last) % nb) * NA_SUB, sub_old=(jnp.clip(t - 1, 0, last) % nb) * NA_SUB,
                n_sub=nb * NA_SUB)
    refs = (q_ref, k_ref, v_ref, kc_ref, vc_ref, bias_ref, o_ref, k_scr, v_scr, p_scr)

    @pl.when(t == 0)
    def _():
        s1[...] = jnp.zeros(s1.shape, F32)

    @pl.when(t % 2 == 0)
    def _():
        _na_stages(*refs, s0, s1, **subs)

    @pl.when(t % 2 == 1)
    def _():
        _na_stages(*refs, s1, s0, **subs)


def _neighbourhood_attention(qkv, qkv_ctx, bias):
    b, l, d3 = qkv.shape
    d = d3 // 3
    n_ctx = qkv_ctx.shape[1]
    pairs = d // LANES
    tq, tk = bias.shape[-2:]
    tb = NA_SUB * tq
    nb = l // tb
    steps = b * pairs * nb

    def decode(t):
        return t // (pairs * nb), (t // nb) % pairs, t % nb

    def lagged(t, lag):
        return decode(jnp.clip(t - lag, 0, steps - 1))

    def q_map(t):
        bi, hp, i = lagged(t, 0)
        return bi, i, hp

    def kv_map(which, lag):
        def index(t):
            bi, hp, _ = lagged(t, lag)
            return bi, 0, which * pairs + hp
        return index

    def bias_map(t):
        return 0, lagged(t, 1)[1], 0, 0

    def out_map(t):
        bi, hp, i = lagged(t, 1)
        return bi, i, hp

    n_keys = tk + n_ctx
    return pl.pallas_call(
        functools.partial(_na_kernel, nb=nb),
        out_shape=jax.ShapeDtypeStruct((b, l, d), BF16),
        grid=(steps + 1,),
        in_specs=[
            pl.BlockSpec((1, tb, LANES), q_map),
            pl.BlockSpec((1, l, LANES), kv_map(1, 0)),
            pl.BlockSpec((1, l, LANES), kv_map(2, 1)),
            pl.BlockSpec((1, n_ctx, LANES), kv_map(1, 0)),
            pl.BlockSpec((1, n_ctx, LANES), kv_map(2, 1)),
            pl.BlockSpec((3, 2, tq, tk), bias_map),
        ],
        out_specs=pl.BlockSpec((1, tb, LANES), out_map),
        scratch_shapes=[
            pltpu.VMEM((NA_SUB, n_keys, LANES), BF16),
            pltpu.VMEM((NA_SUB, n_keys, 2 * LANES), BF16),
            pltpu.VMEM((NA_SUB, 2 * tq, n_keys), BF16),
            pltpu.VMEM((NA_SUB, 2 * tq, n_keys), F32), pltpu.VMEM((NA_SUB, 2 * tq, n_keys), F32),
        ],
        compiler_params=_params(("arbitrary",)),
        name="na_attention",
    )(qkv, qkv, qkv, qkv_ctx, qkv_ctx, bias)


def _halo_rows(prev_ref, main_ref, next_ref, s=0):
    return jnp.concatenate([prev_ref[s], main_ref[s], next_ref[s]], axis=0)


def _zero_outside(u, axis, halo):
    i = pl.program_id(axis)
    rows = u.shape[0]
    keep_top = jnp.where(i == 0, 0.0, 1.0).astype(F32)
    keep_bottom = jnp.where(i == pl.num_programs(axis) - 1, 0.0, 1.0).astype(F32)
    return jnp.concatenate([u[:halo] * keep_top, u[halo:rows - halo], u[rows - halo:] * keep_bottom], axis=0)


def _shift_rows(g, rows):
    return pltpu.roll(g, 1, 0), pltpu.roll(g, rows - 1, 0)


def _pack_pair(hi, lo):
    hi_bits = pltpu.bitcast(hi.astype(BF16).astype(F32), U32)
    lo_bits = pltpu.bitcast(lo.astype(BF16).astype(F32), U32)
    return hi_bits | (lo_bits >> 16)


def _unpack_pair(word):
    return (pltpu.bitcast(word & jnp.uint32(0xFFFF0000), F32), pltpu.bitcast(word << 16, F32))


def _pitch(n2):
    return n2 + 8


def _packed_mix_rows(prev_ref, main_ref, next_ref, n2):
    pitch = _pitch(n2)
    slabs = main_ref.shape[1] // pitch
    words = jnp.concatenate([prev_ref[0]] + [main_ref[0, j * pitch:j * pitch + n2] for j in range(slabs)]
                            + [next_ref[0]], axis=0)
    shift = ((pl.program_id(0) % 2) * 16).astype(U32)
    return pltpu.bitcast((words << shift) & jnp.uint32(0xFFFF0000), F32).astype(BF16)


def _post_kernel(mp_ref, mm_ref, mn_ref, xp_ref, xm_ref, xn_ref, wmix_ref, bmix_ref, g1_ref, n2_ref,
                 sh_ref, sc_ref, g2_ref, wup_ref, cw_ref, cb_ref, wdn_ref, o_ref, *, packed_n2):
    halo = xp_ref.shape[1]
    rows = xm_ref.shape[1] + 2 * halo
    d_ff = wdn_ref.shape[0]
    if packed_n2:
        mix = _packed_mix_rows(mp_ref, mm_ref, mn_ref, packed_n2)
    else:
        mix = _halo_rows(mp_ref, mm_ref, mn_ref)
    x = _halo_rows(xp_ref, xm_ref, xn_ref)
    y = jnp.dot(mix, wmix_ref[...], preferred_element_type=F32) + bmix_ref[...]
    x1 = x + g1_ref[0] * y
    h = _rms_mod(x1, n2_ref[...], sh_ref[0], sc_ref[0]).astype(BF16)
    a = jnp.dot(h, wup_ref[:, :d_ff], preferred_element_type=F32)
    g = _zero_outside(jnp.dot(h, wup_ref[:, d_ff:], preferred_element_type=F32), 1, halo)
    g_prev, g_next = _shift_rows(g, rows)
    gc = (g_prev * cw_ref[0:1] + g * cw_ref[1:2] + g_next * cw_ref[2:3] + cb_ref[...])[halo:rows - halo]
    u = a[halo:rows - halo] * (0.5 * gc * (1.0 + lax.erf(gc * (2.0 ** -0.5))))
    ffn = jnp.dot(u.astype(BF16), wdn_ref[...], preferred_element_type=F32)
    o_ref[0] = x1[halo:rows - halo] + g2_ref[0] * ffn


def _halo_specs(tm, l, d, halo, nb=1):
    nblk = tm // halo
    last = l // halo - 1
    return [
        pl.BlockSpec((nb, halo, d), lambda bi, i: (bi, jnp.maximum(i * nblk - 1, 0), 0)),
        pl.BlockSpec((nb, tm, d), lambda bi, i: (bi, i, 0)),
        pl.BlockSpec((nb, halo, d), lambda bi, i: (bi, jnp.minimum((i + 1) * nblk, last), 0)),
    ]


def _packed_halo_specs(tm, l, d, n2):
    pitch = _pitch(n2)
    tp = tm // n2 * pitch
    last = l // n2 * pitch // 8 - 1
    return [
        pl.BlockSpec((1, 8, d), lambda bi, i: (bi // 2, jnp.maximum(i * (tp // 8) - 2, 0), 0)),
        pl.BlockSpec((1, tp, d), lambda bi, i: (bi // 2, i, 0)),
        pl.BlockSpec((1, 8, d), lambda bi, i: (bi // 2, jnp.minimum((i + 1) * (tp // 8), last), 0)),
    ]


def _post(mix, x, w_mix, b_mix, g1, norm2, sh2, sc2, g2, w_up, conv_w, conv_b, w_down, tm, packed_n2=0):
    b, l, d = x.shape
    d_ff = w_down.shape[0]
    mod = pl.BlockSpec((1, 1, d), lambda bi, i: (bi, 0, 0))
    halo = 8 if packed_n2 else HALO
    mix_specs = _packed_halo_specs(tm, l, d, packed_n2) if packed_n2 else _halo_specs(tm, l, d, halo)
    return pl.pallas_call(
        functools.partial(_post_kernel, packed_n2=packed_n2),
        out_shape=jax.ShapeDtypeStruct((b, l, d), F32),
        grid=(b, l // tm),
        in_specs=mix_specs + _halo_specs(tm, l, d, halo) + [
            _const_spec((d, d)), _const_spec((1, d)), mod, _const_spec((1, d)), mod, mod, mod,
            _const_spec((d, 2 * d_ff)), _const_spec((3, d_ff)), _const_spec((1, d_ff)), _const_spec((d_ff, d)),
        ],
        out_specs=pl.BlockSpec((1, tm, d), lambda bi, i: (bi, i, 0)),
        compiler_params=_params(("parallel", "parallel")),
        name="post_ffn",
    )(mix, mix, mix, x, x, x, w_mix.astype(BF16), b_mix.reshape(1, d), g1, norm2.reshape(1, d), sh2, sc2, g2,
      w_up.astype(BF16), conv_w, conv_b.reshape(1, d_ff), w_down.astype(BF16))


def _hy_in_kernel(xp_ref, xm_ref, xn_ref, sh_ref, sc_ref, g_ref, w_ref, b_ref, cw_ref, cb_ref, x0_ref, z_ref, *, n2):
    halo = xp_ref.shape[1]
    tm = xm_ref.shape[1]
    rows = tm + 2 * halo
    d = xm_ref.shape[-1]
    pitch = _pitch(n2)
    x0s, zs = [], []
    for s in range(2):
        h = _rms_mod(_halo_rows(xp_ref, xm_ref, xn_ref, s), g_ref[...], sh_ref[s], sc_ref[s]).astype(BF16)
        parts = []
        for j in range(3):
            cols = slice(j * d, (j + 1) * d)
            u = _zero_outside(jnp.dot(h, w_ref[:, cols], preferred_element_type=F32) + b_ref[:, cols], 1, halo)
            u_prev, u_next = _shift_rows(u, rows)
            cw = cw_ref[:, cols]
            uc = u_prev * cw[0:1] + u * cw[1:2] + u_next * cw[2:3] + cb_ref[:, cols]
            parts.append(uc[halo:rows - halo])
        x0s.append(parts[0])
        zs.append(parts[2] * parts[1])
    pad = jnp.zeros((pitch - n2, d), U32)
    for ref, pair in ((x0_ref, x0s), (z_ref, zs)):
        words = _pack_pair(pair[0], pair[1])
        for j in range(tm // n2):
            ref[0, j * pitch:j * pitch + n2] = words[j * n2:(j + 1) * n2]
            ref[0, j * pitch + n2:(j + 1) * pitch] = pad


def _hy_in(x, shift, scale, gain, w_in, b_in, short_w, short_b, tm, n2):
    b, l, d = x.shape
    pitch = _pitch(n2)
    tp = tm // n2 * pitch
    mod = pl.BlockSpec((2, 1, d), lambda pi, i: (pi, 0, 0))
    out = pl.BlockSpec((1, tp, d), lambda pi, i: (pi, i, 0))
    shape = jax.ShapeDtypeStruct((b // 2, l // n2 * pitch, d), U32)
    return pl.pallas_call(
        functools.partial(_hy_in_kernel, n2=n2),
        out_shape=(shape, shape),
        grid=(b // 2, l // tm),
        in_specs=_halo_specs(tm, l, d, 8, 2) + [
            mod, mod, _const_spec((1, d)), _const_spec((d, 3 * d)), _const_spec((1, 3 * d)),
            _const_spec((3, 3 * d)), _const_spec((1, 3 * d)),
        ],
        out_specs=(out, out),
        compiler_params=_params(("parallel", "parallel")),
        name="hyena_in",
    )(x, x, x, shift, scale, gain.reshape(1, d), w_in.astype(BF16), b_in.reshape(1, 3 * d),
      short_w, short_b.reshape(1, 3 * d))


def _filter_kernel(bands_ref, w1t_ref, w1c_ref, w1s_ref, b1_ref, w2_ref, b2_ref, w3_ref, b3_ref, wo_ref,
                   freq_ref, delta_ref, hf_ref, hb_ref, ss_ref, *, seq_len, n2):
    pitch = _pitch(n2)
    tl = hf_ref.shape[0] // pitch * n2
    d = hf_ref.shape[1]
    pos = (pl.program_id(0) * tl + lax.broadcasted_iota(jnp.int32, (tl, 1), 0)).astype(F32)
    t = pos * (1.0 / (seq_len - 1))
    ang = bands_ref[...] * ((2.0 * math.pi / seq_len) * pos)
    freq = freq_ref[...]
    dot = functools.partial(jnp.dot, preferred_element_type=F32, precision=HIGHEST)
    pre = t * w1t_ref[...] + dot(jnp.cos(ang), w1c_ref[...]) - dot(jnp.sin(ang), w1s_ref[...])
    hdn = jnp.sin(freq * (pre + b1_ref[...]))
    hdn = jnp.sin(freq * (dot(hdn, w2_ref[...]) + b2_ref[...]))
    hdn = jnp.sin(freq * (dot(hdn, w3_ref[...]) + b3_ref[...]))
    decay = jnp.exp(-t * delta_ref[...])
    hf = dot(hdn, wo_ref[:, :d]) * decay
    hb = dot(hdn, wo_ref[:, d:]) * decay
    pad = jnp.zeros((pitch - n2, d), F32)
    for ref, taps in ((hf_ref, hf), (hb_ref, jnp.where(pos > 0.0, hb, 0.0))):
        for j in range(tl // n2):
            ref[j * pitch:j * pitch + n2] = taps[j * n2:(j + 1) * n2]
            ref[j * pitch + n2:(j + 1) * pitch] = pad

    @pl.when(pl.program_id(0) == 0)
    def _():
        ss_ref[...] = jnp.zeros_like(ss_ref)

    ss_ref[...] += jnp.sum(hf * hf + hb * hb, axis=0, keepdims=True)


def _hyena_filter(seq_len, d, w1, b1, w2, b2, w3, b3, w_out, freq, tl, n2):
    tp = tl // n2 * _pitch(n2)
    padded = jax.ShapeDtypeStruct((seq_len // n2 * _pitch(n2), d), F32)
    width = w2.shape[0]
    bands = np.linspace(1e-4, HY_BANDS - 1, HY_BANDS, dtype=np.float32).reshape(1, HY_BANDS)
    deltas = np.abs(np.linspace(math.log(HY_DECAY_TARGET) / HY_SHORT_DECAY_PCT,
                                math.log(HY_DECAY_TARGET) / HY_LONG_DECAY_PCT, d, dtype=np.float32)).reshape(1, d)
    small = [
        (1, HY_BANDS), (1, width), (HY_BANDS, width), (HY_BANDS, width), (1, width), (width, width), (1, width),
        (width, width), (1, width), (width, 2 * d), (1, width), (1, d),
    ]
    return pl.pallas_call(
        functools.partial(_filter_kernel, seq_len=seq_len, n2=n2),
        out_shape=(padded, padded, jax.ShapeDtypeStruct((1, d), F32)),
        grid=(seq_len // tl,),
        in_specs=[pl.BlockSpec(s, lambda i: (0, 0)) for s in small],
        out_specs=(pl.BlockSpec((tp, d), lambda i: (i, 0)), pl.BlockSpec((tp, d), lambda i: (i, 0)),
                   pl.BlockSpec((1, d), lambda i: (0, 0))),
        compiler_params=_params(("arbitrary",)),
        name="hyena_filter",
    )(jnp.asarray(bands), w1[0:1], w1[1:1 + HY_BANDS], w1[1 + HY_BANDS:], b1.reshape(1, width), w2,
      b2.reshape(1, width), w3, b3.reshape(1, width), w_out, freq.reshape(1, width), jnp.asarray(deltas))


def _dft_tables(n1):
    n = n1 * n1
    half = n1 // 2
    idx = np.arange(n1)
    ang = 2.0 * np.pi * np.outer(idx, idx) / n1
    c, s = np.cos(ang), np.sin(ang)
    fa_pair = np.block([[c[:, :half], s[:, :half]], [-s[:, :half], c[:, :half]]])
    zero = np.zeros((n1, half))
    fa_real = np.block([[c[:, :half], zero], [-s[:, :half], zero], [zero, c[:, :half]], [zero, -s[:, :half]]])
    fa_inv = np.block([[c[:half], -s[:half]], [s[:half], c[:half]]]) / n
    k1 = idx[:, None, None]
    k2 = idx[None, :, None]
    n2 = idx[None, None, :]
    m = (n2 * (k1 + n1 * k2)) % n
    gang = 2.0 * np.pi * m / n
    gr, gi = np.cos(gang), -np.sin(gang)
    g_fwd = np.concatenate([np.concatenate([gr, -gi], axis=2), np.concatenate([gi, gr], axis=2)], axis=1)
    to = lambda a: jnp.asarray(a.astype(np.float32), BF16)
    return to(fa_pair), to(fa_real), to(fa_inv), to(g_fwd)


def _filter_spectrum_kernel(hf_ref, hb_ref, ss_ref, fa_ref, g_ref, o_ref, a_scr):
    n1 = fa_ref.shape[0] // 4
    n2 = g_ref.shape[1] // 2
    pitch = _pitch(n2)
    half = hf_ref.shape[0] // pitch
    kb = g_ref.shape[0]

    @pl.when(pl.program_id(1) == 0)
    def _():
        def body(j, carry):
            rows = jnp.concatenate([hf_ref[pl.ds(j, half, stride=pitch), :], hb_ref[pl.ds(j, half, stride=pitch), :]],
                                   axis=0)
            r = jnp.dot(fa_ref[...], rows.astype(BF16), preferred_element_type=F32)
            a_scr[0, pl.ds(j, n1, stride=pitch), :] = _pack_pair(r[:n1], r[n1:2 * n1])
            a_scr[1, pl.ds(j, n1, stride=pitch), :] = _pack_pair(r[2 * n1:3 * n1], r[3 * n1:])
            return carry

        lax.fori_loop(0, n2, body, 0, unroll=DFT_UNROLL)

    scale = lax.rsqrt(ss_ref[...] + NORM_EPS)
    for k in range(kb):
        row0 = pl.multiple_of((pl.program_id(1) * kb + k) * pitch, 8)
        fr, fi = _unpack_pair(a_scr[0, pl.ds(row0, n2), :])
        br, bi = _unpack_pair(a_scr[1, pl.ds(row0, n2), :])
        a = jnp.concatenate([jnp.concatenate([fr, fi], axis=0), jnp.concatenate([br, bi], axis=0)], axis=1)
        p = jnp.dot(g_ref[k], a.astype(BF16), preferred_element_type=F32)
        lanes = p.shape[1] // 2
        o_ref[k] = _pack_pair((p[:n2, :lanes] + p[:n2, lanes:]) * scale, (p[n2:, :lanes] - p[n2:, lanes:]) * scale)


def _filter_spectrum(fa_filt, g_fwd, hf, hb, energy, kb):
    n1 = g_fwd.shape[0]
    n2 = g_fwd.shape[1] // 2
    lp, d = hf.shape
    tc = LANES
    half_spec = pl.BlockSpec((lp, tc), lambda c, k: (0, c))
    return pl.pallas_call(
        _filter_spectrum_kernel,
        out_shape=jax.ShapeDtypeStruct((n1, n2, d), U32),
        grid=(d // tc, n1 // kb),
        in_specs=[
            half_spec, half_spec, pl.BlockSpec((1, tc), lambda c, k: (0, c)),
            _const_spec(fa_filt.shape),
            pl.BlockSpec((kb, 2 * n2, 2 * n2), lambda c, k: (k, 0, 0)),
        ],
        out_specs=pl.BlockSpec((kb, n2, tc), lambda c, k: (k, 0, c)),
        scratch_shapes=[pltpu.VMEM((2, n1 * _pitch(n2), tc), U32)],
        compiler_params=_params(("parallel", "arbitrary")),
        name="filter_spectrum",
    )(hf, hb, energy, fa_filt, g_fwd)


def _long_conv_kernel(z_ref, x0_ref, hf_ref, gf_ref, fa_ref, fi_ref, db_ref, o_ref, a_scr):
    n1 = fa_ref.shape[0] // 2
    n2 = gf_ref.shape[1] // 2
    half = n1 // 2
    pitch = _pitch(n2)
    kb = gf_ref.shape[0]
    step = pl.program_id(2)

    @pl.when(step == 0)
    def _():
        def body(j, carry):
            zr, zi = _unpack_pair(z_ref[0, pl.ds(j, half, stride=pitch), :])
            rows = jnp.concatenate([zr, zi], axis=0).astype(BF16)
            r = jnp.dot(fa_ref[...], rows, preferred_element_type=F32)
            a_scr[pl.ds(j, n1, stride=pitch), :] = _pack_pair(r[:n1], r[n1:])
            return carry

        lax.fori_loop(0, n2, body, 0, unroll=DFT_UNROLL)

    for k in range(kb):
        row0 = pl.multiple_of((step * kb + k) * pitch, 8)
        ar, ai = _unpack_pair(a_scr[pl.ds(row0, n2), :])
        x = jnp.dot(gf_ref[k], jnp.concatenate([ar, ai], axis=0).astype(BF16), preferred_element_type=F32)
        xr, xi = x[:n2], x[n2:]
        hr, hi = _unpack_pair(hf_ref[k])
        y = jnp.concatenate([xr * hr - xi * hi, xr * hi + xi * hr], axis=0).astype(BF16)
        t = lax.dot_general(gf_ref[k], y, (((0,), (0,)), ((), ())), preferred_element_type=F32)
        a_scr[pl.ds(row0, n2), :] = _pack_pair(t[:n2], t[n2:])

    @pl.when(step == pl.num_programs(2) - 1)
    def _():
        def body(j, carry):
            tr, ti = _unpack_pair(a_scr[pl.ds(j, n1, stride=pitch), :])
            y = jnp.dot(fi_ref[...], jnp.concatenate([tr, ti], axis=0).astype(BF16), preferred_element_type=F32)
            z0, z1 = _unpack_pair(z_ref[0, pl.ds(j, half, stride=pitch), :])
            g0, g1 = _unpack_pair(x0_ref[0, pl.ds(j, half, stride=pitch), :])
            db = db_ref[...]
            o_ref[0, pl.ds(j, half, stride=pitch), :] = _pack_pair(g0 * (y[:half] + z0 * db), g1 * (y[half:] + z1 * db))
            return carry

        lax.fori_loop(0, n2, body, 0, unroll=DFT_UNROLL_OUT)
        pad = jnp.zeros((pitch - n2, o_ref.shape[2]), U32)
        for s in range(half):
            o_ref[0, s * pitch + n2:(s + 1) * pitch] = pad


def _long_conv_gate(x0p, zp, hf, hb, energy, d_bias, l):
    p, lp, d = zp.shape
    n1 = math.isqrt(2 * l)
    assert n1 * n1 == 2 * l and lp == l // n1 * _pitch(n1)
    kb = min(32, n1)
    tc = LANES
    fa_pair, fa_real, fa_inv, g_fwd = _dft_tables(n1)
    spec = _filter_spectrum(fa_real, g_fwd, hf, hb, energy, kb)
    seq = pl.BlockSpec((1, lp, tc), lambda c, pi, k: (pi, 0, c))
    g_blk = pl.BlockSpec((kb, 2 * n1, 2 * n1), lambda c, pi, k: (k, 0, 0))
    return pl.pallas_call(
        _long_conv_kernel,
        out_shape=jax.ShapeDtypeStruct((p, lp, d), U32),
        grid=(d // tc, p, n1 // kb),
        in_specs=[
            seq, seq, pl.BlockSpec((kb, n1, tc), lambda c, pi, k: (k, 0, c)), g_blk,
            _const_spec(fa_pair.shape), _const_spec(fa_inv.shape), pl.BlockSpec((1, tc), lambda c, pi, k: (0, c)),
        ],
        out_specs=seq,
        scratch_shapes=[pltpu.VMEM((n1 * _pitch(n1), tc), U32)],
        compiler_params=_params(("parallel", "parallel", "arbitrary")),
        name="long_conv",
    )(zp, x0p, spec, g_fwd, fa_pair, fa_inv, d_bias.reshape(1, d))


def _mod_rows(mod, lo, hi, d):
    m = mod[lo:hi]
    return [m[:, None, j * d:(j + 1) * d] for j in range(N_MOD)]


def kernel(x, c, ctx, c_ctx, l0_w_mod, l0_b_mod, l0_norm1, l0_norm2, l0_na_w_qkv, l0_na_q_gain, l0_na_k_gain, l0_na_rpb, l0_na_w_o, l0_ffn_w_up, l0_ffn_conv_w, l0_ffn_conv_b, l0_ffn_w_down, l1_w_mod, l1_b_mod, l1_norm1, l1_norm2, l1_hy_w_in, l1_hy_b_in, l1_hy_short_w, l1_hy_short_b, l1_hy_f_w1, l1_hy_f_b1, l1_hy_f_w2, l1_hy_f_b2, l1_hy_f_w3, l1_hy_f_b3, l1_hy_f_wout, l1_hy_f_freq, l1_hy_d_bias, l1_hy_w_out, l1_hy_b_out, l1_ffn_w_up, l1_ffn_conv_w, l1_ffn_conv_b, l1_ffn_w_down):
    b, l, d = x.shape
    n_ctx = ctx.shape[1]
    n_rows = l // GRID_W
    assert n_rows >= K_ROWS and n_rows % (Q_ROWS * NA_SUB) == 0 and b % 2 == 0
    tm = min(512, l)
    tm_qkv = min(1024, l)
    n_dft = math.isqrt(2 * l)

    cond = jnp.zeros((8, d), F32).at[:b].set(c).at[b].set(c_ctx)

    mod = _adaln(cond, l0_w_mod, l0_b_mod)
    sh1, sc1, g1, sh2, sc2, g2 = _mod_rows(mod, 0, b, d)
    csh1, csc1 = _mod_rows(mod, b, b + 1, d)[:2]
    qkv = _qkv(x, sh1, sc1, l0_norm1, l0_na_w_qkv, l0_na_q_gain, l0_na_k_gain, tm_qkv)
    qkv_ctx = _qkv(ctx, csh1, csc1, l0_norm1, l0_na_w_qkv, l0_na_q_gain, l0_na_k_gain, n_ctx)
    bias = _block_bias(_col_bias(l0_na_rpb), n_rows)
    attn = _neighbourhood_attention(qkv, qkv_ctx, bias)
    x = _post(attn, x, l0_na_w_o, jnp.zeros((d,), F32), g1, l0_norm2, sh2, sc2, g2,
              l0_ffn_w_up, l0_ffn_conv_w, l0_ffn_conv_b, l0_ffn_w_down, tm)

    mod = _adaln(cond, l1_w_mod, l1_b_mod)
    sh1, sc1, g1, sh2, sc2, g2 = _mod_rows(mod, 0, b, d)
    x0p, zp = _hy_in(x, sh1, sc1, l1_norm1, l1_hy_w_in, l1_hy_b_in, l1_hy_short_w, l1_hy_short_b, tm, n_dft)
    hf, hb, energy = _hyena_filter(l, d, l1_hy_f_w1, l1_hy_f_b1, l1_hy_f_w2, l1_hy_f_b2, l1_hy_f_w3, l1_hy_f_b3,
                                   l1_hy_f_wout, l1_hy_f_freq, min(1024, l), n_dft)
    gated = _long_conv_gate(x0p, zp, hf, hb, energy, l1_hy_d_bias, l)
    x = _post(gated, x, l1_hy_w_out, l1_hy_b_out, g1, l1_norm2, sh2, sc2, g2,
              l1_ffn_w_up, l1_ffn_conv_w, l1_ffn_conv_b, l1_ffn_w_down, tm, packed_n2=n_dft)
    return x
```

```python
import functools
import math

import numpy as np
import jax
import jax.numpy as jnp
from jax import lax
from jax.experimental import pallas as pl
from jax.experimental.pallas import tpu as pltpu

F32 = jnp.float32
BF16 = jnp.bfloat16
U32 = jnp.uint32
HIGHEST = lax.Precision.HIGHEST

NORM_EPS = 1e-6
N_MOD = 6
HEAD_DIM = 64
GRID_W = 64
WIN_H = 8
WIN_W = 16
HY_BANDS = 16
HY_DECAY_TARGET = 1e-2
HY_SHORT_DECAY_PCT = 0.3
HY_LONG_DECAY_PCT = 1.5

LANES = 128
HALO = 16
Q_ROWS = 4
K_ROWS = Q_ROWS + WIN_H
NA_SUB = 4
NA_CHUNK = 32
DFT_UNROLL = 16
DFT_UNROLL_OUT = 8
LOG2E = math.log2(math.e)
NEG = -1e30
VMEM_LIMIT = 56 * 1024 * 1024


def _params(sem):
    return pltpu.CompilerParams(dimension_semantics=sem, vmem_limit_bytes=VMEM_LIMIT)


def _const_spec(shape):
    return pl.BlockSpec(shape, lambda *_: (0,) * len(shape), pipeline_mode=pl.Buffered(1))


def _rms_mod(x, gain, shift, scale):
    ms = jnp.mean(x * x, axis=-1, keepdims=True)
    y = x * lax.rsqrt(ms + NORM_EPS) * gain
    return y * (1.0 + scale) + shift


def _split_bf16(x):
    hi = x.astype(BF16)
    return hi, (x - hi.astype(F32)).astype(BF16)


def _dot3(x, w):
    x_hi, x_lo = _split_bf16(x)
    w_hi, w_lo = _split_bf16(w)
    return jnp.dot(jnp.concatenate([x_hi, x_lo, x_hi], axis=1), jnp.concatenate([w_hi, w_hi, w_lo], axis=0),
                   preferred_element_type=F32)


def _adaln_kernel(c_ref, w_ref, b_ref, o_ref):
    c = c_ref[...]
    s = c / (1.0 + jnp.exp(-c))
    o_ref[...] = _dot3(s, w_ref[...]) + b_ref[...]


def _adaln(cond, w_mod, b_mod):
    rows, d = cond.shape
    n = w_mod.shape[1]
    tn = d
    return pl.pallas_call(
        _adaln_kernel,
        out_shape=jax.ShapeDtypeStruct((rows, n), F32),
        grid=(n // tn,),
        in_specs=[
            pl.BlockSpec((rows, d), lambda j: (0, 0)),
            pl.BlockSpec((d, tn), lambda j: (0, j)),
            pl.BlockSpec((1, tn), lambda j: (0, j)),
        ],
        out_specs=pl.BlockSpec((rows, tn), lambda j: (0, j)),
        compiler_params=_params(("arbitrary",)),
        name="adaln",
    )(cond, w_mod, b_mod.reshape(1, n))


def _qkv_kernel(x_ref, sh_ref, sc_ref, g_ref, w_ref, qg_ref, kg_ref, p1_ref, p2_ref, o_ref):
    d = x_ref.shape[-1]
    h = _rms_mod(x_ref[0], g_ref[...], sh_ref[0], sc_ref[0]).astype(BF16)
    for j, gain_ref in ((0, qg_ref), (1, kg_ref)):
        t = jnp.dot(h, w_ref[:, j * d:(j + 1) * d], preferred_element_type=F32)
        ms = jnp.dot((t * t).astype(BF16), p1_ref[...], preferred_element_type=F32)
        r = lax.rsqrt(ms + NORM_EPS)
        r_hi = r.astype(BF16)
        r_lo = (r - r_hi.astype(F32)).astype(BF16)
        rr = jnp.dot(jnp.concatenate([r_hi, r_lo], axis=-1), p2_ref[...], preferred_element_type=F32)
        o_ref[0, :, j * d:(j + 1) * d] = (t * rr * gain_ref[...]).astype(BF16)
    v = jnp.dot(h, w_ref[:, 2 * d:3 * d], preferred_element_type=F32)
    o_ref[0, :, 2 * d:3 * d] = v.astype(BF16)


def _qkv(x, shift, scale, gain, w_qkv, q_gain, k_gain, tm):
    b, l, d = x.shape
    heads = d // HEAD_DIM
    per_batch = shift.shape[0] == b
    mod_map = (lambda bi, i: (bi, 0, 0)) if per_batch else (lambda bi, i: (0, 0, 0))
    p1 = np.zeros((d, LANES), np.float32)
    p1[np.arange(d), np.arange(d) // HEAD_DIM] = 1.0 / HEAD_DIM
    p2 = np.zeros((2 * LANES, d), np.float32)
    p2[np.arange(d) // HEAD_DIM, np.arange(d)] = 1.0
    p2[LANES + np.arange(d) // HEAD_DIM, np.arange(d)] = 1.0
    qg = (jnp.tile(q_gain, heads) * (HEAD_DIM ** -0.5 * LOG2E)).reshape(1, d)
    kg = jnp.tile(k_gain, heads).reshape(1, d)
    return pl.pallas_call(
        _qkv_kernel,
        out_shape=jax.ShapeDtypeStruct((b, l, 3 * d), BF16),
        grid=(b, l // tm),
        in_specs=[
            pl.BlockSpec((1, tm, d), lambda bi, i: (bi, i, 0)),
            pl.BlockSpec((1, 1, d), mod_map),
            pl.BlockSpec((1, 1, d), mod_map),
            _const_spec((1, d)),
            _const_spec((d, 3 * d)),
            _const_spec((1, d)),
            _const_spec((1, d)),
            _const_spec((d, LANES)),
            _const_spec((2 * LANES, d)),
        ],
        out_specs=pl.BlockSpec((1, tm, 3 * d), lambda bi, i: (bi, i, 0)),
        compiler_params=_params(("parallel", "parallel")),
        name="qkv",
    )(x, shift, scale, gain.reshape(1, d), w_qkv.astype(BF16), qg, kg,
      jnp.asarray(p1, BF16), jnp.asarray(p2, BF16))


def _toeplitz_kernel(r_ref, oh_ref, m_ref, o_ref):
    o_ref[...] = jnp.dot(r_ref[...], oh_ref[...], preferred_element_type=F32, precision=HIGHEST) + m_ref[...]


def _col_bias(rpb):
    heads, n_dr, n_dc = rpb.shape
    k_pad = 32
    qc = np.arange(GRID_W)[:, None]
    kc = np.arange(GRID_W)[None, :]
    c0 = np.clip(qc - WIN_W // 2, 0, GRID_W - WIN_W)
    valid = (kc >= c0) & (kc < c0 + WIN_W)
    dc = kc - qc + (WIN_W - 1)
    onehot = np.zeros((k_pad, GRID_W, GRID_W), np.float32)
    for j in range(n_dc):
        onehot[j] = ((dc == j) & valid).astype(np.float32)
    onehot = onehot.reshape(k_pad, GRID_W * GRID_W)
    mask = np.where(valid, 0.0, NEG).astype(np.float32).reshape(1, GRID_W * GRID_W)
    rows = heads * n_dr
    rows_pad = -(-rows // 8) * 8
    r2 = jnp.zeros((rows_pad, k_pad), F32).at[:rows, :n_dc].set(rpb.reshape(rows, n_dc))
    t = pl.pallas_call(
        _toeplitz_kernel,
        out_shape=jax.ShapeDtypeStruct((rows_pad, GRID_W * GRID_W), F32),
        name="rpb_toeplitz",
    )(r2, jnp.asarray(onehot), jnp.asarray(mask))
    return t[:rows].reshape(heads, n_dr, GRID_W, GRID_W)


def _block_bias(t, n_rows):
    heads = t.shape[0]
    kh = min(WIN_H, n_rows)
    masked = jnp.full((heads, GRID_W, GRID_W), NEG, F32)
    classes = []
    for q0, k0 in ((0, 0), (Q_ROWS, Q_ROWS - WIN_H // 2), (n_rows - Q_ROWS, n_rows - K_ROWS)):
        strips = []
        for qr in range(Q_ROWS):
            r = q0 + qr
            r0 = min(max(r - kh // 2, 0), n_rows - kh)
            blocks = []
            for kr in range(K_ROWS):
                kk = k0 + kr
                blocks.append(t[:, kk - r + WIN_H - 1] if r0 <= kk < r0 + kh else masked)
            strips.append(jnp.concatenate(blocks, axis=-1))
        classes.append(jnp.concatenate(strips, axis=-2))
    return (jnp.stack(classes) * LOG2E).astype(BF16)


def _na_stages(q_ref, k_ref, v_ref, kc_ref, vc_ref, bias_ref, o_ref, k_scr, v_scr, p_scr, s_new, s_old,
               *, sub_new, sub_old, n_sub):
    tq = q_ref.shape[1] // NA_SUB
    tk = bias_ref.shape[-1]
    n_ctx = kc_ref.shape[1]
    l = k_ref.shape[1]
    first_head = lax.broadcasted_iota(jnp.int32, (1, LANES), 1) < HEAD_DIM

    def window_start(sub):
        return pl.multiple_of(jnp.clip(sub * tq - (tk - tq) // 2, 0, l - tk), tq)

    for j in range(NA_SUB):
        k_scr[j, :tk] = k_ref[0, pl.ds(window_start(sub_new + j), tk), :]
        k_scr[j, tk:] = kc_ref[0]
        q = q_ref[0, j * tq:(j + 1) * tq]
        zero = jnp.zeros_like(q)
        q2 = jnp.concatenate([jnp.where(first_head, q, zero), jnp.where(first_head, zero, q)], axis=0)
        s_new[j] = lax.dot_general(q2, k_scr[j], (((1,), (1,)), ((), ())), preferred_element_type=F32)

    for j in range(NA_SUB):
        sub = sub_old + j
        v_scr[j, :tk, :LANES] = v_ref[0, pl.ds(window_start(sub), tk), :]
        v_scr[j, tk:, :LANES] = vc_ref[0]
        v_scr[j, :, LANES:] = jnp.ones((tk + n_ctx, LANES), BF16)
        edge = jnp.where(sub == 0, 0, jnp.where(sub == n_sub - 1, 2, 1))
        heads = []
        for h in range(2):
            for r in range(0, tq, NA_CHUNK):
                rows = slice(h * tq + r, h * tq + r + NA_CHUNK)
                s_loc = s_old[j, rows, :tk] + bias_ref[edge, h, r:r + NA_CHUNK, :].astype(F32)
                s_ctx = s_old[j, rows, tk:]
                m = jnp.maximum(jnp.max(s_loc, axis=-1, keepdims=True), jnp.max(s_ctx, axis=-1, keepdims=True))
                p_scr[j, rows, :tk] = jnp.exp2((s_loc - m).astype(BF16))
                p_scr[j, rows, tk:] = jnp.exp2((s_ctx - m).astype(BF16))
            o = jnp.dot(p_scr[j, h * tq:(h + 1) * tq], v_scr[j], preferred_element_type=F32)
            heads.append(o[:, :LANES] / o[:, LANES:])
        o_ref[0, j * tq:(j + 1) * tq] = jnp.where(first_head, heads[0], heads[1]).astype(BF16)


def _na_kernel(q_ref, k_ref, v_ref, kc_ref, vc_ref, bias_ref, o_ref, k_scr, v_scr, p_scr, s0, s1, *, nb):
    t = pl.program_id(0)
    last = pl.num_programs(0) - 2
    subs = dict(sub_new=(jnp.minimum(t, last) % nb) * NA_SUB, sub_old=(jnp.clip(t - 1, 0, last) % nb) * NA_SUB,
                n_sub=nb * NA_SUB)
    refs = (q_ref, k_ref, v_ref, kc_ref, vc_ref, bias_ref, o_ref, k_scr, v_scr, p_scr)

    @pl.when(t == 0)
    def _():
        s1[...] = jnp.zeros(s1.shape, F32)

    @pl.when(t % 2 == 0)
    def _():
        _na_stages(*refs, s0, s1, **subs)

    @pl.when(t % 2 == 1)
    def _():
        _na_stages(*refs, s1, s0, **subs)


def _neighbourhood_attention(qkv, qkv_ctx, bias):
    b, l, d3 = qkv.shape
    d = d3 // 3
    n_ctx = qkv_ctx.shape[1]
    pairs = d // LANES
    tq, tk = bias.shape[-2:]
    tb = NA_SUB * tq
    nb = l // tb
    steps = b * pairs * nb

    def decode(t):
        return t // (pairs * nb), (t // nb) % pairs, t % nb

    def lagged(t, lag):
        return decode(jnp.clip(t - lag, 0, steps - 1))

    def q_map(t):
        bi, hp, i = lagged(t, 0)
        return bi, i, hp

    def kv_map(which, lag):
        def index(t):
            bi, hp, _ = lagged(t, lag)
            return bi, 0, which * pairs + hp
        return index

    def bias_map(t):
        return 0, lagged(t, 1)[1], 0, 0

    def out_map(t):
        bi, hp, i = lagged(t, 1)
        return bi, i, hp

    n_keys = tk + n_ctx
    return pl.pallas_call(
        functools.partial(_na_kernel, nb=nb),
        out_shape=jax.ShapeDtypeStruct((b, l, d), BF16),
        grid=(steps + 1,),
        in_specs=[
            pl.BlockSpec((1, tb, LANES), q_map),
            pl.BlockSpec((1, l, LANES), kv_map(1, 0)),
            pl.BlockSpec((1, l, LANES), kv_map(2, 1)),
            pl.BlockSpec((1, n_ctx, LANES), kv_map(1, 0)),
            pl.BlockSpec((1, n_ctx, LANES), kv_map(2, 1)),
            pl.BlockSpec((3, 2, tq, tk), bias_map),
        ],
        out_specs=pl.BlockSpec((1, tb, LANES), out_map),
        scratch_shapes=[
            pltpu.VMEM((NA_SUB, n_keys, LANES), BF16),
            pltpu.VMEM((NA_SUB, n_keys, 2 * LANES), BF16),
            pltpu.VMEM((NA_SUB, 2 * tq, n_keys), BF16),
            pltpu.VMEM((NA_SUB, 2 * tq, n_keys), F32), pltpu.VMEM((NA_SUB, 2 * tq, n_keys), F32),
        ],
        compiler_params=_params(("arbitrary",)),
        name="na_attention",
    )(qkv, qkv, qkv, qkv_ctx, qkv_ctx, bias)


def _halo_rows(prev_ref, main_ref, next_ref, s=0):
    return jnp.concatenate([prev_ref[s], main_ref[s], next_ref[s]], axis=0)


def _zero_outside(u, axis, halo):
    i = pl.program_id(axis)
    rows = u.shape[0]
    keep_top = jnp.where(i == 0, 0.0, 1.0).astype(F32)
    keep_bottom = jnp.where(i == pl.num_programs(axis) - 1, 0.0, 1.0).astype(F32)
    return jnp.concatenate([u[:halo] * keep_top, u[halo:rows - halo], u[rows - halo:] * keep_bottom], axis=0)


def _shift_rows(g, rows):
    return pltpu.roll(g, 1, 0), pltpu.roll(g, rows - 1, 0)


def _pack_pair(hi, lo):
    hi_bits = pltpu.bitcast(hi.astype(BF16).astype(F32), U32)
    lo_bits = pltpu.bitcast(lo.astype(BF16).astype(F32), U32)
    return hi_bits | (lo_bits >> 16)


def _unpack_pair(word):
    return (pltpu.bitcast(word & jnp.uint32(0xFFFF0000), F32), pltpu.bitcast(word << 16, F32))


def _pitch(n2):
    return n2 + 8


def _packed_mix_rows(prev_ref, main_ref, next_ref, n2):
    pitch = _pitch(n2)
    slabs = main_ref.shape[1] // pitch
    words = jnp.concatenate([prev_ref[0]] + [main_ref[0, j * pitch:j * pitch + n2] for j in range(slabs)]
                            + [next_ref[0]], axis=0)
    shift = ((pl.program_id(0) % 2) * 16).astype(U32)
    return pltpu.bitcast((words << shift) & jnp.uint32(0xFFFF0000), F32).astype(BF16)


def _post_kernel(mp_ref, mm_ref, mn_ref, xp_ref, xm_ref, xn_ref, wmix_ref, bmix_ref, g1_ref, n2_ref,
                 sh_ref, sc_ref, g2_ref, wup_ref, cw_ref, cb_ref, wdn_ref, o_ref, *, packed_n2):
    halo = xp_ref.shape[1]
    rows = xm_ref.shape[1] + 2 * halo
    d_ff = wdn_ref.shape[0]
    if packed_n2:
        mix = _packed_mix_rows(mp_ref, mm_ref, mn_ref, packed_n2)
    else:
        mix = _halo_rows(mp_ref, mm_ref, mn_ref)
    x = _halo_rows(xp_ref, xm_ref, xn_ref)
    y = jnp.dot(mix, wmix_ref[...], preferred_element_type=F32) + bmix_ref[...]
    x1 = x + g1_ref[0] * y
    h = _rms_mod(x1, n2_ref[...], sh_ref[0], sc_ref[0]).astype(BF16)
    a = jnp.dot(h, wup_ref[:, :d_ff], preferred_element_type=F32)
    g = _zero_outside(jnp.dot(h, wup_ref[:, d_ff:], preferred_element_type=F32), 1, halo)
    g_prev, g_next = _shift_rows(g, rows)
    gc = (g_prev * cw_ref[0:1] + g * cw_ref[1:2] + g_next * cw_ref[2:3] + cb_ref[...])[halo:rows - halo]
    u = a[halo:rows - halo] * (0.5 * gc * (1.0 + lax.erf(gc * (2.0 ** -0.5))))
    ffn = jnp.dot(u.astype(BF16), wdn_ref[...], preferred_element_type=F32)
    o_ref[0] = x1[halo:rows - halo] + g2_ref[0] * ffn


def _halo_specs(tm, l, d, halo, nb=1):
    nblk = tm // halo
    last = l // halo - 1
    return [
        pl.BlockSpec((nb, halo, d), lambda bi, i: (bi, jnp.maximum(i * nblk - 1, 0), 0)),
        pl.BlockSpec((nb, tm, d), lambda bi, i: (bi, i, 0)),
        pl.BlockSpec((nb, halo, d), lambda bi, i: (bi, jnp.minimum((i + 1) * nblk, last), 0)),
    ]


def _packed_halo_specs(tm, l, d, n2):
    pitch = _pitch(n2)
    tp = tm // n2 * pitch
    last = l // n2 * pitch // 8 - 1
    return [
        pl.BlockSpec((1, 8, d), lambda bi, i: (bi // 2, jnp.maximum(i * (tp // 8) - 2, 0), 0)),
        pl.BlockSpec((1, tp, d), lambda bi, i: (bi // 2, i, 0)),
        pl.BlockSpec((1, 8, d), lambda bi, i: (bi // 2, jnp.minimum((i + 1) * (tp // 8), last), 0)),
    ]


def _post(mix, x, w_mix, b_mix, g1, norm2, sh2, sc2, g2, w_up, conv_w, conv_b, w_down, tm, packed_n2=0):
    b, l, d = x.shape
    d_ff = w_down.shape[0]
    mod = pl.BlockSpec((1, 1, d), lambda bi, i: (bi, 0, 0))
    halo = 8 if packed_n2 else HALO
    mix_specs = _packed_halo_specs(tm, l, d, packed_n2) if packed_n2 else _halo_specs(tm, l, d, halo)
    return pl.pallas_call(
        functools.partial(_post_kernel, packed_n2=packed_n2),
        out_shape=jax.ShapeDtypeStruct((b, l, d), F32),
        grid=(b, l // tm),
        in_specs=mix_specs + _halo_specs(tm, l, d, halo) + [
            _const_spec((d, d)), _const_spec((1, d)), mod, _const_spec((1, d)), mod, mod, mod,
            _const_spec((d, 2 * d_ff)), _const_spec((3, d_ff)), _const_spec((1, d_ff)), _const_spec((d_ff, d)),
        ],
        out_specs=pl.BlockSpec((1, tm, d), lambda bi, i: (bi, i, 0)),
        compiler_params=_params(("parallel", "parallel")),
        name="post_ffn",
    )(mix, mix, mix, x, x, x, w_mix.astype(BF16), b_mix.reshape(1, d), g1, norm2.reshape(1, d), sh2, sc2, g2,
      w_up.astype(BF16), conv_w, conv_b.reshape(1, d_ff), w_down.astype(BF16))


def _hy_in_kernel(xp_ref, xm_ref, xn_ref, sh_ref, sc_ref, g_ref, w_ref, b_ref, cw_ref, cb_ref, x0_ref, z_ref, *, n2):
    halo = xp_ref.shape[1]
    tm = xm_ref.shape[1]
    rows = tm + 2 * halo
    d = xm_ref.shape[-1]
    pitch = _pitch(n2)
    x0s, zs = [], []
    for s in range(2):
        h = _rms_mod(_halo_rows(xp_ref, xm_ref, xn_ref, s), g_ref[...], sh_ref[s], sc_ref[s]).astype(BF16)
        parts = []
        for j in range(3):
            cols = slice(j * d, (j + 1) * d)
            u = _zero_outside(jnp.dot(h, w_ref[:, cols], preferred_element_type=F32) + b_ref[:, cols], 1, halo)
            u_prev, u_next = _shift_rows(u, rows)
            cw = cw_ref[:, cols]
            uc = u_prev * cw[0:1] + u * cw[1:2] + u_next * cw[2:3] + cb_ref[:, cols]
            parts.append(uc[halo:rows - halo])
        x0s.append(parts[0])
        zs.append(parts[2] * parts[1])
    pad = jnp.zeros((pitch - n2, d), U32)
    for ref, pair in ((x0_ref, x0s), (z_ref, zs)):
        words = _pack_pair(pair[0], pair[1])
        for j in range(tm // n2):
            ref[0, j * pitch:j * pitch + n2] = words[j * n2:(j + 1) * n2]
            ref[0, j * pitch + n2:(j + 1) * pitch] = pad


def _hy_in(x, shift, scale, gain, w_in, b_in, short_w, short_b, tm, n2):
    b, l, d = x.shape
    pitch = _pitch(n2)
    tp = tm // n2 * pitch
    mod = pl.BlockSpec((2, 1, d), lambda pi, i: (pi, 0, 0))
    out = pl.BlockSpec((1, tp, d), lambda pi, i: (pi, i, 0))
    shape = jax.ShapeDtypeStruct((b // 2, l // n2 * pitch, d), U32)
    return pl.pallas_call(
        functools.partial(_hy_in_kernel, n2=n2),
        out_shape=(shape, shape),
        grid=(b // 2, l // tm),
        in_specs=_halo_specs(tm, l, d, 8, 2) + [
            mod, mod, _const_spec((1, d)), _const_spec((d, 3 * d)), _const_spec((1, 3 * d)),
            _const_spec((3, 3 * d)), _const_spec((1, 3 * d)),
        ],
        out_specs=(out, out),
        compiler_params=_params(("parallel", "parallel")),
        name="hyena_in",
    )(x, x, x, shift, scale, gain.reshape(1, d), w_in.astype(BF16), b_in.reshape(1, 3 * d),
      short_w, short_b.reshape(1, 3 * d))


def _filter_kernel(bands_ref, w1t_ref, w1c_ref, w1s_ref, b1_ref, w2_ref, b2_ref, w3_ref, b3_ref, wo_ref,
                   freq_ref, delta_ref, hf_ref, hb_ref, ss_ref, *, seq_len, n2):
    pitch = _pitch(n2)
    tl = hf_ref.shape[0] // pitch * n2
    d = hf_ref.shape[1]
    pos = (pl.program_id(0) * tl + lax.broadcasted_iota(jnp.int32, (tl, 1), 0)).astype(F32)
    t = pos * (1.0 / (seq_len - 1))
    ang = bands_ref[...] * ((2.0 * math.pi / seq_len) * pos)
    freq = freq_ref[...]
    pre = t * w1t_ref[...] + _dot3(jnp.cos(ang), w1c_ref[...]) - _dot3(jnp.sin(ang), w1s_ref[...])
    hdn = jnp.sin(freq * (pre + b1_ref[...]))
    hdn = jnp.sin(freq * (_dot3(hdn, w2_ref[...]) + b2_ref[...]))
    hdn = jnp.sin(freq * (_dot3(hdn, w3_ref[...]) + b3_ref[...]))
    decay = jnp.exp(-t * delta_ref[...])
    h = _dot3(hdn, wo_ref[...])
    hf = h[:, :d] * decay
    hb = h[:, d:] * decay
    pad = jnp.zeros((pitch - n2, d), F32)
    for ref, taps in ((hf_ref, hf), (hb_ref, jnp.where(pos > 0.0, hb, 0.0))):
        for j in range(tl // n2):
            ref[j * pitch:j * pitch + n2] = taps[j * n2:(j + 1) * n2]
            ref[j * pitch + n2:(j + 1) * pitch] = pad

    @pl.when(pl.program_id(0) == 0)
    def _():
        ss_ref[...] = jnp.zeros_like(ss_ref)

    ss_ref[...] += jnp.sum(hf * hf + hb * hb, axis=0, keepdims=True)


def _hyena_filter(seq_len, d, w1, b1, w2, b2, w3, b3, w_out, freq, tl, n2):
    tp = tl // n2 * _pitch(n2)
    padded = jax.ShapeDtypeStruct((seq_len // n2 * _pitch(n2), d), F32)
    width = w2.shape[0]
    bands = np.linspace(1e-4, HY_BANDS - 1, HY_BANDS, dtype=np.float32).reshape(1, HY_BANDS)
    deltas = np.abs(np.linspace(math.log(HY_DECAY_TARGET) / HY_SHORT_DECAY_PCT,
                                math.log(HY_DECAY_TARGET) / HY_LONG_DECAY_PCT, d, dtype=np.float32)).reshape(1, d)
    small = [
        (1, HY_BANDS), (1, width), (HY_BANDS, width), (HY_BANDS, width), (1, width), (width, width), (1, width),
        (width, width), (1, width), (width, 2 * d), (1, width), (1, d),
    ]
    return pl.pallas_call(
        functools.partial(_filter_kernel, seq_len=seq_len, n2=n2),
        out_shape=(padded, padded, jax.ShapeDtypeStruct((1, d), F32)),
        grid=(seq_len // tl,),
        in_specs=[pl.BlockSpec(s, lambda i: (0, 0)) for s in small],
        out_specs=(pl.BlockSpec((tp, d), lambda i: (i, 0)), pl.BlockSpec((tp, d), lambda i: (i, 0)),
                   pl.BlockSpec((1, d), lambda i: (0, 0))),
        compiler_params=_params(("arbitrary",)),
        name="hyena_filter",
    )(jnp.asarray(bands), w1[0:1], w1[1:1 + HY_BANDS], w1[1 + HY_BANDS:], b1.reshape(1, width), w2,
      b2.reshape(1, width), w3, b3.reshape(1, width), w_out, freq.reshape(1, width), jnp.asarray(deltas))


def _dft_tables(n1):
    n = n1 * n1
    half = n1 // 2
    idx = np.arange(n1)
    ang = 2.0 * np.pi * np.outer(idx, idx) / n1
    c, s = np.cos(ang), np.sin(ang)
    fa_pair = np.block([[c[:, :half], s[:, :half]], [-s[:, :half], c[:, :half]]])
    zero = np.zeros((n1, half))
    fa_real = np.block([[c[:, :half], zero], [-s[:, :half], zero], [zero, c[:, :half]], [zero, -s[:, :half]]])
    fa_inv = np.block([[c[:half], -s[:half]], [s[:half], c[:half]]]) / n
    k1 = idx[:, None, None]
    k2 = idx[None, :, None]
    n2 = idx[None, None, :]
    m = (n2 * (k1 + n1 * k2)) % n
    gang = 2.0 * np.pi * m / n
    gr, gi = np.cos(gang), -np.sin(gang)
    g_fwd = np.concatenate([np.concatenate([gr, -gi], axis=2), np.concatenate([gi, gr], axis=2)], axis=1)
    to = lambda a: jnp.asarray(a.astype(np.float32), BF16)
    return to(fa_pair), to(fa_real), to(fa_inv), to(g_fwd)


def _filter_spectrum_kernel(hf_ref, hb_ref, ss_ref, fa_ref, g_ref, o_ref, a_scr):
    n1 = fa_ref.shape[0] // 4
    n2 = g_ref.shape[1] // 2
    pitch = _pitch(n2)
    half = hf_ref.shape[0] // pitch
    kb = g_ref.shape[0]

    @pl.when(pl.program_id(1) == 0)
    def _():
        def body(j, carry):
            rows = jnp.concatenate([hf_ref[pl.ds(j, half, stride=pitch), :], hb_ref[pl.ds(j, half, stride=pitch), :]],
                                   axis=0)
            r = jnp.dot(fa_ref[...], rows.astype(BF16), preferred_element_type=F32)
            a_scr[0, pl.ds(j, n1, stride=pitch), :] = _pack_pair(r[:n1], r[n1:2 * n1])
            a_scr[1, pl.ds(j, n1, stride=pitch), :] = _pack_pair(r[2 * n1:3 * n1], r[3 * n1:])
            return carry

        lax.fori_loop(0, n2, body, 0, unroll=DFT_UNROLL)

    scale = lax.rsqrt(ss_ref[...] + NORM_EPS)
    for k in range(kb):
        row0 = pl.multiple_of((pl.program_id(1) * kb + k) * pitch, 8)
        fr, fi = _unpack_pair(a_scr[0, pl.ds(row0, n2), :])
        br, bi = _unpack_pair(a_scr[1, pl.ds(row0, n2), :])
        a = jnp.concatenate([jnp.concatenate([fr, fi], axis=0), jnp.concatenate([br, bi], axis=0)], axis=1)
        p = jnp.dot(g_ref[k], a.astype(BF16), preferred_element_type=F32)
        lanes = p.shape[1] // 2
        o_ref[k] = _pack_pair((p[:n2, :lanes] + p[:n2, lanes:]) * scale, (p[n2:, :lanes] - p[n2:, lanes:]) * scale)


def _filter_spectrum(fa_filt, g_fwd, hf, hb, energy, kb):
    n1 = g_fwd.shape[0]
    n2 = g_fwd.shape[1] // 2
    lp, d = hf.shape
    tc = LANES
    half_spec = pl.BlockSpec((lp, tc), lambda c, k: (0, c))
    return pl.pallas_call(
        _filter_spectrum_kernel,
        out_shape=jax.ShapeDtypeStruct((n1, n2, d), U32),
        grid=(d // tc, n1 // kb),
        in_specs=[
            half_spec, half_spec, pl.BlockSpec((1, tc), lambda c, k: (0, c)),
            _const_spec(fa_filt.shape),
            pl.BlockSpec((kb, 2 * n2, 2 * n2), lambda c, k: (k, 0, 0)),
        ],
        out_specs=pl.BlockSpec((kb, n2, tc), lambda c, k: (k, 0, c)),
        scratch_shapes=[pltpu.VMEM((2, n1 * _pitch(n2), tc), U32)],
        compiler_params=_params(("parallel", "arbitrary")),
        name="filter_spectrum",
    )(hf, hb, energy, fa_filt, g_fwd)


def _long_conv_kernel(z_ref, x0_ref, hf_ref, gf_ref, fa_ref, fi_ref, db_ref, o_ref, a_scr):
    n1 = fa_ref.shape[0] // 2
    n2 = gf_ref.shape[1] // 2
    half = n1 // 2
    pitch = _pitch(n2)
    kb = gf_ref.shape[0]
    step = pl.program_id(2)

    @pl.when(step == 0)
    def _():
        def body(j, carry):
            zr, zi = _unpack_pair(z_ref[0, pl.ds(j, half, stride=pitch), :])
            rows = jnp.concatenate([zr, zi], axis=0).astype(BF16)
            r = jnp.dot(fa_ref[...], rows, preferred_element_type=F32)
            a_scr[pl.ds(j, n1, stride=pitch), :] = _pack_pair(r[:n1], r[n1:])
            return carry

        lax.fori_loop(0, n2, body, 0, unroll=DFT_UNROLL)

    for k in range(kb):
        row0 = pl.multiple_of((step * kb + k) * pitch, 8)
        ar, ai = _unpack_pair(a_scr[pl.ds(row0, n2), :])
        x = jnp.dot(gf_ref[k], jnp.concatenate([ar, ai], axis=0).astype(BF16), preferred_element_type=F32)
        xr, xi = x[:n2], x[n2:]
        hr, hi = _unpack_pair(hf_ref[k])
        y = jnp.concatenate([xr * hr - xi * hi, xr * hi + xi * hr], axis=0).astype(BF16)
        t = lax.dot_general(gf_ref[k], y, (((0,), (0,)), ((), ())), preferred_element_type=F32)
        a_scr[pl.ds(row0, n2), :] = _pack_pair(t[:n2], t[n2:])

    @pl.when(step == pl.num_programs(2) - 1)
    def _():
        def body(j, carry):
            tr, ti = _unpack_pair(a_scr[pl.ds(j, n1, stride=pitch), :])
            y = jnp.dot(fi_ref[...], jnp.concatenate([tr, ti], axis=0).astype(BF16), preferred_element_type=F32)
            z0, z1 = _unpack_pair(z_ref[0, pl.ds(j, half, stride=pitch), :])
            g0, g1 = _unpack_pair(x0_ref[0, pl.ds(j, half, stride=pitch), :])
            db = db_ref[...]
            o_ref[0, pl.ds(j, half, stride=pitch), :] = _pack_pair(g0 * (y[:half] + z0 * db), g1 * (y[half:] + z1 * db))
            return carry

        lax.fori_loop(0, n2, body, 0, unroll=DFT_UNROLL_OUT)
        pad = jnp.zeros((pitch - n2, o_ref.shape[2]), U32)
        for s in range(half):
            o_ref[0, s * pitch + n2:(s + 1) * pitch] = pad


def _long_conv_gate(x0p, zp, hf, hb, energy, d_bias, l):
    p, lp, d = zp.shape
    n1 = math.isqrt(2 * l)
    assert n1 * n1 == 2 * l and lp == l // n1 * _pitch(n1)
    kb = min(32, n1)
    tc = LANES
    fa_pair, fa_real, fa_inv, g_fwd = _dft_tables(n1)
    spec = _filter_spectrum(fa_real, g_fwd, hf, hb, energy, kb)
    seq = pl.BlockSpec((1, lp, tc), lambda c, pi, k: (pi, 0, c))
    g_blk = pl.BlockSpec((kb, 2 * n1, 2 * n1), lambda c, pi, k: (k, 0, 0))
    return pl.pallas_call(
        _long_conv_kernel,
        out_shape=jax.ShapeDtypeStruct((p, lp, d), U32),
        grid=(d // tc, p, n1 // kb),
        in_specs=[
            seq, seq, pl.BlockSpec((kb, n1, tc), lambda c, pi, k: (k, 0, c)), g_blk,
            _const_spec(fa_pair.shape), _const_spec(fa_inv.shape), pl.BlockSpec((1, tc), lambda c, pi, k: (0, c)),
        ],
        out_specs=seq,
        scratch_shapes=[pltpu.VMEM((n1 * _pitch(n1), tc), U32)],
        compiler_params=_params(("parallel", "parallel", "arbitrary")),
        name="long_conv",
    )(zp, x0p, spec, g_fwd, fa_pair, fa_inv, d_bias.reshape(1, d))


def _mod_rows(mod, lo, hi, d):
    m = mod[lo:hi]
    return [m[:, None, j * d:(j + 1) * d] for j in range(N_MOD)]


def kernel(x, c, ctx, c_ctx, l0_w_mod, l0_b_mod, l0_norm1, l0_norm2, l0_na_w_qkv, l0_na_q_gain, l0_na_k_gain, l0_na_rpb, l0_na_w_o, l0_ffn_w_up, l0_ffn_conv_w, l0_ffn_conv_b, l0_ffn_w_down, l1_w_mod, l1_b_mod, l1_norm1, l1_norm2, l1_hy_w_in, l1_hy_b_in, l1_hy_short_w, l1_hy_short_b, l1_hy_f_w1, l1_hy_f_b1, l1_hy_f_w2, l1_hy_f_b2, l1_hy_f_w3, l1_hy_f_b3, l1_hy_f_wout, l1_hy_f_freq, l1_hy_d_bias, l1_hy_w_out, l1_hy_b_out, l1_ffn_w_up, l1_ffn_conv_w, l1_ffn_conv_b, l1_ffn_w_down):
    b, l, d = x.shape
    n_ctx = ctx.shape[1]
    n_rows = l // GRID_W
    assert n_rows >= K_ROWS and n_rows % (Q_ROWS * NA_SUB) == 0 and b % 2 == 0
    tm = min(512, l)
    tm_qkv = min(1024, l)
    n_dft = math.isqrt(2 * l)

    cond = jnp.zeros((8, d), F32).at[:b].set(c).at[b].set(c_ctx)

    mod = _adaln(cond, l0_w_mod, l0_b_mod)
    sh1, sc1, g1, sh2, sc2, g2 = _mod_rows(mod, 0, b, d)
    csh1, csc1 = _mod_rows(mod, b, b + 1, d)[:2]
    qkv = _qkv(x, sh1, sc1, l0_norm1, l0_na_w_qkv, l0_na_q_gain, l0_na_k_gain, tm_qkv)
    qkv_ctx = _qkv(ctx, csh1, csc1, l0_norm1, l0_na_w_qkv, l0_na_q_gain, l0_na_k_gain, n_ctx)
    bias = _block_bias(_col_bias(l0_na_rpb), n_rows)
    attn = _neighbourhood_attention(qkv, qkv_ctx, bias)
    x = _post(attn, x, l0_na_w_o, jnp.zeros((d,), F32), g1, l0_norm2, sh2, sc2, g2,
              l0_ffn_w_up, l0_ffn_conv_w, l0_ffn_conv_b, l0_ffn_w_down, tm)

    mod = _adaln(cond, l1_w_mod, l1_b_mod)
    sh1, sc1, g1, sh2, sc2, g2 = _mod_rows(mod, 0, b, d)
    x0p, zp = _hy_in(x, sh1, sc1, l1_norm1, l1_hy_w_in, l1_hy_b_in, l1_hy_short_w, l1_hy_short_b, tm, n_dft)
    hf, hb, energy = _hyena_filter(l, d, l1_hy_f_w1, l1_hy_f_b1, l1_hy_f_w2, l1_hy_f_b2, l1_hy_f_w3, l1_hy_f_b3,
                                   l1_hy_f_wout, l1_hy_f_freq, min(1024, l), n_dft)
    gated = _long_conv_gate(x0p, zp, hf, hb, energy, l1_hy_d_bias, l)
    x = _post(gated, x, l1_hy_w_out, l1_hy_b_out, g1, l1_norm2, sh2, sc2, g2,
              l1_ffn_w_up, l1_ffn_conv_w, l1_ffn_conv_b, l1_ffn_w_down, tm, packed_n2=n_dft)
    return x
```

```python
import functools
import math

import numpy as np
import jax
import jax.numpy as jnp
from jax import lax
from jax.experimental import pallas as pl
from jax.experimental.pallas import tpu as pltpu

F32 = jnp.float32
BF16 = jnp.bfloat16
U32 = jnp.uint32
HIGHEST = lax.Precision.HIGHEST

NORM_EPS = 1e-6
N_MOD = 6
HEAD_DIM = 64
GRID_W = 64
WIN_H = 8
WIN_W = 16
HY_BANDS = 16
HY_DECAY_TARGET = 1e-2
HY_SHORT_DECAY_PCT = 0.3
HY_LONG_DECAY_PCT = 1.5

LANES = 128
HALO = 16
Q_ROWS = 4
K_ROWS = Q_ROWS + WIN_H
NA_SUB = 4
NA_CHUNK = 32
DFT_UNROLL = 16
DFT_UNROLL_OUT = 8
LOG2E = math.log2(math.e)
NEG = -1e30
VMEM_LIMIT = 56 * 1024 * 1024


def _params(sem):
    return pltpu.CompilerParams(dimension_semantics=sem, vmem_limit_bytes=VMEM_LIMIT)


def _const_spec(shape):
    return pl.BlockSpec(shape, lambda *_: (0,) * len(shape), pipeline_mode=pl.Buffered(1))


def _rms_mod(x, gain, shift, scale):
    ms = jnp.mean(x * x, axis=-1, keepdims=True)
    y = x * lax.rsqrt(ms + NORM_EPS) * gain
    return y * (1.0 + scale) + shift


def _split_bf16(x):
    hi = x.astype(BF16)
    return hi, (x - hi.astype(F32)).astype(BF16)


def _dot3(x, w):
    x_hi, x_lo = _split_bf16(x)
    w_hi, w_lo = _split_bf16(w)
    return jnp.dot(jnp.concatenate([x_hi, x_lo, x_hi], axis=1), jnp.concatenate([w_hi, w_hi, w_lo], axis=0),
                   preferred_element_type=F32)


def _adaln_kernel(c_ref, w_ref, b_ref, o_ref):
    c = c_ref[...]
    s = c / (1.0 + jnp.exp(-c))
    o_ref[...] = _dot3(s, w_ref[...]) + b_ref[...]


def _adaln(cond, w_mod, b_mod):
    rows, d = cond.shape
    n = w_mod.shape[1]
    tn = d
    return pl.pallas_call(
        _adaln_kernel,
        out_shape=jax.ShapeDtypeStruct((rows, n), F32),
        grid=(n // tn,),
        in_specs=[
            pl.BlockSpec((rows, d), lambda j: (0, 0)),
            pl.BlockSpec((d, tn), lambda j: (0, j)),
            pl.BlockSpec((1, tn), lambda j: (0, j)),
        ],
        out_specs=pl.BlockSpec((rows, tn), lambda j: (0, j)),
        compiler_params=_params(("arbitrary",)),
        name="adaln",
    )(cond, w_mod, b_mod.reshape(1, n))


def _qkv_kernel(x_ref, sh_ref, sc_ref, g_ref, w_ref, qg_ref, kg_ref, p1_ref, p2_ref, o_ref):
    d = x_ref.shape[-1]
    h = _rms_mod(x_ref[0], g_ref[...], sh_ref[0], sc_ref[0]).astype(BF16)
    for j, gain_ref in ((0, qg_ref), (1, kg_ref)):
        t = jnp.dot(h, w_ref[:, j * d:(j + 1) * d], preferred_element_type=F32)
        ms = jnp.dot((t * t).astype(BF16), p1_ref[...], preferred_element_type=F32)
        r = lax.rsqrt(ms + NORM_EPS)
        r_hi = r.astype(BF16)
        r_lo = (r - r_hi.astype(F32)).astype(BF16)
        rr = jnp.dot(jnp.concatenate([r_hi, r_lo], axis=-1), p2_ref[...], preferred_element_type=F32)
        o_ref[0, :, j * d:(j + 1) * d] = (t * rr * gain_ref[...]).astype(BF16)
    v = jnp.dot(h, w_ref[:, 2 * d:3 * d], preferred_element_type=F32)
    o_ref[0, :, 2 * d:3 * d] = v.astype(BF16)


def _qkv(x, shift, scale, gain, w_qkv, q_gain, k_gain, tm):
    b, l, d = x.shape
    heads = d // HEAD_DIM
    per_batch = shift.shape[0] == b
    mod_map = (lambda bi, i: (bi, 0, 0)) if per_batch else (lambda bi, i: (0, 0, 0))
    p1 = np.zeros((d, LANES), np.float32)
    p1[np.arange(d), np.arange(d) // HEAD_DIM] = 1.0 / HEAD_DIM
    p2 = np.zeros((2 * LANES, d), np.float32)
    p2[np.arange(d) // HEAD_DIM, np.arange(d)] = 1.0
    p2[LANES + np.arange(d) // HEAD_DIM, np.arange(d)] = 1.0
    qg = (jnp.tile(q_gain, heads) * (HEAD_DIM ** -0.5 * LOG2E)).reshape(1, d)
    kg = jnp.tile(k_gain, heads).reshape(1, d)
    return pl.pallas_call(
        _qkv_kernel,
        out_shape=jax.ShapeDtypeStruct((b, l, 3 * d), BF16),
        grid=(b, l // tm),
        in_specs=[
            pl.BlockSpec((1, tm, d), lambda bi, i: (bi, i, 0)),
            pl.BlockSpec((1, 1, d), mod_map),
            pl.BlockSpec((1, 1, d), mod_map),
            _const_spec((1, d)),
            _const_spec((d, 3 * d)),
            _const_spec((1, d)),
            _const_spec((1, d)),
            _const_spec((d, LANES)),
            _const_spec((2 * LANES, d)),
        ],
        out_specs=pl.BlockSpec((1, tm, 3 * d), lambda bi, i: (bi, i, 0)),
        compiler_params=_params(("parallel", "parallel")),
        name="qkv",
    )(x, shift, scale, gain.reshape(1, d), w_qkv.astype(BF16), qg, kg,
      jnp.asarray(p1, BF16), jnp.asarray(p2, BF16))


def _toeplitz_kernel(r_ref, oh_ref, m_ref, o_ref):
    o_ref[...] = jnp.dot(r_ref[...], oh_ref[...], preferred_element_type=F32, precision=HIGHEST) + m_ref[...]


def _col_bias(rpb):
    heads, n_dr, n_dc = rpb.shape
    k_pad = 32
    qc = np.arange(GRID_W)[:, None]
    kc = np.arange(GRID_W)[None, :]
    c0 = np.clip(qc - WIN_W // 2, 0, GRID_W - WIN_W)
    valid = (kc >= c0) & (kc < c0 + WIN_W)
    dc = kc - qc + (WIN_W - 1)
    onehot = np.zeros((k_pad, GRID_W, GRID_W), np.float32)
    for j in range(n_dc):
        onehot[j] = ((dc == j) & valid).astype(np.float32)
    onehot = onehot.reshape(k_pad, GRID_W * GRID_W)
    mask = np.where(valid, 0.0, NEG).astype(np.float32).reshape(1, GRID_W * GRID_W)
    rows = heads * n_dr
    rows_pad = -(-rows // 8) * 8
    r2 = jnp.zeros((rows_pad, k_pad), F32).at[:rows, :n_dc].set(rpb.reshape(rows, n_dc))
    t = pl.pallas_call(
        _toeplitz_kernel,
        out_shape=jax.ShapeDtypeStruct((rows_pad, GRID_W * GRID_W), F32),
        name="rpb_toeplitz",
    )(r2, jnp.asarray(onehot), jnp.asarray(mask))
    return t[:rows].reshape(heads, n_dr, GRID_W, GRID_W)


def _block_bias(t, n_rows):
    heads = t.shape[0]
    kh = min(WIN_H, n_rows)
    masked = jnp.full((heads, GRID_W, GRID_W), NEG, F32)
    classes = []
    for q0, k0 in ((0, 0), (Q_ROWS, Q_ROWS - WIN_H // 2), (n_rows - Q_ROWS, n_rows - K_ROWS)):
        strips = []
        for qr in range(Q_ROWS):
            r = q0 + qr
            r0 = min(max(r - kh // 2, 0), n_rows - kh)
            blocks = []
            for kr in range(K_ROWS):
                kk = k0 + kr
                blocks.append(t[:, kk - r + WIN_H - 1] if r0 <= kk < r0 + kh else masked)
            strips.append(jnp.concatenate(blocks, axis=-1))
        classes.append(jnp.concatenate(strips, axis=-2))
    return (jnp.stack(classes) * LOG2E).astype(BF16)


def _na_stages(q_ref, k_ref, v_ref, kc_ref, vc_ref, bias_ref, o_ref, k_scr, v_scr, p_scr, s_new, s_old,
               *, sub_new, sub_old, n_sub):
    tq = q_ref.shape[1] // NA_SUB
    tk = bias_ref.shape[-1]
    n_ctx = kc_ref.shape[1]
    l = k_ref.shape[1]
    first_head = lax.broadcasted_iota(jnp.int32, (1, LANES), 1) < HEAD_DIM

    def window_start(sub):
        return pl.multiple_of(jnp.clip(sub * tq - (tk - tq) // 2, 0, l - tk), tq)

    for j in range(NA_SUB):
        k_scr[j, :tk] = k_ref[0, pl.ds(window_start(sub_new + j), tk), :]
        k_scr[j, tk:] = kc_ref[0]
        q = q_ref[0, j * tq:(j + 1) * tq]
        zero = jnp.zeros_like(q)
        q2 = jnp.concatenate([jnp.where(first_head, q, zero), jnp.where(first_head, zero, q)], axis=0)
        s_new[j] = lax.dot_general(q2, k_scr[j], (((1,), (1,)), ((), ())), preferred_element_type=F32)

    for j in range(NA_SUB):
        sub = sub_old + j
        v_scr[j, :tk, :LANES] = v_ref[0, pl.ds(window_start(sub), tk), :]
        v_scr[j, tk:, :LANES] = vc_ref[0]
        v_scr[j, :, LANES:] = jnp.ones((tk + n_ctx, LANES), BF16)
        edge = jnp.where(sub == 0, 0, jnp.where(sub == n_sub - 1, 2, 1))
        heads = []
        for h in range(2):
            for r in range(0, tq, NA_CHUNK):
                rows = slice(h * tq + r, h * tq + r + NA_CHUNK)
                s_loc = s_old[j, rows, :tk] + bias_ref[edge, h, r:r + NA_CHUNK, :].astype(F32)
                s_ctx = s_old[j, rows, tk:]
                m = jnp.maximum(jnp.max(s_loc, axis=-1, keepdims=True), jnp.max(s_ctx, axis=-1, keepdims=True))
                p_scr[j, rows, :tk] = jnp.exp2((s_loc - m).astype(BF16))
                p_scr[j, rows, tk:] = jnp.exp2((s_ctx - m).astype(BF16))
            o = jnp.dot(p_scr[j, h * tq:(h + 1) * tq], v_scr[j], preferred_element_type=F32)
            heads.append(o[:, :LANES] / o[:, LANES:])
        o_ref[0, j * tq:(j + 1) * tq] = jnp.where(first_head, heads[0], heads[1]).astype(BF16)


def _na_kernel(q_ref, k_ref, v_ref, kc_ref, vc_ref, bias_ref, o_ref, k_scr, v_scr, p_scr, s0, s1, *, nb):
    t = pl.program_id(0)
    last = pl.num_programs(0) - 2
    subs = dict(sub_new=(jnp.minimum(t, last) % nb) * NA_SUB, sub_old=(jnp.clip(t - 1, 0, last) % nb) * NA_SUB,
                n_sub=nb * NA_SUB)
    refs = (q_ref, k_ref, v_ref, kc_ref, vc_ref, bias_ref, o_ref, k_scr, v_scr, p_scr)

    @pl.when(t == 0)
    def _():
        s1[...] = jnp.zeros(s1.shape, F32)

    @pl.when(t % 2 == 0)
    def _():
        _na_stages(*refs, s0, s1, **subs)

    @pl.when(t % 2 == 1)
    def _():
        _na_stages(*refs, s1, s0, **subs)


def _neighbourhood_attention(qkv, qkv_ctx, bias):
    b, l, d3 = qkv.shape
    d = d3 // 3
    n_ctx = qkv_ctx.shape[1]
    pairs = d // LANES
    tq, tk = bias.shape[-2:]
    tb = NA_SUB * tq
    nb = l // tb
    steps = b * pairs * nb

    def decode(t):
        return t // (pairs * nb), (t // nb) % pairs, t % nb

    def lagged(t, lag):
        return decode(jnp.clip(t - lag, 0, steps - 1))

    def q_map(t):
        bi, hp, i = lagged(t, 0)
        return bi, i, hp

    def kv_map(which, lag):
        def index(t):
            bi, hp, _ = lagged(t, lag)
            return bi, 0, which * pairs + hp
        return index

    def bias_map(t):
        return 0, lagged(t, 1)[1], 0, 0

    def out_map(t):
        bi, hp, i = lagged(t, 1)
        return bi, i, hp

    n_keys = tk + n_ctx
    return pl.pallas_call(
        functools.partial(_na_kernel, nb=nb),
        out_shape=jax.ShapeDtypeStruct((b, l, d), BF16),
        grid=(steps + 1,),
        in_specs=[
            pl.BlockSpec((1, tb, LANES), q_map),
            pl.BlockSpec((1, l, LANES), kv_map(1, 0)),
            pl.BlockSpec((1, l, LANES), kv_map(2, 1)),
            pl.BlockSpec((1, n_ctx, LANES), kv_map(1, 0)),
            pl.BlockSpec((1, n_ctx, LANES), kv_map(2, 1)),
            pl.BlockSpec((3, 2, tq, tk), bias_map),
        ],
        out_specs=pl.BlockSpec((1, tb, LANES), out_map),
        scratch_shapes=[
            pltpu.VMEM((NA_SUB, n_keys, LANES), BF16),
            pltpu.VMEM((NA_SUB, n_keys, 2 * LANES), BF16),
            pltpu.VMEM((NA_SUB, 2 * tq, n_keys), BF16),
            pltpu.VMEM((NA_SUB, 2 * tq, n_keys), F32), pltpu.VMEM((NA_SUB, 2 * tq, n_keys), F32),
        ],
        compiler_params=_params(("arbitrary",)),
        name="na_attention",
    )(qkv, qkv, qkv, qkv_ctx, qkv_ctx, bias)


def _halo_rows(prev_ref, main_ref, next_ref, s=0):
    return jnp.concatenate([prev_ref[s], main_ref[s], next_ref[s]], axis=0)


def _zero_outside(u, axis, halo):
    i = pl.program_id(axis)
    rows = u.shape[0]
    keep_top = jnp.where(i == 0, 0.0, 1.0).astype(F32)
    keep_bottom = jnp.where(i == pl.num_programs(axis) - 1, 0.0, 1.0).astype(F32)
    return jnp.concatenate([u[:halo] * keep_top, u[halo:rows - halo], u[rows - halo:] * keep_bottom], axis=0)


def _shift_rows(g, rows):
    return pltpu.roll(g, 1, 0), pltpu.roll(g, rows - 1, 0)


def _pack_pair(hi, lo):
    hi_bits = pltpu.bitcast(hi.astype(BF16).astype(F32), U32)
    lo_bits = pltpu.bitcast(lo.astype(BF16).astype(F32), U32)
    return hi_bits | (lo_bits >> 16)


def _unpack_pair(word):
    return (pltpu.bitcast(word & jnp.uint32(0xFFFF0000), F32), pltpu.bitcast(word << 16, F32))


def _pitch(n2):
    return n2 + 8


def _packed_mix_rows(prev_ref, main_ref, next_ref, n2):
    pitch = _pitch(n2)
    slabs = main_ref.shape[1] // pitch
    words = jnp.concatenate([prev_ref[0]] + [main_ref[0, j * pitch:j * pitch + n2] for j in range(slabs)]
                            + [next_ref[0]], axis=0)
    shift = ((pl.program_id(0) % 2) * 16).astype(U32)
    return pltpu.bitcast((words << shift) & jnp.uint32(0xFFFF0000), F32).astype(BF16)


def _post_kernel(mp_ref, mm_ref, mn_ref, xp_ref, xm_ref, xn_ref, wmix_ref, bmix_ref, g1_ref, n2_ref,
                 sh_ref, sc_ref, g2_ref, wup_ref, cw_ref, cb_ref, wdn_ref, o_ref, *, packed_n2):
    halo = xp_ref.shape[1]
    rows = xm_ref.shape[1] + 2 * halo
    d_ff = wdn_ref.shape[0]
    if packed_n2:
        mix = _packed_mix_rows(mp_ref, mm_ref, mn_ref, packed_n2)
    else:
        mix = _halo_rows(mp_ref, mm_ref, mn_ref)
    x = _halo_rows(xp_ref, xm_ref, xn_ref)
    y = jnp.dot(mix, wmix_ref[...], preferred_element_type=F32) + bmix_ref[...]
    x1 = x + g1_ref[0] * y
    h = _rms_mod(x1, n2_ref[...], sh_ref[0], sc_ref[0]).astype(BF16)
    a = jnp.dot(h, wup_ref[:, :d_ff], preferred_element_type=F32)
    g = _zero_outside(jnp.dot(h, wup_ref[:, d_ff:], preferred_element_type=F32), 1, halo)
    g_prev, g_next = _shift_rows(g, rows)
    gc = (g_prev * cw_ref[0:1] + g * cw_ref[1:2] + g_next * cw_ref[2:3] + cb_ref[...])[halo:rows - halo]
    u = a[halo:rows - halo] * (0.5 * gc * (1.0 + lax.erf(gc * (2.0 ** -0.5))))
    ffn = jnp.dot(u.astype(BF16), wdn_ref[...], preferred_element_type=F32)
    o_ref[0] = x1[halo:rows - halo] + g2_ref[0] * ffn


def _halo_specs(tm, l, d, halo, nb=1):
    nblk = tm // halo
    last = l // halo - 1
    return [
        pl.BlockSpec((nb, halo, d), lambda bi, i: (bi, jnp.maximum(i * nblk - 1, 0), 0)),
        pl.BlockSpec((nb, tm, d), lambda bi, i: (bi, i, 0)),
        pl.BlockSpec((nb, halo, d), lambda bi, i: (bi, jnp.minimum((i + 1) * nblk, last), 0)),
    ]


def _packed_halo_specs(tm, l, d, n2):
    pitch = _pitch(n2)
    tp = tm // n2 * pitch
    last = l // n2 * pitch // 8 - 1
    return [
        pl.BlockSpec((1, 8, d), lambda bi, i: (bi // 2, jnp.maximum(i * (tp // 8) - 2, 0), 0)),
        pl.BlockSpec((1, tp, d), lambda bi, i: (bi // 2, i, 0)),
        pl.BlockSpec((1, 8, d), lambda bi, i: (bi // 2, jnp.minimum((i + 1) * (tp // 8), last), 0)),
    ]


def _post(mix, x, w_mix, b_mix, g1, norm2, sh2, sc2, g2, w_up, conv_w, conv_b, w_down, tm, packed_n2=0):
    b, l, d = x.shape
    d_ff = w_down.shape[0]
    mod = pl.BlockSpec((1, 1, d), lambda bi, i: (bi, 0, 0))
    halo = 8 if packed_n2 else HALO
    mix_specs = _packed_halo_specs(tm, l, d, packed_n2) if packed_n2 else _halo_specs(tm, l, d, halo)
    return pl.pallas_call(
        functools.partial(_post_kernel, packed_n2=packed_n2),
        out_shape=jax.ShapeDtypeStruct((b, l, d), F32),
        grid=(b, l // tm),
        in_specs=mix_specs + _halo_specs(tm, l, d, halo) + [
            _const_spec((d, d)), _const_spec((1, d)), mod, _const_spec((1, d)), mod, mod, mod,
            _const_spec((d, 2 * d_ff)), _const_spec((3, d_ff)), _const_spec((1, d_ff)), _const_spec((d_ff, d)),
        ],
        out_specs=pl.BlockSpec((1, tm, d), lambda bi, i: (bi, i, 0)),
        compiler_params=_params(("parallel", "parallel")),
        name="post_ffn",
    )(mix, mix, mix, x, x, x, w_mix.astype(BF16), b_mix.reshape(1, d), g1, norm2.reshape(1, d), sh2, sc2, g2,
      w_up.astype(BF16), conv_w, conv_b.reshape(1, d_ff), w_down.astype(BF16))


def _hy_in_kernel(xp_ref, xm_ref, xn_ref, sh_ref, sc_ref, g_ref, w_ref, cw_ref, cb_ref, e0_ref, e2_ref, x0_ref, z_ref,
                  *, n2):
    halo = xp_ref.shape[1]
    tm = xm_ref.shape[1]
    rows = tm + 2 * halo
    d = xm_ref.shape[-1]
    pitch = _pitch(n2)
    i = pl.program_id(1)
    row = lax.broadcasted_iota(jnp.int32, (8, 1), 0)
    at_start = jnp.where((row == 0) & (i == 0), 1.0, 0.0).astype(F32)
    at_end = jnp.where((row == 7) & (i == pl.num_programs(1) - 1), 1.0, 0.0).astype(F32)
    x0s, zs = [], []
    for s in range(2):
        h = _rms_mod(_halo_rows(xp_ref, xm_ref, xn_ref, s), g_ref[...], sh_ref[s], sc_ref[s]).astype(BF16)
        parts = []
        for j in range(3):
            cols = slice(j * d, (j + 1) * d)
            u = _zero_outside(jnp.dot(h, w_ref[:, cols], preferred_element_type=F32), 1, halo)
            u_prev, u_next = _shift_rows(u, rows)
            cw = cw_ref[:, cols]
            uc = (u_prev * cw[0:1] + u * cw[1:2] + u_next * cw[2:3] + cb_ref[:, cols])[halo:rows - halo]
            parts.append(jnp.concatenate([uc[:8] - at_start * e0_ref[:, cols], uc[8:tm - 8],
                                          uc[tm - 8:] - at_end * e2_ref[:, cols]], axis=0))
        x0s.append(parts[0])
        zs.append(parts[2] * parts[1])
    pad = jnp.zeros((pitch - n2, d), U32)
    for ref, pair in ((x0_ref, x0s), (z_ref, zs)):
        words = _pack_pair(pair[0], pair[1])
        for j in range(tm // n2):
            ref[0, j * pitch:j * pitch + n2] = words[j * n2:(j + 1) * n2]
            ref[0, j * pitch + n2:(j + 1) * pitch] = pad


def _hy_in(x, shift, scale, gain, w_in, b_in, short_w, short_b, tm, n2):
    b, l, d = x.shape
    pitch = _pitch(n2)
    tp = tm // n2 * pitch
    mod = pl.BlockSpec((2, 1, d), lambda pi, i: (pi, 0, 0))
    out = pl.BlockSpec((1, tp, d), lambda pi, i: (pi, i, 0))
    shape = jax.ShapeDtypeStruct((b // 2, l // n2 * pitch, d), U32)
    return pl.pallas_call(
        functools.partial(_hy_in_kernel, n2=n2),
        out_shape=(shape, shape),
        grid=(b // 2, l // tm),
        in_specs=_halo_specs(tm, l, d, 8, 2) + [
            mod, mod, _const_spec((1, d)), _const_spec((d, 3 * d)), _const_spec((3, 3 * d)),
            _const_spec((1, 3 * d)), _const_spec((1, 3 * d)), _const_spec((1, 3 * d)),
        ],
        out_specs=(out, out),
        compiler_params=_params(("parallel", "parallel")),
        name="hyena_in",
    )(x, x, x, shift, scale, gain.reshape(1, d), w_in.astype(BF16), short_w,
      (short_b + b_in * short_w.sum(axis=0)).reshape(1, 3 * d), (b_in * short_w[0]).reshape(1, 3 * d),
      (b_in * short_w[2]).reshape(1, 3 * d))


FILTER_GROUPS = 8


def _filter_kernel(bands_ref, w1t_ref, w1c_ref, w1s_ref, b1_ref, w2_ref, b2_ref, w3_ref, b3_ref, wo_ref,
                   freq_ref, delta_ref, hf_ref, hb_ref, ss_ref, *, seq_len, n2):
    pitch = _pitch(n2)
    tl = hf_ref.shape[0] // pitch * n2
    d = hf_ref.shape[1]
    groups = FILTER_GROUPS
    rows = tl // groups
    width = b1_ref.shape[1] // groups
    base = pl.program_id(0) * tl

    def positions(lanes_per_group, n_lanes):
        lane = lax.broadcasted_iota(jnp.int32, (rows, n_lanes), 1)
        row = lax.broadcasted_iota(jnp.int32, (rows, n_lanes), 0)
        return (base + (lane // lanes_per_group) * rows + row).astype(F32)

    ang = bands_ref[...] * ((2.0 * math.pi / seq_len) * positions(HY_BANDS, groups * HY_BANDS))
    t_wide = positions(width, groups * width) * (1.0 / (seq_len - 1))
    freq = freq_ref[...]
    pre = t_wide * w1t_ref[...] + _dot3(jnp.cos(ang), w1c_ref[...]) - _dot3(jnp.sin(ang), w1s_ref[...])
    hdn = jnp.sin(freq * (pre + b1_ref[...]))
    hdn = jnp.sin(freq * (_dot3(hdn, w2_ref[...]) + b2_ref[...]))
    hdn = jnp.sin(freq * (_dot3(hdn, w3_ref[...]) + b3_ref[...]))

    pad = jnp.zeros((pitch - n2, d), F32)
    energy = jnp.zeros((1, d), F32)
    col = lax.broadcasted_iota(jnp.int32, (rows, 1), 0)
    for q in range(groups // 2):
        h2 = _dot3(hdn[:, 2 * q * width:(2 * q + 2) * width], wo_ref[...])
        for e in range(2):
            p = 2 * q + e
            pos = (base + p * rows + col).astype(F32)
            decay = jnp.exp(-(pos * (1.0 / (seq_len - 1))) * delta_ref[...])
            hf = h2[:, e * 2 * d:e * 2 * d + d] * decay
            hb = h2[:, e * 2 * d + d:(e + 1) * 2 * d] * decay
            energy = energy + jnp.sum(hf * hf + hb * hb, axis=0, keepdims=True)
            for ref, taps in ((hf_ref, hf), (hb_ref, jnp.where(pos > 0.0, hb, 0.0))):
                for j in range(rows // n2):
                    slab = p * (rows // n2) + j
                    ref[slab * pitch:slab * pitch + n2] = taps[j * n2:(j + 1) * n2]
                    ref[slab * pitch + n2:(slab + 1) * pitch] = pad

    @pl.when(pl.program_id(0) == 0)
    def _():
        ss_ref[...] = jnp.zeros_like(ss_ref)

    ss_ref[...] += energy


def _hyena_filter(seq_len, d, w1, b1, w2, b2, w3, b3, w_out, freq, tl, n2):
    groups = FILTER_GROUPS
    assert (tl // groups) % n2 == 0
    tp = tl // n2 * _pitch(n2)
    padded = jax.ShapeDtypeStruct((seq_len // n2 * _pitch(n2), d), F32)
    bands = np.linspace(1e-4, HY_BANDS - 1, HY_BANDS, dtype=np.float32)
    deltas = np.abs(np.linspace(math.log(HY_DECAY_TARGET) / HY_SHORT_DECAY_PCT,
                                math.log(HY_DECAY_TARGET) / HY_LONG_DECAY_PCT, d, dtype=np.float32)).reshape(1, d)
    eye = jnp.eye(groups, dtype=F32)
    tile = lambda v: jnp.tile(v.reshape(1, -1), (1, groups))
    operands = [
        jnp.asarray(np.tile(bands, groups).reshape(1, -1)), tile(w1[0]), jnp.kron(eye, w1[1:1 + HY_BANDS]),
        jnp.kron(eye, w1[1 + HY_BANDS:]), tile(b1), jnp.kron(eye, w2), tile(b2), jnp.kron(eye, w3), tile(b3),
        jnp.kron(jnp.eye(2, dtype=F32), w_out), tile(freq), jnp.asarray(deltas),
    ]
    return pl.pallas_call(
        functools.partial(_filter_kernel, seq_len=seq_len, n2=n2),
        out_shape=(padded, padded, jax.ShapeDtypeStruct((1, d), F32)),
        grid=(seq_len // tl,),
        in_specs=[pl.BlockSpec(o.shape, lambda i: (0, 0)) for o in operands],
        out_specs=(pl.BlockSpec((tp, d), lambda i: (i, 0)), pl.BlockSpec((tp, d), lambda i: (i, 0)),
                   pl.BlockSpec((1, d), lambda i: (0, 0))),
        compiler_params=_params(("arbitrary",)),
        name="hyena_filter",
    )(*operands)


def _dft_tables(n1):
    n = n1 * n1
    half = n1 // 2
    idx = np.arange(n1)
    ang = 2.0 * np.pi * np.outer(idx, idx) / n1
    c, s = np.cos(ang), np.sin(ang)
    fa_pair = np.block([[c[:, :half], s[:, :half]], [-s[:, :half], c[:, :half]]])
    zero = np.zeros((n1, half))
    fa_real = np.block([[c[:, :half], zero], [-s[:, :half], zero], [zero, c[:, :half]], [zero, -s[:, :half]]])
    fa_inv = np.block([[c[:half], -s[:half]], [s[:half], c[:half]]]) / n
    k1 = idx[:, None, None]
    k2 = idx[None, :, None]
    n2 = idx[None, None, :]
    m = (n2 * (k1 + n1 * k2)) % n
    gang = 2.0 * np.pi * m / n
    gr, gi = np.cos(gang), -np.sin(gang)
    g_fwd = np.concatenate([np.concatenate([gr, -gi], axis=2), np.concatenate([gi, gr], axis=2)], axis=1)
    to = lambda a: jnp.asarray(a.astype(np.float32), BF16)
    return to(fa_pair), to(fa_real), to(fa_inv), to(g_fwd)


def _filter_spectrum_kernel(hf_ref, hb_ref, ss_ref, fa_ref, g_ref, o_ref, a_scr):
    n1 = fa_ref.shape[0] // 4
    n2 = g_ref.shape[1] // 2
    pitch = _pitch(n2)
    half = hf_ref.shape[0] // pitch
    kb = g_ref.shape[0]

    @pl.when(pl.program_id(1) == 0)
    def _():
        def body(j, carry):
            rows = jnp.concatenate([hf_ref[pl.ds(j, half, stride=pitch), :], hb_ref[pl.ds(j, half, stride=pitch), :]],
                                   axis=0)
            r = jnp.dot(fa_ref[...], rows.astype(BF16), preferred_element_type=F32)
            a_scr[0, pl.ds(j, n1, stride=pitch), :] = _pack_pair(r[:n1], r[n1:2 * n1])
            a_scr[1, pl.ds(j, n1, stride=pitch), :] = _pack_pair(r[2 * n1:3 * n1], r[3 * n1:])
            return carry

        lax.fori_loop(0, n2, body, 0, unroll=DFT_UNROLL)

    scale = lax.rsqrt(ss_ref[...] + NORM_EPS)
    for k in range(kb):
        row0 = pl.multiple_of((pl.program_id(1) * kb + k) * pitch, 8)
        fr, fi = _unpack_pair(a_scr[0, pl.ds(row0, n2), :])
        br, bi = _unpack_pair(a_scr[1, pl.ds(row0, n2), :])
        a = jnp.concatenate([jnp.concatenate([fr, fi], axis=0), jnp.concatenate([br, bi], axis=0)], axis=1)
        p = jnp.dot(g_ref[k], a.astype(BF16), preferred_element_type=F32)
        lanes = p.shape[1] // 2
        o_ref[k] = _pack_pair((p[:n2, :lanes] + p[:n2, lanes:]) * scale, (p[n2:, :lanes] - p[n2:, lanes:]) * scale)


def _filter_spectrum(fa_filt, g_fwd, hf, hb, energy, kb):
    n1 = g_fwd.shape[0]
    n2 = g_fwd.shape[1] // 2
    lp, d = hf.shape
    tc = LANES
    half_spec = pl.BlockSpec((lp, tc), lambda c, k: (0, c))
    return pl.pallas_call(
        _filter_spectrum_kernel,
        out_shape=jax.ShapeDtypeStruct((n1, n2, d), U32),
        grid=(d // tc, n1 // kb),
        in_specs=[
            half_spec, half_spec, pl.BlockSpec((1, tc), lambda c, k: (0, c)),
            _const_spec(fa_filt.shape),
            pl.BlockSpec((kb, 2 * n2, 2 * n2), lambda c, k: (k, 0, 0)),
        ],
        out_specs=pl.BlockSpec((kb, n2, tc), lambda c, k: (k, 0, c)),
        scratch_shapes=[pltpu.VMEM((2, n1 * _pitch(n2), tc), U32)],
        compiler_params=_params(("parallel", "arbitrary")),
        name="filter_spectrum",
    )(hf, hb, energy, fa_filt, g_fwd)


def _long_conv_kernel(z_ref, x0_ref, hf_ref, gf_ref, fa_ref, fi_ref, db_ref, o_ref, a_scr):
    n1 = fa_ref.shape[0] // 2
    n2 = gf_ref.shape[1] // 2
    half = n1 // 2
    pitch = _pitch(n2)
    kb = gf_ref.shape[0]
    step = pl.program_id(2)

    @pl.when(step == 0)
    def _():
        def body(j, carry):
            zr, zi = _unpack_pair(z_ref[0, pl.ds(j, half, stride=pitch), :])
            rows = jnp.concatenate([zr, zi], axis=0).astype(BF16)
            r = jnp.dot(fa_ref[...], rows, preferred_element_type=F32)
            a_scr[pl.ds(j, n1, stride=pitch), :] = _pack_pair(r[:n1], r[n1:])
            return carry

        lax.fori_loop(0, n2, body, 0, unroll=DFT_UNROLL)

    for k in range(kb):
        row0 = pl.multiple_of((step * kb + k) * pitch, 8)
        ar, ai = _unpack_pair(a_scr[pl.ds(row0, n2), :])
        x = jnp.dot(gf_ref[k], jnp.concatenate([ar, ai], axis=0).astype(BF16), preferred_element_type=F32)
        xr, xi = x[:n2], x[n2:]
        hr, hi = _unpack_pair(hf_ref[k])
        y = jnp.concatenate([xr * hr - xi * hi, xr * hi + xi * hr], axis=0).astype(BF16)
        t = lax.dot_general(gf_ref[k], y, (((0,), (0,)), ((), ())), preferred_element_type=F32)
        a_scr[pl.ds(row0, n2), :] = _pack_pair(t[:n2], t[n2:])

    @pl.when(step == pl.num_programs(2) - 1)
    def _():
        def body(j, carry):
            tr, ti = _unpack_pair(a_scr[pl.ds(j, n1, stride=pitch), :])
            y = jnp.dot(fi_ref[...], jnp.concatenate([tr, ti], axis=0).astype(BF16), preferred_element_type=F32)
            z0, z1 = _unpack_pair(z_ref[0, pl.ds(j, half, stride=pitch), :])
            g0, g1 = _unpack_pair(x0_ref[0, pl.ds(j, half, stride=pitch), :])
            db = db_ref[...]
            o_ref[0, pl.ds(j, half, stride=pitch), :] = _pack_pair(g0 * (y[:half] + z0 * db), g1 * (y[half:] + z1 * db))
            return carry

        lax.fori_loop(0, n2, body, 0, unroll=DFT_UNROLL_OUT)
        pad = jnp.zeros((pitch - n2, o_ref.shape[2]), U32)
        for s in range(half):
            o_ref[0, s * pitch + n2:(s + 1) * pitch] = pad


def _long_conv_gate(x0p, zp, hf, hb, energy, d_bias, l):
    p, lp, d = zp.shape
    n1 = math.isqrt(2 * l)
    assert n1 * n1 == 2 * l and lp == l // n1 * _pitch(n1)
    kb = min(32, n1)
    tc = LANES
    fa_pair, fa_real, fa_inv, g_fwd = _dft_tables(n1)
    spec = _filter_spectrum(fa_real, g_fwd, hf, hb, energy, kb)
    seq = pl.BlockSpec((1, lp, tc), lambda c, pi, k: (pi, 0, c))
    g_blk = pl.BlockSpec((kb, 2 * n1, 2 * n1), lambda c, pi, k: (k, 0, 0))
    return pl.pallas_call(
        _long_conv_kernel,
        out_shape=jax.ShapeDtypeStruct((p, lp, d), U32),
        grid=(d // tc, p, n1 // kb),
        in_specs=[
            seq, seq, pl.BlockSpec((kb, n1, tc), lambda c, pi, k: (k, 0, c)), g_blk,
            _const_spec(fa_pair.shape), _const_spec(fa_inv.shape), pl.BlockSpec((1, tc), lambda c, pi, k: (0, c)),
        ],
        out_specs=seq,
        scratch_shapes=[pltpu.VMEM((n1 * _pitch(n1), tc), U32)],
        compiler_params=_params(("parallel", "parallel", "arbitrary")),
        name="long_conv",
    )(zp, x0p, spec, g_fwd, fa_pair, fa_inv, d_bias.reshape(1, d))


def _mod_rows(mod, lo, hi, d):
    m = mod[lo:hi]
    return [m[:, None, j * d:(j + 1) * d] for j in range(N_MOD)]


def kernel(x, c, ctx, c_ctx, l0_w_mod, l0_b_mod, l0_norm1, l0_norm2, l0_na_w_qkv, l0_na_q_gain, l0_na_k_gain, l0_na_rpb, l0_na_w_o, l0_ffn_w_up, l0_ffn_conv_w, l0_ffn_conv_b, l0_ffn_w_down, l1_w_mod, l1_b_mod, l1_norm1, l1_norm2, l1_hy_w_in, l1_hy_b_in, l1_hy_short_w, l1_hy_short_b, l1_hy_f_w1, l1_hy_f_b1, l1_hy_f_w2, l1_hy_f_b2, l1_hy_f_w3, l1_hy_f_b3, l1_hy_f_wout, l1_hy_f_freq, l1_hy_d_bias, l1_hy_w_out, l1_hy_b_out, l1_ffn_w_up, l1_ffn_conv_w, l1_ffn_conv_b, l1_ffn_w_down):
    b, l, d = x.shape
    n_ctx = ctx.shape[1]
    n_rows = l // GRID_W
    assert n_rows >= K_ROWS and n_rows % (Q_ROWS * NA_SUB) == 0 and b % 2 == 0
    tm = min(512, l)
    tm_qkv = min(1024, l)
    n_dft = math.isqrt(2 * l)

    cond = jnp.zeros((8, d), F32).at[:b].set(c).at[b].set(c_ctx)

    mod = _adaln(cond, l0_w_mod, l0_b_mod)
    sh1, sc1, g1, sh2, sc2, g2 = _mod_rows(mod, 0, b, d)
    csh1, csc1 = _mod_rows(mod, b, b + 1, d)[:2]
    qkv = _qkv(x, sh1, sc1, l0_norm1, l0_na_w_qkv, l0_na_q_gain, l0_na_k_gain, tm_qkv)
    qkv_ctx = _qkv(ctx, csh1, csc1, l0_norm1, l0_na_w_qkv, l0_na_q_gain, l0_na_k_gain, n_ctx)
    bias = _block_bias(_col_bias(l0_na_rpb), n_rows)
    attn = _neighbourhood_attention(qkv, qkv_ctx, bias)
    x = _post(attn, x, l0_na_w_o, jnp.zeros((d,), F32), g1, l0_norm2, sh2, sc2, g2,
              l0_ffn_w_up, l0_ffn_conv_w, l0_ffn_conv_b, l0_ffn_w_down, tm)

    mod = _adaln(cond, l1_w_mod, l1_b_mod)
    sh1, sc1, g1, sh2, sc2, g2 = _mod_rows(mod, 0, b, d)
    x0p, zp = _hy_in(x, sh1, sc1, l1_norm1, l1_hy_w_in, l1_hy_b_in, l1_hy_short_w, l1_hy_short_b, tm, n_dft)
    hf, hb, energy = _hyena_filter(l, d, l1_hy_f_w1, l1_hy_f_b1, l1_hy_f_w2, l1_hy_f_b2, l1_hy_f_w3, l1_hy_f_b3,
                                   l1_hy_f_wout, l1_hy_f_freq, min(1024, l), n_dft)
    gated = _long_conv_gate(x0p, zp, hf, hb, energy, l1_hy_d_bias, l)
    x = _post(gated, x, l1_hy_w_out, l1_hy_b_out, g1, l1_norm2, sh2, sc2, g2,
              l1_ffn_w_up, l1_ffn_conv_w, l1_ffn_conv_b, l1_ffn_w_down, tm, packed_n2=n_dft)
    return x
```

```python
import functools
import math

import numpy as np
import jax
import jax.numpy as jnp
from jax import lax
from jax.experimental import pallas as pl
from jax.experimental.pallas import tpu as pltpu

F32 = jnp.float32
BF16 = jnp.bfloat16
U32 = jnp.uint32
HIGHEST = lax.Precision.HIGHEST

NORM_EPS = 1e-6
N_MOD = 6
HEAD_DIM = 64
GRID_W = 64
WIN_H = 8
WIN_W = 16
HY_BANDS = 16
HY_DECAY_TARGET = 1e-2
HY_SHORT_DECAY_PCT = 0.3
HY_LONG_DECAY_PCT = 1.5

LANES = 128
HALO = 16
Q_ROWS = 4
K_ROWS = Q_ROWS + WIN_H
NA_SUB = 4
NA_CHUNK = 32
DFT_UNROLL = 16
DFT_UNROLL_OUT = 8
LOG2E = math.log2(math.e)
NEG = -1e30
VMEM_LIMIT = 56 * 1024 * 1024


def _params(sem):
    return pltpu.CompilerParams(dimension_semantics=sem, vmem_limit_bytes=VMEM_LIMIT)


def _const_spec(shape):
    return pl.BlockSpec(shape, lambda *_: (0,) * len(shape), pipeline_mode=pl.Buffered(1))


def _rms_mod(x, gain, shift, scale):
    ms = jnp.mean(x * x, axis=-1, keepdims=True)
    y = x * lax.rsqrt(ms + NORM_EPS) * gain
    return y * (1.0 + scale) + shift


def _split_bf16(x):
    hi = x.astype(BF16)
    return hi, (x - hi.astype(F32)).astype(BF16)


def _dot3(x, w):
    x_hi, x_lo = _split_bf16(x)
    w_hi, w_lo = _split_bf16(w)
    return jnp.dot(jnp.concatenate([x_hi, x_lo, x_hi], axis=1), jnp.concatenate([w_hi, w_hi, w_lo], axis=0),
                   preferred_element_type=F32)


def _adaln_kernel(c_ref, w_ref, b_ref, o_ref):
    c = c_ref[...]
    s = c / (1.0 + jnp.exp(-c))
    o_ref[...] = _dot3(s, w_ref[...]) + b_ref[...]


def _adaln(cond, w_mod, b_mod):
    rows, d = cond.shape
    n = w_mod.shape[1]
    tn = d
    return pl.pallas_call(
        _adaln_kernel,
        out_shape=jax.ShapeDtypeStruct((rows, n), F32),
        grid=(n // tn,),
        in_specs=[
            pl.BlockSpec((rows, d), lambda j: (0, 0)),
            pl.BlockSpec((d, tn), lambda j: (0, j)),
            pl.BlockSpec((1, tn), lambda j: (0, j)),
        ],
        out_specs=pl.BlockSpec((rows, tn), lambda j: (0, j)),
        compiler_params=_params(("arbitrary",)),
        name="adaln",
    )(cond, w_mod, b_mod.reshape(1, n))


def _qkv_kernel(x_ref, sh_ref, sc_ref, g_ref, w_ref, qg_ref, kg_ref, p1_ref, p2_ref, o_ref):
    d = x_ref.shape[-1]
    h = _rms_mod(x_ref[0], g_ref[...], sh_ref[0], sc_ref[0]).astype(BF16)
    for j, gain_ref in ((0, qg_ref), (1, kg_ref)):
        t = jnp.dot(h, w_ref[:, j * d:(j + 1) * d], preferred_element_type=F32)
        ms = jnp.dot((t * t).astype(BF16), p1_ref[...], preferred_element_type=F32)
        r = lax.rsqrt(ms + NORM_EPS)
        r_hi = r.astype(BF16)
        r_lo = (r - r_hi.astype(F32)).astype(BF16)
        rr = jnp.dot(jnp.concatenate([r_hi, r_lo], axis=-1), p2_ref[...], preferred_element_type=F32)
        o_ref[0, :, j * d:(j + 1) * d] = (t * rr * gain_ref[...]).astype(BF16)
    v = jnp.dot(h, w_ref[:, 2 * d:3 * d], preferred_element_type=F32)
    o_ref[0, :, 2 * d:3 * d] = v.astype(BF16)


def _qkv(x, shift, scale, gain, w_qkv, q_gain, k_gain, tm):
    b, l, d = x.shape
    heads = d // HEAD_DIM
    per_batch = shift.shape[0] == b
    mod_map = (lambda bi, i: (bi, 0, 0)) if per_batch else (lambda bi, i: (0, 0, 0))
    p1 = np.zeros((d, LANES), np.float32)
    p1[np.arange(d), np.arange(d) // HEAD_DIM] = 1.0 / HEAD_DIM
    p2 = np.zeros((2 * LANES, d), np.float32)
    p2[np.arange(d) // HEAD_DIM, np.arange(d)] = 1.0
    p2[LANES + np.arange(d) // HEAD_DIM, np.arange(d)] = 1.0
    qg = (jnp.tile(q_gain, heads) * (HEAD_DIM ** -0.5 * LOG2E)).reshape(1, d)
    kg = jnp.tile(k_gain, heads).reshape(1, d)
    return pl.pallas_call(
        _qkv_kernel,
        out_shape=jax.ShapeDtypeStruct((b, l, 3 * d), BF16),
        grid=(b, l // tm),
        in_specs=[
            pl.BlockSpec((1, tm, d), lambda bi, i: (bi, i, 0)),
            pl.BlockSpec((1, 1, d), mod_map),
            pl.BlockSpec((1, 1, d), mod_map),
            _const_spec((1, d)),
            _const_spec((d, 3 * d)),
            _const_spec((1, d)),
            _const_spec((1, d)),
            _const_spec((d, LANES)),
            _const_spec((2 * LANES, d)),
        ],
        out_specs=pl.BlockSpec((1, tm, 3 * d), lambda bi, i: (bi, i, 0)),
        compiler_params=_params(("parallel", "parallel")),
        name="qkv",
    )(x, shift, scale, gain.reshape(1, d), w_qkv.astype(BF16), qg, kg,
      jnp.asarray(p1, BF16), jnp.asarray(p2, BF16))


def _toeplitz_kernel(r_ref, oh_ref, m_ref, o_ref):
    o_ref[...] = jnp.dot(r_ref[...], oh_ref[...], preferred_element_type=F32, precision=HIGHEST) + m_ref[...]


def _col_bias(rpb):
    heads, n_dr, n_dc = rpb.shape
    k_pad = 32
    qc = np.arange(GRID_W)[:, None]
    kc = np.arange(GRID_W)[None, :]
    c0 = np.clip(qc - WIN_W // 2, 0, GRID_W - WIN_W)
    valid = (kc >= c0) & (kc < c0 + WIN_W)
    dc = kc - qc + (WIN_W - 1)
    onehot = np.zeros((k_pad, GRID_W, GRID_W), np.float32)
    for j in range(n_dc):
        onehot[j] = ((dc == j) & valid).astype(np.float32)
    onehot = onehot.reshape(k_pad, GRID_W * GRID_W)
    mask = np.where(valid, 0.0, NEG).astype(np.float32).reshape(1, GRID_W * GRID_W)
    rows = heads * n_dr
    rows_pad = -(-rows // 8) * 8
    r2 = jnp.zeros((rows_pad, k_pad), F32).at[:rows, :n_dc].set(rpb.reshape(rows, n_dc))
    t = pl.pallas_call(
        _toeplitz_kernel,
        out_shape=jax.ShapeDtypeStruct((rows_pad, GRID_W * GRID_W), F32),
        name="rpb_toeplitz",
    )(r2, jnp.asarray(onehot), jnp.asarray(mask))
    return t[:rows].reshape(heads, n_dr, GRID_W, GRID_W)


def _block_bias(t, n_rows):
    heads = t.shape[0]
    kh = min(WIN_H, n_rows)
    masked = jnp.full((heads, GRID_W, GRID_W), NEG, F32)
    classes = []
    for q0, k0 in ((0, 0), (Q_ROWS, Q_ROWS - WIN_H // 2), (n_rows - Q_ROWS, n_rows - K_ROWS)):
        strips = []
        for qr in range(Q_ROWS):
            r = q0 + qr
            r0 = min(max(r - kh // 2, 0), n_rows - kh)
            blocks = []
            for kr in range(K_ROWS):
                kk = k0 + kr
                blocks.append(t[:, kk - r + WIN_H - 1] if r0 <= kk < r0 + kh else masked)
            strips.append(jnp.concatenate(blocks, axis=-1))
        classes.append(jnp.concatenate(strips, axis=-2))
    return (jnp.stack(classes) * LOG2E).astype(BF16)


def _na_stages(q_ref, k_ref, v_ref, kc_ref, vc_ref, bias_ref, o_ref, k_scr, v_scr, p_scr, s_new, s_old,
               *, sub_new, sub_old, n_sub):
    tq = q_ref.shape[1] // NA_SUB
    tk = bias_ref.shape[-1]
    n_ctx = kc_ref.shape[1]
    l = k_ref.shape[1]
    first_head = lax.broadcasted_iota(jnp.int32, (1, LANES), 1) < HEAD_DIM

    def window_start(sub):
        return pl.multiple_of(jnp.clip(sub * tq - (tk - tq) // 2, 0, l - tk), tq)

    for j in range(NA_SUB):
        k_scr[j, :tk] = k_ref[0, pl.ds(window_start(sub_new + j), tk), :]
        k_scr[j, tk:] = kc_ref[0]
        q = q_ref[0, j * tq:(j + 1) * tq]
        zero = jnp.zeros_like(q)
        q2 = jnp.concatenate([jnp.where(first_head, q, zero), jnp.where(first_head, zero, q)], axis=0)
        s_new[j] = lax.dot_general(q2, k_scr[j], (((1,), (1,)), ((), ())), preferred_element_type=F32)

    for j in range(NA_SUB):
        sub = sub_old + j
        v_scr[j, :tk, :LANES] = v_ref[0, pl.ds(window_start(sub), tk), :]
        v_scr[j, tk:, :LANES] = vc_ref[0]
        v_scr[j, :, LANES:] = jnp.ones((tk + n_ctx, LANES), BF16)
        edge = jnp.where(sub == 0, 0, jnp.where(sub == n_sub - 1, 2, 1))
        heads = []
        for h in range(2):
            for r in range(0, tq, NA_CHUNK):
                rows = slice(h * tq + r, h * tq + r + NA_CHUNK)
                s_loc = s_old[j, rows, :tk] + bias_ref[edge, h, r:r + NA_CHUNK, :].astype(F32)
                s_ctx = s_old[j, rows, tk:]
                m = jnp.maximum(jnp.max(s_loc, axis=-1, keepdims=True), jnp.max(s_ctx, axis=-1, keepdims=True))
                p_scr[j, rows, :tk] = jnp.exp2((s_loc - m).astype(BF16))
                p_scr[j, rows, tk:] = jnp.exp2((s_ctx - m).astype(BF16))
            o = jnp.dot(p_scr[j, h * tq:(h + 1) * tq], v_scr[j], preferred_element_type=F32)
            heads.append(o[:, :LANES] / o[:, LANES:])
        o_ref[0, j * tq:(j + 1) * tq] = jnp.where(first_head, heads[0], heads[1]).astype(BF16)


def _na_kernel(q_ref, k_ref, v_ref, kc_ref, vc_ref, bias_ref, o_ref, k_scr, v_scr, p_scr, s0, s1, *, nb):
    t = pl.program_id(0)
    last = pl.num_programs(0) - 2
    subs = dict(sub_new=(jnp.minimum(t, last) % nb) * NA_SUB, sub_old=(jnp.clip(t - 1, 0, last) % nb) * NA_SUB,
                n_sub=nb * NA_SUB)
    refs = (q_ref, k_ref, v_ref, kc_ref, vc_ref, bias_ref, o_ref, k_scr, v_scr, p_scr)

    @pl.when(t == 0)
    def _():
        s1[...] = jnp.zeros(s1.shape, F32)

    @pl.when(t % 2 == 0)
    def _():
        _na_stages(*refs, s0, s1, **subs)

    @pl.when(t % 2 == 1)
    def _():
        _na_stages(*refs, s1, s0, **subs)


def _neighbourhood_attention(qkv, qkv_ctx, bias):
    b, l, d3 = qkv.shape
    d = d3 // 3
    n_ctx = qkv_ctx.shape[1]
    pairs = d // LANES
    tq, tk = bias.shape[-2:]
    tb = NA_SUB * tq
    nb = l // tb
    steps = b * pairs * nb

    def decode(t):
        return t // (pairs * nb), (t // nb) % pairs, t % nb

    def lagged(t, lag):
        return decode(jnp.clip(t - lag, 0, steps - 1))

    def q_map(t):
        bi, hp, i = lagged(t, 0)
        return bi, i, hp

    def kv_map(which, lag):
        def index(t):
            bi, hp, _ = lagged(t, lag)
            return bi, 0, which * pairs + hp
        return index

    def bias_map(t):
        return 0, lagged(t, 1)[1], 0, 0

    def out_map(t):
        bi, hp, i = lagged(t, 1)
        return bi, i, hp

    n_keys = tk + n_ctx
    return pl.pallas_call(
        functools.partial(_na_kernel, nb=nb),
        out_shape=jax.ShapeDtypeStruct((b, l, d), BF16),
        grid=(steps + 1,),
        in_specs=[
            pl.BlockSpec((1, tb, LANES), q_map),
            pl.BlockSpec((1, l, LANES), kv_map(1, 0)),
            pl.BlockSpec((1, l, LANES), kv_map(2, 1)),
            pl.BlockSpec((1, n_ctx, LANES), kv_map(1, 0)),
            pl.BlockSpec((1, n_ctx, LANES), kv_map(2, 1)),
            pl.BlockSpec((3, 2, tq, tk), bias_map),
        ],
        out_specs=pl.BlockSpec((1, tb, LANES), out_map),
        scratch_shapes=[
            pltpu.VMEM((NA_SUB, n_keys, LANES), BF16),
            pltpu.VMEM((NA_SUB, n_keys, 2 * LANES), BF16),
            pltpu.VMEM((NA_SUB, 2 * tq, n_keys), BF16),
            pltpu.VMEM((NA_SUB, 2 * tq, n_keys), F32), pltpu.VMEM((NA_SUB, 2 * tq, n_keys), F32),
        ],
        compiler_params=_params(("arbitrary",)),
        name="na_attention",
    )(qkv, qkv, qkv, qkv_ctx, qkv_ctx, bias)


def _halo_rows(prev_ref, main_ref, next_ref, s=0):
    return jnp.concatenate([prev_ref[s], main_ref[s], next_ref[s]], axis=0)


def _zero_outside(u, axis, halo):
    i = pl.program_id(axis)
    rows = u.shape[0]
    keep_top = jnp.where(i == 0, 0.0, 1.0).astype(F32)
    keep_bottom = jnp.where(i == pl.num_programs(axis) - 1, 0.0, 1.0).astype(F32)
    return jnp.concatenate([u[:halo] * keep_top, u[halo:rows - halo], u[rows - halo:] * keep_bottom], axis=0)


def _shift_rows(g, rows):
    return pltpu.roll(g, 1, 0), pltpu.roll(g, rows - 1, 0)


def _pack_pair(hi, lo):
    hi_bits = pltpu.bitcast(hi.astype(BF16).astype(F32), U32)
    lo_bits = pltpu.bitcast(lo.astype(BF16).astype(F32), U32)
    return hi_bits | (lo_bits >> 16)


def _unpack_pair(word):
    return (pltpu.bitcast(word & jnp.uint32(0xFFFF0000), F32), pltpu.bitcast(word << 16, F32))


def _pitch(n2):
    return n2 + 8


def _packed_mix_rows(prev_ref, main_ref, next_ref, n2):
    pitch = _pitch(n2)
    slabs = main_ref.shape[1] // pitch
    words = jnp.concatenate([prev_ref[0]] + [main_ref[0, j * pitch:j * pitch + n2] for j in range(slabs)]
                            + [next_ref[0]], axis=0)
    shift = ((pl.program_id(0) % 2) * 16).astype(U32)
    return pltpu.bitcast((words << shift) & jnp.uint32(0xFFFF0000), F32).astype(BF16)


def _post_kernel(mp_ref, mm_ref, mn_ref, xp_ref, xm_ref, xn_ref, wmix_ref, bmix_ref, g1_ref, n2_ref,
                 sh_ref, sc_ref, g2_ref, wup_ref, cw_ref, cb_ref, wdn_ref, o_ref, *, packed_n2):
    halo = xp_ref.shape[1]
    rows = xm_ref.shape[1] + 2 * halo
    d_ff = wdn_ref.shape[0]
    if packed_n2:
        mix = _packed_mix_rows(mp_ref, mm_ref, mn_ref, packed_n2)
    else:
        mix = _halo_rows(mp_ref, mm_ref, mn_ref)
    x = _halo_rows(xp_ref, xm_ref, xn_ref)
    y = jnp.dot(mix, wmix_ref[...], preferred_element_type=F32) + bmix_ref[...]
    x1 = x + g1_ref[0] * y
    h32 = _rms_mod(x1, n2_ref[...], sh_ref[0], sc_ref[0])
    h = h32.astype(BF16)
    a = jnp.dot(h32[halo:rows - halo].astype(BF16), wup_ref[:, :d_ff], preferred_element_type=F32)
    g = _zero_outside(jnp.dot(h, wup_ref[:, d_ff:], preferred_element_type=F32), 1, halo)
    g_prev, g_next = _shift_rows(g, rows)
    gc = (g_prev * cw_ref[0:1] + g * cw_ref[1:2] + g_next * cw_ref[2:3] + cb_ref[...])[halo:rows - halo]
    u = a * (0.5 * gc * (1.0 + lax.erf(gc * (2.0 ** -0.5))))
    ffn = jnp.dot(u.astype(BF16), wdn_ref[...], preferred_element_type=F32)
    o_ref[0] = x1[halo:rows - halo] + g2_ref[0] * ffn


def _halo_specs(tm, l, d, halo, nb=1):
    nblk = tm // halo
    last = l // halo - 1
    return [
        pl.BlockSpec((nb, halo, d), lambda bi, i: (bi, jnp.maximum(i * nblk - 1, 0), 0)),
        pl.BlockSpec((nb, tm, d), lambda bi, i: (bi, i, 0)),
        pl.BlockSpec((nb, halo, d), lambda bi, i: (bi, jnp.minimum((i + 1) * nblk, last), 0)),
    ]


def _packed_halo_specs(tm, l, d, n2):
    pitch = _pitch(n2)
    tp = tm // n2 * pitch
    last = l // n2 * pitch // 8 - 1
    return [
        pl.BlockSpec((1, 8, d), lambda bi, i: (bi // 2, jnp.maximum(i * (tp // 8) - 2, 0), 0)),
        pl.BlockSpec((1, tp, d), lambda bi, i: (bi // 2, i, 0)),
        pl.BlockSpec((1, 8, d), lambda bi, i: (bi // 2, jnp.minimum((i + 1) * (tp // 8), last), 0)),
    ]


def _post(mix, x, w_mix, b_mix, g1, norm2, sh2, sc2, g2, w_up, conv_w, conv_b, w_down, tm, packed_n2=0):
    b, l, d = x.shape
    d_ff = w_down.shape[0]
    mod = pl.BlockSpec((1, 1, d), lambda bi, i: (bi, 0, 0))
    halo = 8 if packed_n2 else HALO
    mix_specs = _packed_halo_specs(tm, l, d, packed_n2) if packed_n2 else _halo_specs(tm, l, d, halo)
    return pl.pallas_call(
        functools.partial(_post_kernel, packed_n2=packed_n2),
        out_shape=jax.ShapeDtypeStruct((b, l, d), F32),
        grid=(b, l // tm),
        in_specs=mix_specs + _halo_specs(tm, l, d, halo) + [
            _const_spec((d, d)), _const_spec((1, d)), mod, _const_spec((1, d)), mod, mod, mod,
            _const_spec((d, 2 * d_ff)), _const_spec((3, d_ff)), _const_spec((1, d_ff)), _const_spec((d_ff, d)),
        ],
        out_specs=pl.BlockSpec((1, tm, d), lambda bi, i: (bi, i, 0)),
        compiler_params=_params(("parallel", "parallel")),
        name="post_ffn",
    )(mix, mix, mix, x, x, x, w_mix.astype(BF16), b_mix.reshape(1, d), g1, norm2.reshape(1, d), sh2, sc2, g2,
      w_up.astype(BF16), conv_w, conv_b.reshape(1, d_ff), w_down.astype(BF16))


def _hy_in_kernel(xp_ref, xm_ref, xn_ref, sh_ref, sc_ref, g_ref, w_ref, cw_ref, cb_ref, e0_ref, e2_ref, x0_ref, z_ref,
                  *, n2):
    halo = xp_ref.shape[1]
    tm = xm_ref.shape[1]
    rows = tm + 2 * halo
    d = xm_ref.shape[-1]
    pitch = _pitch(n2)
    i = pl.program_id(1)
    row = lax.broadcasted_iota(jnp.int32, (8, 1), 0)
    at_start = jnp.where((row == 0) & (i == 0), 1.0, 0.0).astype(F32)
    at_end = jnp.where((row == 7) & (i == pl.num_programs(1) - 1), 1.0, 0.0).astype(F32)
    x0s, zs = [], []
    for s in range(2):
        h = _rms_mod(_halo_rows(xp_ref, xm_ref, xn_ref, s), g_ref[...], sh_ref[s], sc_ref[s]).astype(BF16)
        parts = []
        for j in range(3):
            cols = slice(j * d, (j + 1) * d)
            u = _zero_outside(jnp.dot(h, w_ref[:, cols], preferred_element_type=F32), 1, halo)
            u_prev, u_next = _shift_rows(u, rows)
            cw = cw_ref[:, cols]
            uc = (u_prev * cw[0:1] + u * cw[1:2] + u_next * cw[2:3] + cb_ref[:, cols])[halo:rows - halo]
            parts.append(jnp.concatenate([uc[:8] - at_start * e0_ref[:, cols], uc[8:tm - 8],
                                          uc[tm - 8:] - at_end * e2_ref[:, cols]], axis=0))
        x0s.append(parts[0])
        zs.append(parts[2] * parts[1])
    pad = jnp.zeros((pitch - n2, d), U32)
    for ref, pair in ((x0_ref, x0s), (z_ref, zs)):
        words = _pack_pair(pair[0], pair[1])
        for j in range(tm // n2):
            ref[0, j * pitch:j * pitch + n2] = words[j * n2:(j + 1) * n2]
            ref[0, j * pitch + n2:(j + 1) * pitch] = pad


def _hy_in(x, shift, scale, gain, w_in, b_in, short_w, short_b, tm, n2):
    b, l, d = x.shape
    pitch = _pitch(n2)
    tp = tm // n2 * pitch
    mod = pl.BlockSpec((2, 1, d), lambda pi, i: (pi, 0, 0))
    out = pl.BlockSpec((1, tp, d), lambda pi, i: (pi, i, 0))
    shape = jax.ShapeDtypeStruct((b // 2, l // n2 * pitch, d), U32)
    return pl.pallas_call(
        functools.partial(_hy_in_kernel, n2=n2),
        out_shape=(shape, shape),
        grid=(b // 2, l // tm),
        in_specs=_halo_specs(tm, l, d, 8, 2) + [
            mod, mod, _const_spec((1, d)), _const_spec((d, 3 * d)), _const_spec((3, 3 * d)),
            _const_spec((1, 3 * d)), _const_spec((1, 3 * d)), _const_spec((1, 3 * d)),
        ],
        out_specs=(out, out),
        compiler_params=_params(("parallel", "parallel")),
        name="hyena_in",
    )(x, x, x, shift, scale, gain.reshape(1, d), w_in.astype(BF16), short_w,
      (short_b + b_in * short_w.sum(axis=0)).reshape(1, 3 * d), (b_in * short_w[0]).reshape(1, 3 * d),
      (b_in * short_w[2]).reshape(1, 3 * d))


FILTER_GROUPS = 8


def _filter_kernel(bands_ref, w1t_ref, w1c_ref, w1s_ref, b1_ref, w2_ref, b2_ref, w3_ref, b3_ref, wo_ref,
                   freq_ref, delta_ref, hf_ref, hb_ref, ss_ref, *, seq_len, n2):
    pitch = _pitch(n2)
    tl = hf_ref.shape[0] // pitch * n2
    d = hf_ref.shape[1]
    groups = FILTER_GROUPS
    rows = tl // groups
    width = b1_ref.shape[1] // groups
    base = pl.program_id(0) * tl

    def positions(lanes_per_group, n_lanes):
        lane = lax.broadcasted_iota(jnp.int32, (rows, n_lanes), 1)
        row = lax.broadcasted_iota(jnp.int32, (rows, n_lanes), 0)
        return (base + (lane // lanes_per_group) * rows + row).astype(F32)

    ang = bands_ref[...] * ((2.0 * math.pi / seq_len) * positions(HY_BANDS, groups * HY_BANDS))
    t_wide = positions(width, groups * width) * (1.0 / (seq_len - 1))
    freq = freq_ref[...]
    pre = t_wide * w1t_ref[...] + _dot3(jnp.cos(ang), w1c_ref[...]) - _dot3(jnp.sin(ang), w1s_ref[...])
    hdn = jnp.sin(freq * (pre + b1_ref[...]))
    hdn = jnp.sin(freq * (_dot3(hdn, w2_ref[...]) + b2_ref[...]))
    hdn = jnp.sin(freq * (_dot3(hdn, w3_ref[...]) + b3_ref[...]))

    pad = jnp.zeros((pitch - n2, d), F32)
    energy = jnp.zeros((1, d), F32)
    col = lax.broadcasted_iota(jnp.int32, (rows, 1), 0)
    for q in range(groups // 2):
        h2 = _dot3(hdn[:, 2 * q * width:(2 * q + 2) * width], wo_ref[...])
        for e in range(2):
            p = 2 * q + e
            pos = (base + p * rows + col).astype(F32)
            decay = jnp.exp(-(pos * (1.0 / (seq_len - 1))) * delta_ref[...])
            hf = h2[:, e * 2 * d:e * 2 * d + d] * decay
            hb = h2[:, e * 2 * d + d:(e + 1) * 2 * d] * decay
            energy = energy + jnp.sum(hf * hf + hb * hb, axis=0, keepdims=True)
            for ref, taps in ((hf_ref, hf), (hb_ref, jnp.where(pos > 0.0, hb, 0.0))):
                for j in range(rows // n2):
                    slab = p * (rows // n2) + j
                    ref[slab * pitch:slab * pitch + n2] = taps[j * n2:(j + 1) * n2]
                    ref[slab * pitch + n2:(slab + 1) * pitch] = pad

    @pl.when(pl.program_id(0) == 0)
    def _():
        ss_ref[...] = jnp.zeros_like(ss_ref)

    ss_ref[...] += energy


def _hyena_filter(seq_len, d, w1, b1, w2, b2, w3, b3, w_out, freq, tl, n2):
    groups = FILTER_GROUPS
    assert (tl // groups) % n2 == 0
    tp = tl // n2 * _pitch(n2)
    padded = jax.ShapeDtypeStruct((seq_len // n2 * _pitch(n2), d), F32)
    bands = np.linspace(1e-4, HY_BANDS - 1, HY_BANDS, dtype=np.float32)
    deltas = np.abs(np.linspace(math.log(HY_DECAY_TARGET) / HY_SHORT_DECAY_PCT,
                                math.log(HY_DECAY_TARGET) / HY_LONG_DECAY_PCT, d, dtype=np.float32)).reshape(1, d)
    eye = jnp.eye(groups, dtype=F32)
    tile = lambda v: jnp.tile(v.reshape(1, -1), (1, groups))
    operands = [
        jnp.asarray(np.tile(bands, groups).reshape(1, -1)), tile(w1[0]), jnp.kron(eye, w1[1:1 + HY_BANDS]),
        jnp.kron(eye, w1[1 + HY_BANDS:]), tile(b1), jnp.kron(eye, w2), tile(b2), jnp.kron(eye, w3), tile(b3),
        jnp.kron(jnp.eye(2, dtype=F32), w_out), tile(freq), jnp.asarray(deltas),
    ]
    return pl.pallas_call(
        functools.partial(_filter_kernel, seq_len=seq_len, n2=n2),
        out_shape=(padded, padded, jax.ShapeDtypeStruct((1, d), F32)),
        grid=(seq_len // tl,),
        in_specs=[pl.BlockSpec(o.shape, lambda i: (0, 0)) for o in operands],
        out_specs=(pl.BlockSpec((tp, d), lambda i: (i, 0)), pl.BlockSpec((tp, d), lambda i: (i, 0)),
                   pl.BlockSpec((1, d), lambda i: (0, 0))),
        compiler_params=_params(("arbitrary",)),
        name="hyena_filter",
    )(*operands)


def _dft_tables(n1):
    n = n1 * n1
    half = n1 // 2
    idx = np.arange(n1)
    ang = 2.0 * np.pi * np.outer(idx, idx) / n1
    c, s = np.cos(ang), np.sin(ang)
    fa_pair = np.block([[c[:, :half], s[:, :half]], [-s[:, :half], c[:, :half]]])
    zero = np.zeros((n1, half))
    fa_real = np.block([[c[:, :half], zero], [-s[:, :half], zero], [zero, c[:, :half]], [zero, -s[:, :half]]])
    fa_inv = np.block([[c[:half], -s[:half]], [s[:half], c[:half]]]) / n
    k1 = idx[:, None, None]
    k2 = idx[None, :, None]
    n2 = idx[None, None, :]
    m = (n2 * (k1 + n1 * k2)) % n
    gang = 2.0 * np.pi * m / n
    gr, gi = np.cos(gang), -np.sin(gang)
    g_fwd = np.concatenate([np.concatenate([gr, -gi], axis=2), np.concatenate([gi, gr], axis=2)], axis=1)
    to = lambda a: jnp.asarray(a.astype(np.float32), BF16)
    return to(fa_pair), to(fa_real), to(fa_inv), to(g_fwd)


def _filter_spectrum_kernel(hf_ref, hb_ref, ss_ref, fa_ref, g_ref, o_ref, a_scr):
    n1 = fa_ref.shape[0] // 4
    n2 = g_ref.shape[1] // 2
    pitch = _pitch(n2)
    half = hf_ref.shape[0] // pitch
    kb = g_ref.shape[0]

    @pl.when(pl.program_id(1) == 0)
    def _():
        def body(j, carry):
            rows = jnp.concatenate([hf_ref[pl.ds(j, half, stride=pitch), :], hb_ref[pl.ds(j, half, stride=pitch), :]],
                                   axis=0)
            r = jnp.dot(fa_ref[...], rows.astype(BF16), preferred_element_type=F32)
            a_scr[0, pl.ds(j, n1, stride=pitch), :] = _pack_pair(r[:n1], r[n1:2 * n1])
            a_scr[1, pl.ds(j, n1, stride=pitch), :] = _pack_pair(r[2 * n1:3 * n1], r[3 * n1:])
            return carry

        lax.fori_loop(0, n2, body, 0, unroll=DFT_UNROLL)

    scale = lax.rsqrt(ss_ref[...] + NORM_EPS)
    for k in range(kb):
        row0 = pl.multiple_of((pl.program_id(1) * kb + k) * pitch, 8)
        fr, fi = _unpack_pair(a_scr[0, pl.ds(row0, n2), :])
        br, bi = _unpack_pair(a_scr[1, pl.ds(row0, n2), :])
        a = jnp.concatenate([jnp.concatenate([fr, fi], axis=0), jnp.concatenate([br, bi], axis=0)], axis=1)
        p = jnp.dot(g_ref[k], a.astype(BF16), preferred_element_type=F32)
        lanes = p.shape[1] // 2
        o_ref[k] = _pack_pair((p[:n2, :lanes] + p[:n2, lanes:]) * scale, (p[n2:, :lanes] - p[n2:, lanes:]) * scale)


def _filter_spectrum(fa_filt, g_fwd, hf, hb, energy, kb):
    n1 = g_fwd.shape[0]
    n2 = g_fwd.shape[1] // 2
    lp, d = hf.shape
    tc = LANES
    half_spec = pl.BlockSpec((lp, tc), lambda c, k: (0, c))
    return pl.pallas_call(
        _filter_spectrum_kernel,
        out_shape=jax.ShapeDtypeStruct((n1, n2, d), U32),
        grid=(d // tc, n1 // kb),
        in_specs=[
            half_spec, half_spec, pl.BlockSpec((1, tc), lambda c, k: (0, c)),
            _const_spec(fa_filt.shape),
            pl.BlockSpec((kb, 2 * n2, 2 * n2), lambda c, k: (k, 0, 0)),
        ],
        out_specs=pl.BlockSpec((kb, n2, tc), lambda c, k: (k, 0, c)),
        scratch_shapes=[pltpu.VMEM((2, n1 * _pitch(n2), tc), U32)],
        compiler_params=_params(("parallel", "arbitrary")),
        name="filter_spectrum",
    )(hf, hb, energy, fa_filt, g_fwd)


def _long_conv_kernel(z_ref, x0_ref, hf_ref, gf_ref, fa_ref, fi_ref, db_ref, o_ref, a_scr):
    n1 = fa_ref.shape[0] // 2
    n2 = gf_ref.shape[1] // 2
    half = n1 // 2
    pitch = _pitch(n2)
    kb = gf_ref.shape[0]
    step = pl.program_id(2)

    @pl.when(step == 0)
    def _():
        def body(j, carry):
            zr, zi = _unpack_pair(z_ref[0, pl.ds(j, half, stride=pitch), :])
            rows = jnp.concatenate([zr, zi], axis=0).astype(BF16)
            r = jnp.dot(fa_ref[...], rows, preferred_element_type=F32)
            a_scr[pl.ds(j, n1, stride=pitch), :] = _pack_pair(r[:n1], r[n1:])
            return carry

        lax.fori_loop(0, n2, body, 0, unroll=DFT_UNROLL)

    for k in range(kb):
        row0 = pl.multiple_of((step * kb + k) * pitch, 8)
        ar, ai = _unpack_pair(a_scr[pl.ds(row0, n2), :])
        x = jnp.dot(gf_ref[k], jnp.concatenate([ar, ai], axis=0).astype(BF16), preferred_element_type=F32)
        xr, xi = x[:n2], x[n2:]
        hr, hi = _unpack_pair(hf_ref[k])
        y = jnp.concatenate([xr * hr - xi * hi, xr * hi + xi * hr], axis=0).astype(BF16)
        t = lax.dot_general(gf_ref[k], y, (((0,), (0,)), ((), ())), preferred_element_type=F32)
        a_scr[pl.ds(row0, n2), :] = _pack_pair(t[:n2], t[n2:])

    @pl.when(step == pl.num_programs(2) - 1)
    def _():
        def body(j, carry):
            tr, ti = _unpack_pair(a_scr[pl.ds(j, n1, stride=pitch), :])
            y = jnp.dot(fi_ref[...], jnp.concatenate([tr, ti], axis=0).astype(BF16), preferred_element_type=F32)
            z0, z1 = _unpack_pair(z_ref[0, pl.ds(j, half, stride=pitch), :])
            g0, g1 = _unpack_pair(x0_ref[0, pl.ds(j, half, stride=pitch), :])
            db = db_ref[...]
            o_ref[0, pl.ds(j, half, stride=pitch), :] = _pack_pair(g0 * (y[:half] + z0 * db), g1 * (y[half:] + z1 * db))
            return carry

        lax.fori_loop(0, n2, body, 0, unroll=DFT_UNROLL_OUT)
        pad = jnp.zeros((pitch - n2, o_ref.shape[2]), U32)
        for s in range(half):
            o_ref[0, s * pitch + n2:(s + 1) * pitch] = pad


def _long_conv_gate(x0p, zp, hf, hb, energy, d_bias, l):
    p, lp, d = zp.shape
    n1 = math.isqrt(2 * l)
    assert n1 * n1 == 2 * l and lp == l // n1 * _pitch(n1)
    kb = min(32, n1)
    tc = LANES
    fa_pair, fa_real, fa_inv, g_fwd = _dft_tables(n1)
    spec = _filter_spectrum(fa_real, g_fwd, hf, hb, energy, kb)
    seq = pl.BlockSpec((1, lp, tc), lambda c, pi, k: (pi, 0, c))
    g_blk = pl.BlockSpec((kb, 2 * n1, 2 * n1), lambda c, pi, k: (k, 0, 0))
    return pl.pallas_call(
        _long_conv_kernel,
        out_shape=jax.ShapeDtypeStruct((p, lp, d), U32),
        grid=(d // tc, p, n1 // kb),
        in_specs=[
            seq, seq, pl.BlockSpec((kb, n1, tc), lambda c, pi, k: (k, 0, c)), g_blk,
            _const_spec(fa_pair.shape), _const_spec(fa_inv.shape), pl.BlockSpec((1, tc), lambda c, pi, k: (0, c)),
        ],
        out_specs=seq,
        scratch_shapes=[pltpu.VMEM((n1 * _pitch(n1), tc), U32)],
        compiler_params=_params(("parallel", "parallel", "arbitrary")),
        name="long_conv",
    )(zp, x0p, spec, g_fwd, fa_pair, fa_inv, d_bias.reshape(1, d))


def _mod_rows(mod, lo, hi, d):
    m = mod[lo:hi]
    return [m[:, None, j * d:(j + 1) * d] for j in range(N_MOD)]


def kernel(x, c, ctx, c_ctx, l0_w_mod, l0_b_mod, l0_norm1, l0_norm2, l0_na_w_qkv, l0_na_q_gain, l0_na_k_gain, l0_na_rpb, l0_na_w_o, l0_ffn_w_up, l0_ffn_conv_w, l0_ffn_conv_b, l0_ffn_w_down, l1_w_mod, l1_b_mod, l1_norm1, l1_norm2, l1_hy_w_in, l1_hy_b_in, l1_hy_short_w, l1_hy_short_b, l1_hy_f_w1, l1_hy_f_b1, l1_hy_f_w2, l1_hy_f_b2, l1_hy_f_w3, l1_hy_f_b3, l1_hy_f_wout, l1_hy_f_freq, l1_hy_d_bias, l1_hy_w_out, l1_hy_b_out, l1_ffn_w_up, l1_ffn_conv_w, l1_ffn_conv_b, l1_ffn_w_down):
    b, l, d = x.shape
    n_ctx = ctx.shape[1]
    n_rows = l // GRID_W
    assert n_rows >= K_ROWS and n_rows % (Q_ROWS * NA_SUB) == 0 and b % 2 == 0
    tm = min(512, l)
    tm_qkv = min(1024, l)
    n_dft = math.isqrt(2 * l)

    cond = jnp.zeros((8, d), F32).at[:b].set(c).at[b].set(c_ctx)

    mod = _adaln(cond, l0_w_mod, l0_b_mod)
    sh1, sc1, g1, sh2, sc2, g2 = _mod_rows(mod, 0, b, d)
    csh1, csc1 = _mod_rows(mod, b, b + 1, d)[:2]
    qkv = _qkv(x, sh1, sc1, l0_norm1, l0_na_w_qkv, l0_na_q_gain, l0_na_k_gain, tm_qkv)
    qkv_ctx = _qkv(ctx, csh1, csc1, l0_norm1, l0_na_w_qkv, l0_na_q_gain, l0_na_k_gain, n_ctx)
    bias = _block_bias(_col_bias(l0_na_rpb), n_rows)
    attn = _neighbourhood_attention(qkv, qkv_ctx, bias)
    x = _post(attn, x, l0_na_w_o, jnp.zeros((d,), F32), g1, l0_norm2, sh2, sc2, g2,
              l0_ffn_w_up, l0_ffn_conv_w, l0_ffn_conv_b, l0_ffn_w_down, tm)

    mod = _adaln(cond, l1_w_mod, l1_b_mod)
    sh1, sc1, g1, sh2, sc2, g2 = _mod_rows(mod, 0, b, d)
    x0p, zp = _hy_in(x, sh1, sc1, l1_norm1, l1_hy_w_in, l1_hy_b_in, l1_hy_short_w, l1_hy_short_b, tm, n_dft)
    hf, hb, energy = _hyena_filter(l, d, l1_hy_f_w1, l1_hy_f_b1, l1_hy_f_w2, l1_hy_f_b2, l1_hy_f_w3, l1_hy_f_b3,
                                   l1_hy_f_wout, l1_hy_f_freq, min(1024, l), n_dft)
    gated = _long_conv_gate(x0p, zp, hf, hb, energy, l1_hy_d_bias, l)
    x = _post(gated, x, l1_hy_w_out, l1_hy_b_out, g1, l1_norm2, sh2, sc2, g2,
              l1_ffn_w_up, l1_ffn_conv_w, l1_ffn_conv_b, l1_ffn_w_down, tm, packed_n2=n_dft)
    return x
```

```python
import functools
import math

import numpy as np
import jax
import jax.numpy as jnp
from jax import lax
from jax.experimental import pallas as pl
from jax.experimental.pallas import tpu as pltpu

F32 = jnp.float32
BF16 = jnp.bfloat16
U32 = jnp.uint32
HIGHEST = lax.Precision.HIGHEST

NORM_EPS = 1e-6
N_MOD = 6
HEAD_DIM = 64
GRID_W = 64
WIN_H = 8
WIN_W = 16
HY_BANDS = 16
HY_DECAY_TARGET = 1e-2
HY_SHORT_DECAY_PCT = 0.3
HY_LONG_DECAY_PCT = 1.5

LANES = 128
HALO = 16
Q_ROWS = 4
K_ROWS = Q_ROWS + WIN_H
NA_SUB = 4
NA_CHUNK = 32
DFT_UNROLL = 16
DFT_UNROLL_OUT = 16
LOG2E = math.log2(math.e)
NEG = -1e30
VMEM_LIMIT = 56 * 1024 * 1024


def _params(sem):
    return pltpu.CompilerParams(dimension_semantics=sem, vmem_limit_bytes=VMEM_LIMIT)


def _const_spec(shape):
    return pl.BlockSpec(shape, lambda *_: (0,) * len(shape), pipeline_mode=pl.Buffered(1))


def _rms_mod(x, gain, shift, scale):
    ms = jnp.mean(x * x, axis=-1, keepdims=True)
    y = x * lax.rsqrt(ms + NORM_EPS) * gain
    return y * (1.0 + scale) + shift


def _split_bf16(x):
    hi = x.astype(BF16)
    return hi, (x - hi.astype(F32)).astype(BF16)


def _dot3(x, w):
    x_hi, x_lo = _split_bf16(x)
    w_hi, w_lo = _split_bf16(w)
    return jnp.dot(jnp.concatenate([x_hi, x_lo, x_hi], axis=1), jnp.concatenate([w_hi, w_hi, w_lo], axis=0),
                   preferred_element_type=F32)


def _adaln_kernel(c_ref, w_ref, b_ref, o_ref):
    c = c_ref[...]
    s = c / (1.0 + jnp.exp(-c))
    o_ref[...] = _dot3(s, w_ref[...]) + b_ref[...]


def _adaln(cond, w_mod, b_mod):
    rows, d = cond.shape
    n = w_mod.shape[1]
    tn = d
    return pl.pallas_call(
        _adaln_kernel,
        out_shape=jax.ShapeDtypeStruct((rows, n), F32),
        grid=(n // tn,),
        in_specs=[
            pl.BlockSpec((rows, d), lambda j: (0, 0)),
            pl.BlockSpec((d, tn), lambda j: (0, j)),
            pl.BlockSpec((1, tn), lambda j: (0, j)),
        ],
        out_specs=pl.BlockSpec((rows, tn), lambda j: (0, j)),
        compiler_params=_params(("arbitrary",)),
        name="adaln",
    )(cond, w_mod, b_mod.reshape(1, n))


def _qkv_kernel(x_ref, sh_ref, sc_ref, g_ref, w_ref, qg_ref, kg_ref, p1_ref, p2_ref, o_ref):
    d = x_ref.shape[-1]
    h = _rms_mod(x_ref[0], g_ref[...], sh_ref[0], sc_ref[0]).astype(BF16)
    for j, gain_ref in ((0, qg_ref), (1, kg_ref)):
        t = jnp.dot(h, w_ref[:, j * d:(j + 1) * d], preferred_element_type=F32)
        ms = jnp.dot((t * t).astype(BF16), p1_ref[...], preferred_element_type=F32)
        r = lax.rsqrt(ms + NORM_EPS)
        r_hi = r.astype(BF16)
        r_lo = (r - r_hi.astype(F32)).astype(BF16)
        rr = jnp.dot(jnp.concatenate([r_hi, r_lo], axis=-1), p2_ref[...], preferred_element_type=F32)
        o_ref[0, :, j * d:(j + 1) * d] = (t * rr * gain_ref[...]).astype(BF16)
    v = jnp.dot(h, w_ref[:, 2 * d:3 * d], preferred_element_type=F32)
    o_ref[0, :, 2 * d:3 * d] = v.astype(BF16)


def _qkv(x, shift, scale, gain, w_qkv, q_gain, k_gain, tm):
    b, l, d = x.shape
    heads = d // HEAD_DIM
    per_batch = shift.shape[0] == b
    mod_map = (lambda bi, i: (bi, 0, 0)) if per_batch else (lambda bi, i: (0, 0, 0))
    p1 = np.zeros((d, LANES), np.float32)
    p1[np.arange(d), np.arange(d) // HEAD_DIM] = 1.0 / HEAD_DIM
    p2 = np.zeros((2 * LANES, d), np.float32)
    p2[np.arange(d) // HEAD_DIM, np.arange(d)] = 1.0
    p2[LANES + np.arange(d) // HEAD_DIM, np.arange(d)] = 1.0
    qg = (jnp.tile(q_gain, heads) * (HEAD_DIM ** -0.5 * LOG2E)).reshape(1, d)
    kg = jnp.tile(k_gain, heads).reshape(1, d)
    return pl.pallas_call(
        _qkv_kernel,
        out_shape=jax.ShapeDtypeStruct((b, l, 3 * d), BF16),
        grid=(b, l // tm),
        in_specs=[
            pl.BlockSpec((1, tm, d), lambda bi, i: (bi, i, 0)),
            pl.BlockSpec((1, 1, d), mod_map),
            pl.BlockSpec((1, 1, d), mod_map),
            _const_spec((1, d)),
            _const_spec((d, 3 * d)),
            _const_spec((1, d)),
            _const_spec((1, d)),
            _const_spec((d, LANES)),
            _const_spec((2 * LANES, d)),
        ],
        out_specs=pl.BlockSpec((1, tm, 3 * d), lambda bi, i: (bi, i, 0)),
        compiler_params=_params(("parallel", "parallel")),
        name="qkv",
    )(x, shift, scale, gain.reshape(1, d), w_qkv.astype(BF16), qg, kg,
      jnp.asarray(p1, BF16), jnp.asarray(p2, BF16))


def _toeplitz_kernel(r_ref, oh_ref, m_ref, o_ref):
    o_ref[...] = jnp.dot(r_ref[...], oh_ref[...], preferred_element_type=F32, precision=HIGHEST) + m_ref[...]


def _col_bias(rpb):
    heads, n_dr, n_dc = rpb.shape
    k_pad = 32
    qc = np.arange(GRID_W)[:, None]
    kc = np.arange(GRID_W)[None, :]
    c0 = np.clip(qc - WIN_W // 2, 0, GRID_W - WIN_W)
    valid = (kc >= c0) & (kc < c0 + WIN_W)
    dc = kc - qc + (WIN_W - 1)
    onehot = np.zeros((k_pad, GRID_W, GRID_W), np.float32)
    for j in range(n_dc):
        onehot[j] = ((dc == j) & valid).astype(np.float32)
    onehot = onehot.reshape(k_pad, GRID_W * GRID_W)
    mask = np.where(valid, 0.0, NEG).astype(np.float32).reshape(1, GRID_W * GRID_W)
    rows = heads * n_dr
    rows_pad = -(-rows // 8) * 8
    r2 = jnp.zeros((rows_pad, k_pad), F32).at[:rows, :n_dc].set(rpb.reshape(rows, n_dc))
    t = pl.pallas_call(
        _toeplitz_kernel,
        out_shape=jax.ShapeDtypeStruct((rows_pad, GRID_W * GRID_W), F32),
        name="rpb_toeplitz",
    )(r2, jnp.asarray(onehot), jnp.asarray(mask))
    return t[:rows].reshape(heads, n_dr, GRID_W, GRID_W)


def _block_bias(t, n_rows):
    heads = t.shape[0]
    kh = min(WIN_H, n_rows)
    masked = jnp.full((heads, GRID_W, GRID_W), NEG, F32)
    classes = []
    for q0, k0 in ((0, 0), (Q_ROWS, Q_ROWS - WIN_H // 2), (n_rows - Q_ROWS, n_rows - K_ROWS)):
        strips = []
        for qr in range(Q_ROWS):
            r = q0 + qr
            r0 = min(max(r - kh // 2, 0), n_rows - kh)
            blocks = []
            for kr in range(K_ROWS):
                kk = k0 + kr
                blocks.append(t[:, kk - r + WIN_H - 1] if r0 <= kk < r0 + kh else masked)
            strips.append(jnp.concatenate(blocks, axis=-1))
        classes.append(jnp.concatenate(strips, axis=-2))
    return (jnp.stack(classes) * LOG2E).astype(BF16)


def _na_stages(q_ref, k_ref, v_ref, kc_ref, vc_ref, bias_ref, o_ref, k_scr, v_scr, p_scr, s_new, s_old,
               *, sub_new, sub_old, n_sub):
    tq = q_ref.shape[1] // NA_SUB
    tk = bias_ref.shape[-1]
    n_ctx = kc_ref.shape[1]
    l = k_ref.shape[1]
    first_head = lax.broadcasted_iota(jnp.int32, (1, LANES), 1) < HEAD_DIM

    def window_start(sub):
        return pl.multiple_of(jnp.clip(sub * tq - (tk - tq) // 2, 0, l - tk), tq)

    for j in range(NA_SUB):
        k_scr[j, :tk] = k_ref[0, pl.ds(window_start(sub_new + j), tk), :]
        k_scr[j, tk:] = kc_ref[0]
        q = q_ref[0, j * tq:(j + 1) * tq]
        zero = jnp.zeros_like(q)
        q2 = jnp.concatenate([jnp.where(first_head, q, zero), jnp.where(first_head, zero, q)], axis=0)
        s_new[j] = lax.dot_general(q2, k_scr[j], (((1,), (1,)), ((), ())), preferred_element_type=F32)

    for j in range(NA_SUB):
        sub = sub_old + j
        v_scr[j, :tk, :LANES] = v_ref[0, pl.ds(window_start(sub), tk), :]
        v_scr[j, tk:, :LANES] = vc_ref[0]
        v_scr[j, :, LANES:] = jnp.ones((tk + n_ctx, LANES), BF16)
        edge = jnp.where(sub == 0, 0, jnp.where(sub == n_sub - 1, 2, 1))
        heads = []
        for h in range(2):
            for r in range(0, tq, NA_CHUNK):
                rows = slice(h * tq + r, h * tq + r + NA_CHUNK)
                s_loc = s_old[j, rows, :tk] + bias_ref[edge, h, r:r + NA_CHUNK, :].astype(F32)
                s_ctx = s_old[j, rows, tk:]
                m = jnp.maximum(jnp.max(s_loc, axis=-1, keepdims=True), jnp.max(s_ctx, axis=-1, keepdims=True))
                p_scr[j, rows, :tk] = jnp.exp2((s_loc - m).astype(BF16))
                p_scr[j, rows, tk:] = jnp.exp2((s_ctx - m).astype(BF16))
            o = jnp.dot(p_scr[j, h * tq:(h + 1) * tq], v_scr[j], preferred_element_type=F32)
            heads.append(o[:, :LANES] / o[:, LANES:])
        o_ref[0, j * tq:(j + 1) * tq] = jnp.where(first_head, heads[0], heads[1]).astype(BF16)


def _na_kernel(q_ref, k_ref, v_ref, kc_ref, vc_ref, bias_ref, o_ref, k_scr, v_scr, p_scr, s0, s1, *, nb):
    t = pl.program_id(0)
    last = pl.num_programs(0) - 2
    subs = dict(sub_new=(jnp.minimum(t, last) % nb) * NA_SUB, sub_old=(jnp.clip(t - 1, 0, last) % nb) * NA_SUB,
                n_sub=nb * NA_SUB)
    refs = (q_ref, k_ref, v_ref, kc_ref, vc_ref, bias_ref, o_ref, k_scr, v_scr, p_scr)

    @pl.when(t == 0)
    def _():
        s1[...] = jnp.zeros(s1.shape, F32)

    @pl.when(t % 2 == 0)
    def _():
        _na_stages(*refs, s0, s1, **subs)

    @pl.when(t % 2 == 1)
    def _():
        _na_stages(*refs, s1, s0, **subs)


def _neighbourhood_attention(qkv, qkv_ctx, bias):
    b, l, d3 = qkv.shape
    d = d3 // 3
    n_ctx = qkv_ctx.shape[1]
    pairs = d // LANES
    tq, tk = bias.shape[-2:]
    tb = NA_SUB * tq
    nb = l // tb
    steps = b * pairs * nb

    def decode(t):
        return t // (pairs * nb), (t // nb) % pairs, t % nb

    def lagged(t, lag):
        return decode(jnp.clip(t - lag, 0, steps - 1))

    def q_map(t):
        bi, hp, i = lagged(t, 0)
        return bi, i, hp

    def kv_map(which, lag):
        def index(t):
            bi, hp, _ = lagged(t, lag)
            return bi, 0, which * pairs + hp
        return index

    def bias_map(t):
        return 0, lagged(t, 1)[1], 0, 0

    def out_map(t):
        bi, hp, i = lagged(t, 1)
        return bi, i, hp

    n_keys = tk + n_ctx
    return pl.pallas_call(
        functools.partial(_na_kernel, nb=nb),
        out_shape=jax.ShapeDtypeStruct((b, l, d), BF16),
        grid=(steps + 1,),
        in_specs=[
            pl.BlockSpec((1, tb, LANES), q_map),
            pl.BlockSpec((1, l, LANES), kv_map(1, 0)),
            pl.BlockSpec((1, l, LANES), kv_map(2, 1)),
            pl.BlockSpec((1, n_ctx, LANES), kv_map(1, 0)),
            pl.BlockSpec((1, n_ctx, LANES), kv_map(2, 1)),
            pl.BlockSpec((3, 2, tq, tk), bias_map),
        ],
        out_specs=pl.BlockSpec((1, tb, LANES), out_map),
        scratch_shapes=[
            pltpu.VMEM((NA_SUB, n_keys, LANES), BF16),
            pltpu.VMEM((NA_SUB, n_keys, 2 * LANES), BF16),
            pltpu.VMEM((NA_SUB, 2 * tq, n_keys), BF16),
            pltpu.VMEM((NA_SUB, 2 * tq, n_keys), F32), pltpu.VMEM((NA_SUB, 2 * tq, n_keys), F32),
        ],
        compiler_params=_params(("arbitrary",)),
        name="na_attention",
    )(qkv, qkv, qkv, qkv_ctx, qkv_ctx, bias)


def _halo_rows(prev_ref, main_ref, next_ref, s=0):
    return jnp.concatenate([prev_ref[s], main_ref[s], next_ref[s]], axis=0)


def _zero_outside(u, axis, halo):
    i = pl.program_id(axis)
    rows = u.shape[0]
    keep_top = jnp.where(i == 0, 0.0, 1.0).astype(F32)
    keep_bottom = jnp.where(i == pl.num_programs(axis) - 1, 0.0, 1.0).astype(F32)
    return jnp.concatenate([u[:halo] * keep_top, u[halo:rows - halo], u[rows - halo:] * keep_bottom], axis=0)


def _shift_rows(g, rows):
    return pltpu.roll(g, 1, 0), pltpu.roll(g, rows - 1, 0)


def _pack_pair(hi, lo):
    hi_bits = pltpu.bitcast(hi.astype(BF16).astype(F32), U32)
    lo_bits = pltpu.bitcast(lo.astype(BF16).astype(F32), U32)
    return hi_bits | (lo_bits >> 16)


def _unpack_pair(word):
    return (pltpu.bitcast(word & jnp.uint32(0xFFFF0000), F32), pltpu.bitcast(word << 16, F32))


def _pitch(n2):
    return n2 + 8


def _packed_rows(prev_ref, main_ref, next_ref, n2):
    pitch = _pitch(n2)
    slabs = main_ref.shape[1] // pitch
    words = jnp.concatenate([prev_ref[0]] + [main_ref[0, j * pitch:j * pitch + n2] for j in range(slabs)]
                            + [next_ref[0]], axis=0)
    shift = ((pl.program_id(0) % 2) * 16).astype(U32)
    return pltpu.bitcast((words << shift) & jnp.uint32(0xFFFF0000), F32)


def _post_kernel(*refs, packed_n2):
    n_mix = 6 if packed_n2 else 3
    mix_refs, refs = refs[:n_mix], refs[n_mix:]
    (xp_ref, xm_ref, xn_ref, wmix_ref, bmix_ref, g1_ref, n2_ref, sh_ref, sc_ref, g2_ref, wup_ref, cw_ref, cb_ref,
     wdn_ref, o_ref) = refs
    halo = xp_ref.shape[1]
    rows = xm_ref.shape[1] + 2 * halo
    d_ff = wdn_ref.shape[0]
    if packed_n2:
        mix = (_packed_rows(*mix_refs[:3], packed_n2) * _packed_rows(*mix_refs[3:], packed_n2)).astype(BF16)
    else:
        mix = _halo_rows(*mix_refs)
    x = _halo_rows(xp_ref, xm_ref, xn_ref)
    y = jnp.dot(mix, wmix_ref[...], preferred_element_type=F32) + bmix_ref[...]
    x1 = x + g1_ref[0] * y
    h32 = _rms_mod(x1, n2_ref[...], sh_ref[0], sc_ref[0])
    h = h32.astype(BF16)
    a = jnp.dot(h32[halo:rows - halo].astype(BF16), wup_ref[:, :d_ff], preferred_element_type=F32)
    g = _zero_outside(jnp.dot(h, wup_ref[:, d_ff:], preferred_element_type=F32), 1, halo)
    g_prev, g_next = _shift_rows(g, rows)
    gc = (g_prev * cw_ref[0:1] + g * cw_ref[1:2] + g_next * cw_ref[2:3] + cb_ref[...])[halo:rows - halo]
    u = a * (0.5 * gc * (1.0 + lax.erf(gc * (2.0 ** -0.5))))
    ffn = jnp.dot(u.astype(BF16), wdn_ref[...], preferred_element_type=F32)
    o_ref[0] = x1[halo:rows - halo] + g2_ref[0] * ffn


def _halo_specs(tm, l, d, halo, nb=1):
    nblk = tm // halo
    last = l // halo - 1
    return [
        pl.BlockSpec((nb, halo, d), lambda bi, i: (bi, jnp.maximum(i * nblk - 1, 0), 0)),
        pl.BlockSpec((nb, tm, d), lambda bi, i: (bi, i, 0)),
        pl.BlockSpec((nb, halo, d), lambda bi, i: (bi, jnp.minimum((i + 1) * nblk, last), 0)),
    ]


def _packed_halo_specs(tm, l, d, n2):
    pitch = _pitch(n2)
    tp = tm // n2 * pitch
    last = l // n2 * pitch // 8 - 1
    return [
        pl.BlockSpec((1, 8, d), lambda bi, i: (bi // 2, jnp.maximum(i * (tp // 8) - 2, 0), 0)),
        pl.BlockSpec((1, tp, d), lambda bi, i: (bi // 2, i, 0)),
        pl.BlockSpec((1, 8, d), lambda bi, i: (bi // 2, jnp.minimum((i + 1) * (tp // 8), last), 0)),
    ]


def _post(mix, x, w_mix, b_mix, g1, norm2, sh2, sc2, g2, w_up, conv_w, conv_b, w_down, tm, packed_n2=0):
    b, l, d = x.shape
    d_ff = w_down.shape[0]
    mod = pl.BlockSpec((1, 1, d), lambda bi, i: (bi, 0, 0))
    halo = 8 if packed_n2 else HALO
    if packed_n2:
        mix_specs = _packed_halo_specs(tm, l, d, packed_n2) * 2
        mix_args = (mix[0],) * 3 + (mix[1],) * 3
    else:
        mix_specs = _halo_specs(tm, l, d, halo)
        mix_args = (mix,) * 3
    return pl.pallas_call(
        functools.partial(_post_kernel, packed_n2=packed_n2),
        out_shape=jax.ShapeDtypeStruct((b, l, d), F32),
        grid=(b, l // tm),
        in_specs=mix_specs + _halo_specs(tm, l, d, halo) + [
            _const_spec((d, d)), _const_spec((1, d)), mod, _const_spec((1, d)), mod, mod, mod,
            _const_spec((d, 2 * d_ff)), _const_spec((3, d_ff)), _const_spec((1, d_ff)), _const_spec((d_ff, d)),
        ],
        out_specs=pl.BlockSpec((1, tm, d), lambda bi, i: (bi, i, 0)),
        compiler_params=_params(("parallel", "parallel")),
        name="post_ffn",
    )(*mix_args, x, x, x, w_mix.astype(BF16), b_mix.reshape(1, d), g1, norm2.reshape(1, d), sh2, sc2, g2,
      w_up.astype(BF16), conv_w, conv_b.reshape(1, d_ff), w_down.astype(BF16))


def _hy_in_kernel(xp_ref, xm_ref, xn_ref, sh_ref, sc_ref, g_ref, w_ref, cw_ref, cb_ref, e0_ref, e2_ref, x0_ref, z_ref,
                  *, n2):
    halo = xp_ref.shape[1]
    tm = xm_ref.shape[1]
    rows = tm + 2 * halo
    d = xm_ref.shape[-1]
    pitch = _pitch(n2)
    i = pl.program_id(1)
    row = lax.broadcasted_iota(jnp.int32, (8, 1), 0)
    at_start = jnp.where((row == 0) & (i == 0), 1.0, 0.0).astype(F32)
    at_end = jnp.where((row == 7) & (i == pl.num_programs(1) - 1), 1.0, 0.0).astype(F32)
    x0s, zs = [], []
    for s in range(2):
        h = _rms_mod(_halo_rows(xp_ref, xm_ref, xn_ref, s), g_ref[...], sh_ref[s], sc_ref[s]).astype(BF16)
        parts = []
        for j in range(3):
            cols = slice(j * d, (j + 1) * d)
            u = _zero_outside(jnp.dot(h, w_ref[:, cols], preferred_element_type=F32), 1, halo)
            u_prev, u_next = _shift_rows(u, rows)
            cw = cw_ref[:, cols]
            uc = (u_prev * cw[0:1] + u * cw[1:2] + u_next * cw[2:3] + cb_ref[:, cols])[halo:rows - halo]
            parts.append(jnp.concatenate([uc[:8] - at_start * e0_ref[:, cols], uc[8:tm - 8],
                                          uc[tm - 8:] - at_end * e2_ref[:, cols]], axis=0))
        x0s.append(parts[0])
        zs.append(parts[2] * parts[1])
    pad = jnp.zeros((pitch - n2, d), U32)
    for ref, pair in ((x0_ref, x0s), (z_ref, zs)):
        words = _pack_pair(pair[0], pair[1])
        for j in range(tm // n2):
            ref[0, j * pitch:j * pitch + n2] = words[j * n2:(j + 1) * n2]
            ref[0, j * pitch + n2:(j + 1) * pitch] = pad


def _hy_in(x, shift, scale, gain, w_in, b_in, short_w, short_b, tm, n2):
    b, l, d = x.shape
    pitch = _pitch(n2)
    tp = tm // n2 * pitch
    mod = pl.BlockSpec((2, 1, d), lambda pi, i: (pi, 0, 0))
    out = pl.BlockSpec((1, tp, d), lambda pi, i: (pi, i, 0))
    shape = jax.ShapeDtypeStruct((b // 2, l // n2 * pitch, d), U32)
    return pl.pallas_call(
        functools.partial(_hy_in_kernel, n2=n2),
        out_shape=(shape, shape),
        grid=(b // 2, l // tm),
        in_specs=_halo_specs(tm, l, d, 8, 2) + [
            mod, mod, _const_spec((1, d)), _const_spec((d, 3 * d)), _const_spec((3, 3 * d)),
            _const_spec((1, 3 * d)), _const_spec((1, 3 * d)), _const_spec((1, 3 * d)),
        ],
        out_specs=(out, out),
        compiler_params=_params(("parallel", "parallel")),
        name="hyena_in",
    )(x, x, x, shift, scale, gain.reshape(1, d), w_in.astype(BF16), short_w,
      (short_b + b_in * short_w.sum(axis=0)).reshape(1, 3 * d), (b_in * short_w[0]).reshape(1, 3 * d),
      (b_in * short_w[2]).reshape(1, 3 * d))


FILTER_GROUPS = 8


def _filter_kernel(bands_ref, w1t_ref, w1c_ref, w1s_ref, b1_ref, w2_ref, b2_ref, w3_ref, b3_ref, wo_ref,
                   freq_ref, delta_ref, hf_ref, hb_ref, ss_ref, *, seq_len, n2):
    pitch = _pitch(n2)
    tl = hf_ref.shape[0] // pitch * n2
    d = hf_ref.shape[1]
    groups = FILTER_GROUPS
    rows = tl // groups
    width = b1_ref.shape[1] // groups
    base = pl.program_id(0) * tl

    def positions(lanes_per_group, n_lanes):
        lane = lax.broadcasted_iota(jnp.int32, (rows, n_lanes), 1)
        row = lax.broadcasted_iota(jnp.int32, (rows, n_lanes), 0)
        return (base + (lane // lanes_per_group) * rows + row).astype(F32)

    ang = bands_ref[...] * ((2.0 * math.pi / seq_len) * positions(HY_BANDS, groups * HY_BANDS))
    t_wide = positions(width, groups * width) * (1.0 / (seq_len - 1))
    freq = freq_ref[...]
    pre = t_wide * w1t_ref[...] + _dot3(jnp.cos(ang), w1c_ref[...]) - _dot3(jnp.sin(ang), w1s_ref[...])
    hdn = jnp.sin(freq * (pre + b1_ref[...]))
    hdn = jnp.sin(freq * (_dot3(hdn, w2_ref[...]) + b2_ref[...]))
    hdn = jnp.sin(freq * (_dot3(hdn, w3_ref[...]) + b3_ref[...]))

    pad = jnp.zeros((pitch - n2, d), F32)
    energy = jnp.zeros((1, d), F32)
    col = lax.broadcasted_iota(jnp.int32, (rows, 1), 0)
    for q in range(groups // 2):
        h2 = _dot3(hdn[:, 2 * q * width:(2 * q + 2) * width], wo_ref[...])
        for e in range(2):
            p = 2 * q + e
            pos = (base + p * rows + col).astype(F32)
            decay = jnp.exp(-(pos * (1.0 / (seq_len - 1))) * delta_ref[...])
            hf = h2[:, e * 2 * d:e * 2 * d + d] * decay
            hb = h2[:, e * 2 * d + d:(e + 1) * 2 * d] * decay
            energy = energy + jnp.sum(hf * hf + hb * hb, axis=0, keepdims=True)
            for ref, taps in ((hf_ref, hf), (hb_ref, jnp.where(pos > 0.0, hb, 0.0))):
                for j in range(rows // n2):
                    slab = p * (rows // n2) + j
                    ref[slab * pitch:slab * pitch + n2] = taps[j * n2:(j + 1) * n2]
                    ref[slab * pitch + n2:(slab + 1) * pitch] = pad

    @pl.when(pl.program_id(0) == 0)
    def _():
        ss_ref[...] = jnp.zeros_like(ss_ref)

    ss_ref[...] += energy


def _hyena_filter(seq_len, d, w1, b1, w2, b2, w3, b3, w_out, freq, tl, n2):
    groups = FILTER_GROUPS
    assert (tl // groups) % n2 == 0
    tp = tl // n2 * _pitch(n2)
    padded = jax.ShapeDtypeStruct((seq_len // n2 * _pitch(n2), d), F32)
    bands = np.linspace(1e-4, HY_BANDS - 1, HY_BANDS, dtype=np.float32)
    deltas = np.abs(np.linspace(math.log(HY_DECAY_TARGET) / HY_SHORT_DECAY_PCT,
                                math.log(HY_DECAY_TARGET) / HY_LONG_DECAY_PCT, d, dtype=np.float32)).reshape(1, d)
    eye = jnp.eye(groups, dtype=F32)
    tile = lambda v: jnp.tile(v.reshape(1, -1), (1, groups))
    operands = [
        jnp.asarray(np.tile(bands, groups).reshape(1, -1)), tile(w1[0]), jnp.kron(eye, w1[1:1 + HY_BANDS]),
        jnp.kron(eye, w1[1 + HY_BANDS:]), tile(b1), jnp.kron(eye, w2), tile(b2), jnp.kron(eye, w3), tile(b3),
        jnp.kron(jnp.eye(2, dtype=F32), w_out), tile(freq), jnp.asarray(deltas),
    ]
    return pl.pallas_call(
        functools.partial(_filter_kernel, seq_len=seq_len, n2=n2),
        out_shape=(padded, padded, jax.ShapeDtypeStruct((1, d), F32)),
        grid=(seq_len // tl,),
        in_specs=[pl.BlockSpec(o.shape, lambda i: (0, 0)) for o in operands],
        out_specs=(pl.BlockSpec((tp, d), lambda i: (i, 0)), pl.BlockSpec((tp, d), lambda i: (i, 0)),
                   pl.BlockSpec((1, d), lambda i: (0, 0))),
        compiler_params=_params(("arbitrary",)),
        name="hyena_filter",
    )(*operands)


def _dft_tables(n1):
    n = n1 * n1
    half = n1 // 2
    idx = np.arange(n1)
    ang = 2.0 * np.pi * np.outer(idx, idx) / n1
    c, s = np.cos(ang), np.sin(ang)
    fa_pair = np.block([[c[:, :half], s[:, :half]], [-s[:, :half], c[:, :half]]])
    zero = np.zeros((n1, half))
    fa_real = np.block([[c[:, :half], zero], [-s[:, :half], zero], [zero, c[:, :half]], [zero, -s[:, :half]]])
    fa_inv = np.block([[c[:half], -s[:half]], [s[:half], c[:half]]]) / n
    k1 = idx[:, None, None]
    k2 = idx[None, :, None]
    n2 = idx[None, None, :]
    m = (n2 * (k1 + n1 * k2)) % n
    gang = 2.0 * np.pi * m / n
    gr, gi = np.cos(gang), -np.sin(gang)
    g_fwd = np.concatenate([np.concatenate([gr, -gi], axis=2), np.concatenate([gi, gr], axis=2)], axis=1)
    to = lambda a: jnp.asarray(a.astype(np.float32), BF16)
    return to(fa_pair), to(fa_real), to(fa_inv), to(g_fwd)


def _filter_spectrum_kernel(hf_ref, hb_ref, ss_ref, db_ref, fa_ref, g_ref, o_ref, a_scr):
    n1 = fa_ref.shape[0] // 4
    n2 = g_ref.shape[1] // 2
    pitch = _pitch(n2)
    half = hf_ref.shape[0] // pitch
    kb = g_ref.shape[0]

    @pl.when(pl.program_id(1) == 0)
    def _():
        def body(j, carry):
            rows = jnp.concatenate([hf_ref[pl.ds(j, half, stride=pitch), :], hb_ref[pl.ds(j, half, stride=pitch), :]],
                                   axis=0)
            r = jnp.dot(fa_ref[...], rows.astype(BF16), preferred_element_type=F32)
            a_scr[0, pl.ds(j, n1, stride=pitch), :] = _pack_pair(r[:n1], r[n1:2 * n1])
            a_scr[1, pl.ds(j, n1, stride=pitch), :] = _pack_pair(r[2 * n1:3 * n1], r[3 * n1:])
            return carry

        lax.fori_loop(0, n2, body, 0, unroll=DFT_UNROLL)

    scale = lax.rsqrt(ss_ref[...] + NORM_EPS)
    for k in range(kb):
        row0 = pl.multiple_of((pl.program_id(1) * kb + k) * pitch, 8)
        fr, fi = _unpack_pair(a_scr[0, pl.ds(row0, n2), :])
        br, bi = _unpack_pair(a_scr[1, pl.ds(row0, n2), :])
        a = jnp.concatenate([jnp.concatenate([fr, fi], axis=0), jnp.concatenate([br, bi], axis=0)], axis=1)
        p = jnp.dot(g_ref[k], a.astype(BF16), preferred_element_type=F32)
        lanes = p.shape[1] // 2
        o_ref[k] = _pack_pair((p[:n2, :lanes] + p[:n2, lanes:]) * scale + db_ref[...],
                              (p[n2:, :lanes] - p[n2:, lanes:]) * scale)


def _filter_spectrum(fa_filt, g_fwd, hf, hb, energy, d_bias, kb):
    n1 = g_fwd.shape[0]
    n2 = g_fwd.shape[1] // 2
    lp, d = hf.shape
    tc = LANES
    half_spec = pl.BlockSpec((lp, tc), lambda c, k: (0, c))
    return pl.pallas_call(
        _filter_spectrum_kernel,
        out_shape=jax.ShapeDtypeStruct((n1, n2, d), U32),
        grid=(d // tc, n1 // kb),
        in_specs=[
            half_spec, half_spec, pl.BlockSpec((1, tc), lambda c, k: (0, c)), pl.BlockSpec((1, tc), lambda c, k: (0, c)),
            _const_spec(fa_filt.shape),
            pl.BlockSpec((kb, 2 * n2, 2 * n2), lambda c, k: (k, 0, 0)),
        ],
        out_specs=pl.BlockSpec((kb, n2, tc), lambda c, k: (k, 0, c)),
        scratch_shapes=[pltpu.VMEM((2, n1 * _pitch(n2), tc), U32)],
        compiler_params=_params(("parallel", "arbitrary")),
        name="filter_spectrum",
    )(hf, hb, energy, d_bias.reshape(1, d), fa_filt, g_fwd)


def _long_conv_kernel(z_ref, hf_ref, gf_ref, fa_ref, fi_ref, o_ref, a_scr):
    n1 = fa_ref.shape[0] // 2
    n2 = gf_ref.shape[1] // 2
    half = n1 // 2
    pitch = _pitch(n2)
    kb = gf_ref.shape[0]
    step = pl.program_id(2)

    @pl.when(step == 0)
    def _():
        def body(j, carry):
            zr, zi = _unpack_pair(z_ref[0, pl.ds(j, half, stride=pitch), :])
            rows = jnp.concatenate([zr, zi], axis=0).astype(BF16)
            r = jnp.dot(fa_ref[...], rows, preferred_element_type=F32)
            a_scr[pl.ds(j, n1, stride=pitch), :] = _pack_pair(r[:n1], r[n1:])
            return carry

        lax.fori_loop(0, n2, body, 0, unroll=DFT_UNROLL)

    for k in range(kb):
        row0 = pl.multiple_of((step * kb + k) * pitch, 8)
        ar, ai = _unpack_pair(a_scr[pl.ds(row0, n2), :])
        x = jnp.dot(gf_ref[k], jnp.concatenate([ar, ai], axis=0).astype(BF16), preferred_element_type=F32)
        xr, xi = x[:n2], x[n2:]
        hr, hi = _unpack_pair(hf_ref[k])
        y = jnp.concatenate([xr * hr - xi * hi, xr * hi + xi * hr], axis=0).astype(BF16)
        t = lax.dot_general(gf_ref[k], y, (((0,), (0,)), ((), ())), preferred_element_type=F32)
        a_scr[pl.ds(row0, n2), :] = _pack_pair(t[:n2], t[n2:])

    @pl.when(step == pl.num_programs(2) - 1)
    def _():
        def body(j, carry):
            tr, ti = _unpack_pair(a_scr[pl.ds(j, n1, stride=pitch), :])
            y = jnp.dot(fi_ref[...], jnp.concatenate([tr, ti], axis=0).astype(BF16), preferred_element_type=F32)
            o_ref[0, pl.ds(j, half, stride=pitch), :] = _pack_pair(y[:half], y[half:])
            return carry

        lax.fori_loop(0, n2, body, 0, unroll=DFT_UNROLL_OUT)
        pad = jnp.zeros((pitch - n2, o_ref.shape[2]), U32)
        for s in range(half):
            o_ref[0, s * pitch + n2:(s + 1) * pitch] = pad


def _long_conv(zp, hf, hb, energy, d_bias, l):
    p, lp, d = zp.shape
    n1 = math.isqrt(2 * l)
    assert n1 * n1 == 2 * l and lp == l // n1 * _pitch(n1)
    kb = min(64, n1)
    tc = LANES
    fa_pair, fa_real, fa_inv, g_fwd = _dft_tables(n1)
    spec = _filter_spectrum(fa_real, g_fwd, hf, hb, energy, d_bias, min(32, n1))
    seq = pl.BlockSpec((1, lp, tc), lambda c, pi, k: (pi, 0, c))
    g_blk = pl.BlockSpec((kb, 2 * n1, 2 * n1), lambda c, pi, k: (k, 0, 0))
    return pl.pallas_call(
        _long_conv_kernel,
        out_shape=jax.ShapeDtypeStruct((p, lp, d), U32),
        grid=(d // tc, p, n1 // kb),
        in_specs=[
            seq, pl.BlockSpec((kb, n1, tc), lambda c, pi, k: (k, 0, c)), g_blk,
            _const_spec(fa_pair.shape), _const_spec(fa_inv.shape),
        ],
        out_specs=seq,
        scratch_shapes=[pltpu.VMEM((n1 * _pitch(n1), tc), U32)],
        compiler_params=_params(("parallel", "parallel", "arbitrary")),
        name="long_conv",
    )(zp, spec, g_fwd, fa_pair, fa_inv)


def _mod_rows(mod, lo, hi, d):
    m = mod[lo:hi]
    return [m[:, None, j * d:(j + 1) * d] for j in range(N_MOD)]


def kernel(x, c, ctx, c_ctx, l0_w_mod, l0_b_mod, l0_norm1, l0_norm2, l0_na_w_qkv, l0_na_q_gain, l0_na_k_gain, l0_na_rpb, l0_na_w_o, l0_ffn_w_up, l0_ffn_conv_w, l0_ffn_conv_b, l0_ffn_w_down, l1_w_mod, l1_b_mod, l1_norm1, l1_norm2, l1_hy_w_in, l1_hy_b_in, l1_hy_short_w, l1_hy_short_b, l1_hy_f_w1, l1_hy_f_b1, l1_hy_f_w2, l1_hy_f_b2, l1_hy_f_w3, l1_hy_f_b3, l1_hy_f_wout, l1_hy_f_freq, l1_hy_d_bias, l1_hy_w_out, l1_hy_b_out, l1_ffn_w_up, l1_ffn_conv_w, l1_ffn_conv_b, l1_ffn_w_down):
    b, l, d = x.shape
    n_ctx = ctx.shape[1]
    n_rows = l // GRID_W
    assert n_rows >= K_ROWS and n_rows % (Q_ROWS * NA_SUB) == 0 and b % 2 == 0
    tm = min(512, l)
    tm_qkv = min(1024, l)
    n_dft = math.isqrt(2 * l)

    cond = jnp.zeros((8, d), F32).at[:b].set(c).at[b].set(c_ctx)

    mod = _adaln(cond, l0_w_mod, l0_b_mod)
    sh1, sc1, g1, sh2, sc2, g2 = _mod_rows(mod, 0, b, d)
    csh1, csc1 = _mod_rows(mod, b, b + 1, d)[:2]
    qkv = _qkv(x, sh1, sc1, l0_norm1, l0_na_w_qkv, l0_na_q_gain, l0_na_k_gain, tm_qkv)
    qkv_ctx = _qkv(ctx, csh1, csc1, l0_norm1, l0_na_w_qkv, l0_na_q_gain, l0_na_k_gain, n_ctx)
    bias = _block_bias(_col_bias(l0_na_rpb), n_rows)
    attn = _neighbourhood_attention(qkv, qkv_ctx, bias)
    x = _post(attn, x, l0_na_w_o, jnp.zeros((d,), F32), g1, l0_norm2, sh2, sc2, g2,
              l0_ffn_w_up, l0_ffn_conv_w, l0_ffn_conv_b, l0_ffn_w_down, tm)

    mod = _adaln(cond, l1_w_mod, l1_b_mod)
    sh1, sc1, g1, sh2, sc2, g2 = _mod_rows(mod, 0, b, d)
    x0p, zp = _hy_in(x, sh1, sc1, l1_norm1, l1_hy_w_in, l1_hy_b_in, l1_hy_short_w, l1_hy_short_b, tm, n_dft)
    hf, hb, energy = _hyena_filter(l, d, l1_hy_f_w1, l1_hy_f_b1, l1_hy_f_w2, l1_hy_f_b2, l1_hy_f_w3, l1_hy_f_b3,
                                   l1_hy_f_wout, l1_hy_f_freq, min(1024, l), n_dft)
    conv = _long_conv(zp, hf, hb, energy, l1_hy_d_bias, l)
    x = _post((conv, x0p), x, l1_hy_w_out, l1_hy_b_out, g1, l1_norm2, sh2, sc2, g2,
              l1_ffn_w_up, l1_ffn_conv_w, l1_ffn_conv_b, l1_ffn_w_down, tm, packed_n2=n_dft)
    return x
```

```python
import functools
import math

import numpy as np
import jax
import jax.numpy as jnp
from jax import lax
from jax.experimental import pallas as pl
from jax.experimental.pallas import tpu as pltpu

F32 = jnp.float32
BF16 = jnp.bfloat16
U32 = jnp.uint32
HIGHEST = lax.Precision.HIGHEST

NORM_EPS = 1e-6
N_MOD = 6
HEAD_DIM = 64
GRID_W = 64
WIN_H = 8
WIN_W = 16
HY_BANDS = 16
HY_DECAY_TARGET = 1e-2
HY_SHORT_DECAY_PCT = 0.3
HY_LONG_DECAY_PCT = 1.5

LANES = 128
HALO = 16
Q_ROWS = 4
K_ROWS = Q_ROWS + WIN_H
NA_SUB = 4
NA_CHUNK = 32
DFT_UNROLL = 16
DFT_UNROLL_OUT = 16
LOG2E = math.log2(math.e)
NEG = -1e30
VMEM_LIMIT = 56 * 1024 * 1024


def _params(sem):
    return pltpu.CompilerParams(dimension_semantics=sem, vmem_limit_bytes=VMEM_LIMIT)


def _const_spec(shape):
    return pl.BlockSpec(shape, lambda *_: (0,) * len(shape), pipeline_mode=pl.Buffered(1))


def _rms_mod(x, gain, shift, scale):
    ms = jnp.mean(x * x, axis=-1, keepdims=True)
    y = x * lax.rsqrt(ms + NORM_EPS) * gain
    return y * (1.0 + scale) + shift


def _split_bf16(x):
    hi = x.astype(BF16)
    return hi, (x - hi.astype(F32)).astype(BF16)


def _dot3(x, w):
    x_hi, x_lo = _split_bf16(x)
    w_hi, w_lo = _split_bf16(w)
    return jnp.dot(jnp.concatenate([x_hi, x_lo, x_hi], axis=1), jnp.concatenate([w_hi, w_hi, w_lo], axis=0),
                   preferred_element_type=F32)


def _adaln_kernel(c_ref, w_ref, b_ref, o_ref):
    c = c_ref[...]
    s = c / (1.0 + jnp.exp(-c))
    o_ref[...] = _dot3(s, w_ref[...]) + b_ref[...]


def _adaln(cond, w_mod, b_mod):
    rows, d = cond.shape
    n = w_mod.shape[1]
    tn = d
    return pl.pallas_call(
        _adaln_kernel,
        out_shape=jax.ShapeDtypeStruct((rows, n), F32),
        grid=(n // tn,),
        in_specs=[
            pl.BlockSpec((rows, d), lambda j: (0, 0)),
            pl.BlockSpec((d, tn), lambda j: (0, j)),
            pl.BlockSpec((1, tn), lambda j: (0, j)),
        ],
        out_specs=pl.BlockSpec((rows, tn), lambda j: (0, j)),
        compiler_params=_params(("arbitrary",)),
        name="adaln",
    )(cond, w_mod, b_mod.reshape(1, n))


def _qkv_kernel(x_ref, sh_ref, sc_ref, g_ref, w_ref, qg_ref, kg_ref, p1_ref, p2_ref, o_ref):
    d = x_ref.shape[-1]
    h = _rms_mod(x_ref[0], g_ref[...], sh_ref[0], sc_ref[0]).astype(BF16)
    for j, gain_ref in ((0, qg_ref), (1, kg_ref)):
        t = jnp.dot(h, w_ref[:, j * d:(j + 1) * d], preferred_element_type=F32)
        ms = jnp.dot((t * t).astype(BF16), p1_ref[...], preferred_element_type=F32)
        r = lax.rsqrt(ms + NORM_EPS)
        r_hi = r.astype(BF16)
        r_lo = (r - r_hi.astype(F32)).astype(BF16)
        rr = jnp.dot(jnp.concatenate([r_hi, r_lo], axis=-1), p2_ref[...], preferred_element_type=F32)
        o_ref[0, :, j * d:(j + 1) * d] = (t * rr * gain_ref[...]).astype(BF16)
    v = jnp.dot(h, w_ref[:, 2 * d:3 * d], preferred_element_type=F32)
    o_ref[0, :, 2 * d:3 * d] = v.astype(BF16)


def _qkv(x, shift, scale, gain, w_qkv, q_gain, k_gain, tm):
    b, l, d = x.shape
    heads = d // HEAD_DIM
    per_batch = shift.shape[0] == b
    mod_map = (lambda bi, i: (bi, 0, 0)) if per_batch else (lambda bi, i: (0, 0, 0))
    p1 = np.zeros((d, LANES), np.float32)
    p1[np.arange(d), np.arange(d) // HEAD_DIM] = 1.0 / HEAD_DIM
    p2 = np.zeros((2 * LANES, d), np.float32)
    p2[np.arange(d) // HEAD_DIM, np.arange(d)] = 1.0
    p2[LANES + np.arange(d) // HEAD_DIM, np.arange(d)] = 1.0
    qg = (jnp.tile(q_gain, heads) * (HEAD_DIM ** -0.5 * LOG2E)).reshape(1, d)
    kg = jnp.tile(k_gain, heads).reshape(1, d)
    return pl.pallas_call(
        _qkv_kernel,
        out_shape=jax.ShapeDtypeStruct((b, l, 3 * d), BF16),
        grid=(b, l // tm),
        in_specs=[
            pl.BlockSpec((1, tm, d), lambda bi, i: (bi, i, 0)),
            pl.BlockSpec((1, 1, d), mod_map),
            pl.BlockSpec((1, 1, d), mod_map),
            _const_spec((1, d)),
            _const_spec((d, 3 * d)),
            _const_spec((1, d)),
            _const_spec((1, d)),
            _const_spec((d, LANES)),
            _const_spec((2 * LANES, d)),
        ],
        out_specs=pl.BlockSpec((1, tm, 3 * d), lambda bi, i: (bi, i, 0)),
        compiler_params=_params(("parallel", "parallel")),
        name="qkv",
    )(x, shift, scale, gain.reshape(1, d), w_qkv.astype(BF16), qg, kg,
      jnp.asarray(p1, BF16), jnp.asarray(p2, BF16))


def _toeplitz_kernel(r_ref, oh_ref, m_ref, o_ref):
    o_ref[...] = jnp.dot(r_ref[...], oh_ref[...], preferred_element_type=F32, precision=HIGHEST) + m_ref[...]


def _col_bias(rpb):
    heads, n_dr, n_dc = rpb.shape
    k_pad = 32
    qc = np.arange(GRID_W)[:, None]
    kc = np.arange(GRID_W)[None, :]
    c0 = np.clip(qc - WIN_W // 2, 0, GRID_W - WIN_W)
    valid = (kc >= c0) & (kc < c0 + WIN_W)
    dc = kc - qc + (WIN_W - 1)
    onehot = np.zeros((k_pad, GRID_W, GRID_W), np.float32)
    for j in range(n_dc):
        onehot[j] = ((dc == j) & valid).astype(np.float32)
    onehot = onehot.reshape(k_pad, GRID_W * GRID_W)
    mask = np.where(valid, 0.0, NEG).astype(np.float32).reshape(1, GRID_W * GRID_W)
    rows = heads * n_dr
    rows_pad = -(-rows // 8) * 8
    r2 = jnp.zeros((rows_pad, k_pad), F32).at[:rows, :n_dc].set(rpb.reshape(rows, n_dc))
    t = pl.pallas_call(
        _toeplitz_kernel,
        out_shape=jax.ShapeDtypeStruct((rows_pad, GRID_W * GRID_W), F32),
        name="rpb_toeplitz",
    )(r2, jnp.asarray(onehot), jnp.asarray(mask))
    return t[:rows].reshape(heads, n_dr, GRID_W, GRID_W)


def _block_bias(t, n_rows):
    heads = t.shape[0]
    kh = min(WIN_H, n_rows)
    masked = jnp.full((heads, GRID_W, GRID_W), NEG, F32)
    classes = []
    for q0, k0 in ((0, 0), (Q_ROWS, Q_ROWS - WIN_H // 2), (n_rows - Q_ROWS, n_rows - K_ROWS)):
        strips = []
        for qr in range(Q_ROWS):
            r = q0 + qr
            r0 = min(max(r - kh // 2, 0), n_rows - kh)
            blocks = []
            for kr in range(K_ROWS):
                kk = k0 + kr
                blocks.append(t[:, kk - r + WIN_H - 1] if r0 <= kk < r0 + kh else masked)
            strips.append(jnp.concatenate(blocks, axis=-1))
        classes.append(jnp.concatenate(strips, axis=-2))
    return (jnp.stack(classes) * LOG2E).astype(BF16)


def _na_stages(q_ref, k_ref, v_ref, kc_ref, vc_ref, bias_ref, o_ref, k_scr, v_scr, p_scr, s_new, s_old,
               *, sub_new, sub_old, n_sub):
    tq = q_ref.shape[1] // NA_SUB
    tk = bias_ref.shape[-1]
    n_ctx = kc_ref.shape[1]
    l = k_ref.shape[1]
    first_head = lax.broadcasted_iota(jnp.int32, (1, LANES), 1) < HEAD_DIM

    def window_start(sub):
        return pl.multiple_of(jnp.clip(sub * tq - (tk - tq) // 2, 0, l - tk), tq)

    anchor = None
    for j in range(NA_SUB):
        k_scr[j, :tk] = k_ref[0, pl.ds(window_start(sub_new + j), tk), :]
        k_scr[j, tk:] = kc_ref[0]
        q = q_ref[0, j * tq:(j + 1) * tq]
        zero = jnp.zeros_like(q)
        if anchor is not None:
            bits = (pltpu.bitcast(anchor, U32) >> 16) >> 16
            zero = jnp.concatenate([pltpu.bitcast(bits, F32).astype(BF16), zero[anchor.shape[0]:]], axis=0)
        q2 = jnp.concatenate([jnp.where(first_head, q, zero), jnp.where(first_head, zero, q)], axis=0)
        s_new[j] = lax.dot_general(q2, k_scr[j], (((1,), (1,)), ((), ())), preferred_element_type=F32)

        sub = sub_old + j
        v_scr[j, :tk, :LANES] = v_ref[0, pl.ds(window_start(sub), tk), :]
        v_scr[j, tk:, :LANES] = vc_ref[0]
        v_scr[j, :, LANES:] = jnp.ones((tk + n_ctx, LANES), BF16)
        edge = jnp.where(sub == 0, 0, jnp.where(sub == n_sub - 1, 2, 1))
        heads = []
        for h in range(2):
            for r in range(0, tq, NA_CHUNK):
                rows = slice(h * tq + r, h * tq + r + NA_CHUNK)
                s_loc = s_old[j, rows, :tk] + bias_ref[edge, h, r:r + NA_CHUNK, :].astype(F32)
                s_ctx = s_old[j, rows, tk:]
                m = jnp.maximum(jnp.max(s_loc, axis=-1, keepdims=True), jnp.max(s_ctx, axis=-1, keepdims=True))
                p_scr[j, rows, :tk] = jnp.exp2((s_loc - m).astype(BF16))
                p_scr[j, rows, tk:] = jnp.exp2((s_ctx - m).astype(BF16))
            o = jnp.dot(p_scr[j, h * tq:(h + 1) * tq], v_scr[j], preferred_element_type=F32)
            heads.append(o[:, :LANES] / o[:, LANES:])
            anchor = o[:16, :LANES]
        o_ref[0, j * tq:(j + 1) * tq] = jnp.where(first_head, heads[0], heads[1]).astype(BF16)


def _na_kernel(q_ref, k_ref, v_ref, kc_ref, vc_ref, bias_ref, o_ref, k_scr, v_scr, p_scr, s0, s1, *, nb):
    t = pl.program_id(0)
    last = pl.num_programs(0) - 2
    subs = dict(sub_new=(jnp.minimum(t, last) % nb) * NA_SUB, sub_old=(jnp.clip(t - 1, 0, last) % nb) * NA_SUB,
                n_sub=nb * NA_SUB)
    refs = (q_ref, k_ref, v_ref, kc_ref, vc_ref, bias_ref, o_ref, k_scr, v_scr, p_scr)

    @pl.when(t == 0)
    def _():
        s1[...] = jnp.zeros(s1.shape, F32)

    @pl.when(t % 2 == 0)
    def _():
        _na_stages(*refs, s0, s1, **subs)

    @pl.when(t % 2 == 1)
    def _():
        _na_stages(*refs, s1, s0, **subs)


def _neighbourhood_attention(qkv, qkv_ctx, bias):
    b, l, d3 = qkv.shape
    d = d3 // 3
    n_ctx = qkv_ctx.shape[1]
    pairs = d // LANES
    tq, tk = bias.shape[-2:]
    tb = NA_SUB * tq
    nb = l // tb
    steps = b * pairs * nb

    def decode(t):
        return t // (pairs * nb), (t // nb) % pairs, t % nb

    def lagged(t, lag):
        return decode(jnp.clip(t - lag, 0, steps - 1))

    def q_map(t):
        bi, hp, i = lagged(t, 0)
        return bi, i, hp

    def kv_map(which, lag):
        def index(t):
            bi, hp, _ = lagged(t, lag)
            return bi, 0, which * pairs + hp
        return index

    def bias_map(t):
        return 0, lagged(t, 1)[1], 0, 0

    def out_map(t):
        bi, hp, i = lagged(t, 1)
        return bi, i, hp

    n_keys = tk + n_ctx
    return pl.pallas_call(
        functools.partial(_na_kernel, nb=nb),
        out_shape=jax.ShapeDtypeStruct((b, l, d), BF16),
        grid=(steps + 1,),
        in_specs=[
            pl.BlockSpec((1, tb, LANES), q_map),
            pl.BlockSpec((1, l, LANES), kv_map(1, 0)),
            pl.BlockSpec((1, l, LANES), kv_map(2, 1)),
            pl.BlockSpec((1, n_ctx, LANES), kv_map(1, 0)),
            pl.BlockSpec((1, n_ctx, LANES), kv_map(2, 1)),
            pl.BlockSpec((3, 2, tq, tk), bias_map),
        ],
        out_specs=pl.BlockSpec((1, tb, LANES), out_map),
        scratch_shapes=[
            pltpu.VMEM((NA_SUB, n_keys, LANES), BF16),
            pltpu.VMEM((NA_SUB, n_keys, 2 * LANES), BF16),
            pltpu.VMEM((NA_SUB, 2 * tq, n_keys), BF16),
            pltpu.VMEM((NA_SUB, 2 * tq, n_keys), F32), pltpu.VMEM((NA_SUB, 2 * tq, n_keys), F32),
        ],
        compiler_params=_params(("arbitrary",)),
        name="na_attention",
    )(qkv, qkv, qkv, qkv_ctx, qkv_ctx, bias)


def _halo_rows(prev_ref, main_ref, next_ref, s=0):
    return jnp.concatenate([prev_ref[s], main_ref[s], next_ref[s]], axis=0)


def _zero_outside(u, axis, halo):
    i = pl.program_id(axis)
    rows = u.shape[0]
    keep_top = jnp.where(i == 0, 0.0, 1.0).astype(F32)
    keep_bottom = jnp.where(i == pl.num_programs(axis) - 1, 0.0, 1.0).astype(F32)
    return jnp.concatenate([u[:halo] * keep_top, u[halo:rows - halo], u[rows - halo:] * keep_bottom], axis=0)


def _shift_rows(g, rows):
    return pltpu.roll(g, 1, 0), pltpu.roll(g, rows - 1, 0)


def _pack_pair(hi, lo):
    hi_bits = pltpu.bitcast(hi.astype(BF16).astype(F32), U32)
    lo_bits = pltpu.bitcast(lo.astype(BF16).astype(F32), U32)
    return hi_bits | (lo_bits >> 16)


def _unpack_pair(word):
    return (pltpu.bitcast(word & jnp.uint32(0xFFFF0000), F32), pltpu.bitcast(word << 16, F32))


def _pitch(n2):
    return n2 + 8


def _packed_rows(prev_ref, main_ref, next_ref, n2):
    pitch = _pitch(n2)
    slabs = main_ref.shape[1] // pitch
    words = jnp.concatenate([prev_ref[0]] + [main_ref[0, j * pitch:j * pitch + n2] for j in range(slabs)]
                            + [next_ref[0]], axis=0)
    shift = ((pl.program_id(0) % 2) * 16).astype(U32)
    return pltpu.bitcast((words << shift) & jnp.uint32(0xFFFF0000), F32)


def _post_kernel(*refs, packed_n2):
    n_mix = 6 if packed_n2 else 3
    mix_refs, refs = refs[:n_mix], refs[n_mix:]
    (xp_ref, xm_ref, xn_ref, wmix_ref, bmix_ref, g1_ref, n2_ref, sh_ref, sc_ref, g2_ref, wup_ref, cw_ref, cb_ref,
     wdn_ref, o_ref) = refs
    halo = xp_ref.shape[1]
    rows = xm_ref.shape[1] + 2 * halo
    d_ff = wdn_ref.shape[0]
    if packed_n2:
        mix = (_packed_rows(*mix_refs[:3], packed_n2) * _packed_rows(*mix_refs[3:], packed_n2)).astype(BF16)
    else:
        mix = _halo_rows(*mix_refs)
    x = _halo_rows(xp_ref, xm_ref, xn_ref)
    y = jnp.dot(mix, wmix_ref[...], preferred_element_type=F32) + bmix_ref[...]
    x1 = x + g1_ref[0] * y
    h32 = _rms_mod(x1, n2_ref[...], sh_ref[0], sc_ref[0])
    h = h32.astype(BF16)
    a = jnp.dot(h32[halo:rows - halo].astype(BF16), wup_ref[:, :d_ff], preferred_element_type=F32)
    g = _zero_outside(jnp.dot(h, wup_ref[:, d_ff:], preferred_element_type=F32), 1, halo)
    g_prev, g_next = _shift_rows(g, rows)
    gc = (g_prev * cw_ref[0:1] + g * cw_ref[1:2] + g_next * cw_ref[2:3] + cb_ref[...])[halo:rows - halo]
    u = a * (0.5 * gc * (1.0 + lax.erf(gc * (2.0 ** -0.5))))
    ffn = jnp.dot(u.astype(BF16), wdn_ref[...], preferred_element_type=F32)
    o_ref[0] = x1[halo:rows - halo] + g2_ref[0] * ffn


def _halo_specs(tm, l, d, halo, nb=1):
    nblk = tm // halo
    last = l // halo - 1
    return [
        pl.BlockSpec((nb, halo, d), lambda bi, i: (bi, jnp.maximum(i * nblk - 1, 0), 0)),
        pl.BlockSpec((nb, tm, d), lambda bi, i: (bi, i, 0)),
        pl.BlockSpec((nb, halo, d), lambda bi, i: (bi, jnp.minimum((i + 1) * nblk, last), 0)),
    ]


def _packed_halo_specs(tm, l, d, n2):
    pitch = _pitch(n2)
    tp = tm // n2 * pitch
    last = l // n2 * pitch // 8 - 1
    return [
        pl.BlockSpec((1, 8, d), lambda bi, i: (bi // 2, jnp.maximum(i * (tp // 8) - 2, 0), 0)),
        pl.BlockSpec((1, tp, d), lambda bi, i: (bi // 2, i, 0)),
        pl.BlockSpec((1, 8, d), lambda bi, i: (bi // 2, jnp.minimum((i + 1) * (tp // 8), last), 0)),
    ]


def _post(mix, x, w_mix, b_mix, g1, norm2, sh2, sc2, g2, w_up, conv_w, conv_b, w_down, tm, packed_n2=0):
    b, l, d = x.shape
    d_ff = w_down.shape[0]
    mod = pl.BlockSpec((1, 1, d), lambda bi, i: (bi, 0, 0))
    halo = 8 if packed_n2 else HALO
    if packed_n2:
        mix_specs = _packed_halo_specs(tm, l, d, packed_n2) * 2
        mix_args = (mix[0],) * 3 + (mix[1],) * 3
    else:
        mix_specs = _halo_specs(tm, l, d, halo)
        mix_args = (mix,) * 3
    return pl.pallas_call(
        functools.partial(_post_kernel, packed_n2=packed_n2),
        out_shape=jax.ShapeDtypeStruct((b, l, d), F32),
        grid=(b, l // tm),
        in_specs=mix_specs + _halo_specs(tm, l, d, halo) + [
            _const_spec((d, d)), _const_spec((1, d)), mod, _const_spec((1, d)), mod, mod, mod,
            _const_spec((d, 2 * d_ff)), _const_spec((3, d_ff)), _const_spec((1, d_ff)), _const_spec((d_ff, d)),
        ],
        out_specs=pl.BlockSpec((1, tm, d), lambda bi, i: (bi, i, 0)),
        compiler_params=_params(("parallel", "parallel")),
        name="post_ffn",
    )(*mix_args, x, x, x, w_mix.astype(BF16), b_mix.reshape(1, d), g1, norm2.reshape(1, d), sh2, sc2, g2,
      w_up.astype(BF16), conv_w, conv_b.reshape(1, d_ff), w_down.astype(BF16))


def _hy_in_kernel(xp_ref, xm_ref, xn_ref, sh_ref, sc_ref, g_ref, w_ref, cw_ref, cb_ref, e0_ref, e2_ref, x0_ref, z_ref,
                  *, n2):
    halo = xp_ref.shape[1]
    tm = xm_ref.shape[1]
    rows = tm + 2 * halo
    d = xm_ref.shape[-1]
    pitch = _pitch(n2)
    i = pl.program_id(1)
    row = lax.broadcasted_iota(jnp.int32, (8, 1), 0)
    at_start = jnp.where((row == 0) & (i == 0), 1.0, 0.0).astype(F32)
    at_end = jnp.where((row == 7) & (i == pl.num_programs(1) - 1), 1.0, 0.0).astype(F32)
    x0s, zs = [], []
    for s in range(2):
        h = _rms_mod(_halo_rows(xp_ref, xm_ref, xn_ref, s), g_ref[...], sh_ref[s], sc_ref[s]).astype(BF16)
        parts = []
        for j in range(3):
            cols = slice(j * d, (j + 1) * d)
            u = _zero_outside(jnp.dot(h, w_ref[:, cols], preferred_element_type=F32), 1, halo)
            u_prev, u_next = _shift_rows(u, rows)
            cw = cw_ref[:, cols]
            uc = (u_prev * cw[0:1] + u * cw[1:2] + u_next * cw[2:3] + cb_ref[:, cols])[halo:rows - halo]
            parts.append(jnp.concatenate([uc[:8] - at_start * e0_ref[:, cols], uc[8:tm - 8],
                                          uc[tm - 8:] - at_end * e2_ref[:, cols]], axis=0))
        x0s.append(parts[0])
        zs.append(parts[2] * parts[1])
    pad = jnp.zeros((pitch - n2, d), U32)
    for ref, pair in ((x0_ref, x0s), (z_ref, zs)):
        words = _pack_pair(pair[0], pair[1])
        for j in range(tm // n2):
            ref[0, j * pitch:j * pitch + n2] = words[j * n2:(j + 1) * n2]
            ref[0, j * pitch + n2:(j + 1) * pitch] = pad


def _hy_in(x, shift, scale, gain, w_in, b_in, short_w, short_b, tm, n2):
    b, l, d = x.shape
    pitch = _pitch(n2)
    tp = tm // n2 * pitch
    mod = pl.BlockSpec((2, 1, d), lambda pi, i: (pi, 0, 0))
    out = pl.BlockSpec((1, tp, d), lambda pi, i: (pi, i, 0))
    shape = jax.ShapeDtypeStruct((b // 2, l // n2 * pitch, d), U32)
    return pl.pallas_call(
        functools.partial(_hy_in_kernel, n2=n2),
        out_shape=(shape, shape),
        grid=(b // 2, l // tm),
        in_specs=_halo_specs(tm, l, d, 8, 2) + [
            mod, mod, _const_spec((1, d)), _const_spec((d, 3 * d)), _const_spec((3, 3 * d)),
            _const_spec((1, 3 * d)), _const_spec((1, 3 * d)), _const_spec((1, 3 * d)),
        ],
        out_specs=(out, out),
        compiler_params=_params(("parallel", "parallel")),
        name="hyena_in",
    )(x, x, x, shift, scale, gain.reshape(1, d), w_in.astype(BF16), short_w,
      (short_b + b_in * short_w.sum(axis=0)).reshape(1, 3 * d), (b_in * short_w[0]).reshape(1, 3 * d),
      (b_in * short_w[2]).reshape(1, 3 * d))


FILTER_GROUPS = 8


def _filter_kernel(bands_ref, w1t_ref, w1c_ref, w1s_ref, b1_ref, w2_ref, b2_ref, w3_ref, b3_ref, wo_ref,
                   freq_ref, delta_ref, hf_ref, hb_ref, ss_ref, *, seq_len, n2):
    pitch = _pitch(n2)
    tl = hf_ref.shape[0] // pitch * n2
    d = hf_ref.shape[1]
    groups = FILTER_GROUPS
    rows = tl // groups
    width = b1_ref.shape[1] // groups
    base = pl.program_id(0) * tl

    def positions(lanes_per_group, n_lanes):
        lane = lax.broadcasted_iota(jnp.int32, (rows, n_lanes), 1)
        row = lax.broadcasted_iota(jnp.int32, (rows, n_lanes), 0)
        return (base + (lane // lanes_per_group) * rows + row).astype(F32)

    ang = bands_ref[...] * ((2.0 * math.pi / seq_len) * positions(HY_BANDS, groups * HY_BANDS))
    t_wide = positions(width, groups * width) * (1.0 / (seq_len - 1))
    freq = freq_ref[...]
    pre = t_wide * w1t_ref[...] + _dot3(jnp.cos(ang), w1c_ref[...]) - _dot3(jnp.sin(ang), w1s_ref[...])
    hdn = jnp.sin(freq * (pre + b1_ref[...]))
    hdn = jnp.sin(freq * (_dot3(hdn, w2_ref[...]) + b2_ref[...]))
    hdn = jnp.sin(freq * (_dot3(hdn, w3_ref[...]) + b3_ref[...]))

    pad = jnp.zeros((pitch - n2, d), F32)
    energy = jnp.zeros((1, d), F32)
    col = lax.broadcasted_iota(jnp.int32, (rows, 1), 0)
    for q in range(groups // 2):
        h2 = _dot3(hdn[:, 2 * q * width:(2 * q + 2) * width], wo_ref[...])
        for e in range(2):
            p = 2 * q + e
            pos = (base + p * rows + col).astype(F32)
            decay = jnp.exp(-(pos * (1.0 / (seq_len - 1))) * delta_ref[...])
            hf = h2[:, e * 2 * d:e * 2 * d + d] * decay
            hb = h2[:, e * 2 * d + d:(e + 1) * 2 * d] * decay
            energy = energy + jnp.sum(hf * hf + hb * hb, axis=0, keepdims=True)
            for ref, taps in ((hf_ref, hf), (hb_ref, jnp.where(pos > 0.0, hb, 0.0))):
                for j in range(rows // n2):
                    slab = p * (rows // n2) + j
                    ref[slab * pitch:slab * pitch + n2] = taps[j * n2:(j + 1) * n2]
                    ref[slab * pitch + n2:(slab + 1) * pitch] = pad

    @pl.when(pl.program_id(0) == 0)
    def _():
        ss_ref[...] = jnp.zeros_like(ss_ref)

    ss_ref[...] += energy


def _hyena_filter(seq_len, d, w1, b1, w2, b2, w3, b3, w_out, freq, tl, n2):
    groups = FILTER_GROUPS
    assert (tl // groups) % n2 == 0
    tp = tl // n2 * _pitch(n2)
    padded = jax.ShapeDtypeStruct((seq_len // n2 * _pitch(n2), d), F32)
    bands = np.linspace(1e-4, HY_BANDS - 1, HY_BANDS, dtype=np.float32)
    deltas = np.abs(np.linspace(math.log(HY_DECAY_TARGET) / HY_SHORT_DECAY_PCT,
                                math.log(HY_DECAY_TARGET) / HY_LONG_DECAY_PCT, d, dtype=np.float32)).reshape(1, d)
    eye = jnp.eye(groups, dtype=F32)
    tile = lambda v: jnp.tile(v.reshape(1, -1), (1, groups))
    operands = [
        jnp.asarray(np.tile(bands, groups).reshape(1, -1)), tile(w1[0]), jnp.kron(eye, w1[1:1 + HY_BANDS]),
        jnp.kron(eye, w1[1 + HY_BANDS:]), tile(b1), jnp.kron(eye, w2), tile(b2), jnp.kron(eye, w3), tile(b3),
        jnp.kron(jnp.eye(2, dtype=F32), w_out), tile(freq), jnp.asarray(deltas),
    ]
    return pl.pallas_call(
        functools.partial(_filter_kernel, seq_len=seq_len, n2=n2),
        out_shape=(padded, padded, jax.ShapeDtypeStruct((1, d), F32)),
        grid=(seq_len // tl,),
        in_specs=[pl.BlockSpec(o.shape, lambda i: (0, 0)) for o in operands],
        out_specs=(pl.BlockSpec((tp, d), lambda i: (i, 0)), pl.BlockSpec((tp, d), lambda i: (i, 0)),
                   pl.BlockSpec((1, d), lambda i: (0, 0))),
        compiler_params=_params(("arbitrary",)),
        name="hyena_filter",
    )(*operands)


def _dft_tables(n1):
    n = n1 * n1
    half = n1 // 2
    idx = np.arange(n1)
    ang = 2.0 * np.pi * np.outer(idx, idx) / n1
    c, s = np.cos(ang), np.sin(ang)
    fa_pair = np.block([[c[:, :half], s[:, :half]], [-s[:, :half], c[:, :half]]])
    zero = np.zeros((n1, half))
    fa_real = np.block([[c[:, :half], zero], [-s[:, :half], zero], [zero, c[:, :half]], [zero, -s[:, :half]]])
    fa_inv = np.block([[c[:half], -s[:half]], [s[:half], c[:half]]]) / n
    k1 = idx[:, None, None]
    k2 = idx[None, :, None]
    n2 = idx[None, None, :]
    m = (n2 * (k1 + n1 * k2)) % n
    gang = 2.0 * np.pi * m / n
    gr, gi = np.cos(gang), -np.sin(gang)
    g_fwd = np.concatenate([np.concatenate([gr, -gi], axis=2), np.concatenate([gi, gr], axis=2)], axis=1)
    to = lambda a: jnp.asarray(a.astype(np.float32), BF16)
    return to(fa_pair), to(fa_real), to(fa_inv), to(g_fwd)


def _filter_spectrum_kernel(hf_ref, hb_ref, ss_ref, db_ref, fa_ref, g_ref, o_ref, a_scr):
    n1 = fa_ref.shape[0] // 4
    n2 = g_ref.shape[1] // 2
    pitch = _pitch(n2)
    half = hf_ref.shape[0] // pitch
    kb = g_ref.shape[0]

    @pl.when(pl.program_id(1) == 0)
    def _():
        def body(j, carry):
            rows = jnp.concatenate([hf_ref[pl.ds(j, half, stride=pitch), :], hb_ref[pl.ds(j, half, stride=pitch), :]],
                                   axis=0)
            r = jnp.dot(fa_ref[...], rows.astype(BF16), preferred_element_type=F32)
            a_scr[0, pl.ds(j, n1, stride=pitch), :] = _pack_pair(r[:n1], r[n1:2 * n1])
            a_scr[1, pl.ds(j, n1, stride=pitch), :] = _pack_pair(r[2 * n1:3 * n1], r[3 * n1:])
            return carry

        lax.fori_loop(0, n2, body, 0, unroll=DFT_UNROLL)

    scale = lax.rsqrt(ss_ref[...] + NORM_EPS)
    for k in range(kb):
        row0 = pl.multiple_of((pl.program_id(1) * kb + k) * pitch, 8)
        fr, fi = _unpack_pair(a_scr[0, pl.ds(row0, n2), :])
        br, bi = _unpack_pair(a_scr[1, pl.ds(row0, n2), :])
        a = jnp.concatenate([jnp.concatenate([fr, fi], axis=0), jnp.concatenate([br, bi], axis=0)], axis=1)
        p = jnp.dot(g_ref[k], a.astype(BF16), preferred_element_type=F32)
        lanes = p.shape[1] // 2
        o_ref[k] = _pack_pair((p[:n2, :lanes] + p[:n2, lanes:]) * scale + db_ref[...],
                              (p[n2:, :lanes] - p[n2:, lanes:]) * scale)


def _filter_spectrum(fa_filt, g_fwd, hf, hb, energy, d_bias, kb):
    n1 = g_fwd.shape[0]
    n2 = g_fwd.shape[1] // 2
    lp, d = hf.shape
    tc = LANES
    half_spec = pl.BlockSpec((lp, tc), lambda c, k: (0, c))
    return pl.pallas_call(
        _filter_spectrum_kernel,
        out_shape=jax.ShapeDtypeStruct((n1, n2, d), U32),
        grid=(d // tc, n1 // kb),
        in_specs=[
            half_spec, half_spec, pl.BlockSpec((1, tc), lambda c, k: (0, c)), pl.BlockSpec((1, tc), lambda c, k: (0, c)),
            _const_spec(fa_filt.shape),
            pl.BlockSpec((kb, 2 * n2, 2 * n2), lambda c, k: (k, 0, 0)),
        ],
        out_specs=pl.BlockSpec((kb, n2, tc), lambda c, k: (k, 0, c)),
        scratch_shapes=[pltpu.VMEM((2, n1 * _pitch(n2), tc), U32)],
        compiler_params=_params(("parallel", "arbitrary")),
        name="filter_spectrum",
    )(hf, hb, energy, d_bias.reshape(1, d), fa_filt, g_fwd)


def _long_conv_kernel(z_ref, hf_ref, gf_ref, fa_ref, fi_ref, o_ref, a_scr):
    n1 = fa_ref.shape[0] // 2
    n2 = gf_ref.shape[1] // 2
    half = n1 // 2
    pitch = _pitch(n2)
    kb = gf_ref.shape[0]
    step = pl.program_id(2)

    @pl.when(step == 0)
    def _():
        def body(j, carry):
            zr, zi = _unpack_pair(z_ref[0, pl.ds(j, half, stride=pitch), :])
            rows = jnp.concatenate([zr, zi], axis=0).astype(BF16)
            r = jnp.dot(fa_ref[...], rows, preferred_element_type=F32)
            a_scr[pl.ds(j, n1, stride=pitch), :] = _pack_pair(r[:n1], r[n1:])
            return carry

        lax.fori_loop(0, n2, body, 0, unroll=DFT_UNROLL)

    for k in range(kb):
        row0 = pl.multiple_of((step * kb + k) * pitch, 8)
        ar, ai = _unpack_pair(a_scr[pl.ds(row0, n2), :])
        x = jnp.dot(gf_ref[k], jnp.concatenate([ar, ai], axis=0).astype(BF16), preferred_element_type=F32)
        xr, xi = x[:n2], x[n2:]
        hr, hi = _unpack_pair(hf_ref[k])
        y = jnp.concatenate([xr * hr - xi * hi, xr * hi + xi * hr], axis=0).astype(BF16)
        t = lax.dot_general(gf_ref[k], y, (((0,), (0,)), ((), ())), preferred_element_type=F32)
        a_scr[pl.ds(row0, n2), :] = _pack_pair(t[:n2], t[n2:])

    @pl.when(step == pl.num_programs(2) - 1)
    def _():
        def body(j, carry):
            tr, ti = _unpack_pair(a_scr[pl.ds(j, n1, stride=pitch), :])
            y = jnp.dot(fi_ref[...], jnp.concatenate([tr, ti], axis=0).astype(BF16), preferred_element_type=F32)
            o_ref[0, pl.ds(j, half, stride=pitch), :] = _pack_pair(y[:half], y[half:])
            return carry

        lax.fori_loop(0, n2, body, 0, unroll=DFT_UNROLL_OUT)
        pad = jnp.zeros((pitch - n2, o_ref.shape[2]), U32)
        for s in range(half):
            o_ref[0, s * pitch + n2:(s + 1) * pitch] = pad


def _long_conv(zp, hf, hb, energy, d_bias, l):
    p, lp, d = zp.shape
    n1 = math.isqrt(2 * l)
    assert n1 * n1 == 2 * l and lp == l // n1 * _pitch(n1)
    kb = min(64, n1)
    tc = LANES
    fa_pair, fa_real, fa_inv, g_fwd = _dft_tables(n1)
    spec = _filter_spectrum(fa_real, g_fwd, hf, hb, energy, d_bias, min(32, n1))
    seq = pl.BlockSpec((1, lp, tc), lambda c, pi, k: (pi, 0, c))
    g_blk = pl.BlockSpec((kb, 2 * n1, 2 * n1), lambda c, pi, k: (k, 0, 0))
    return pl.pallas_call(
        _long_conv_kernel,
        out_shape=jax.ShapeDtypeStruct((p, lp, d), U32),
        grid=(d // tc, p, n1 // kb),
        in_specs=[
            seq, pl.BlockSpec((kb, n1, tc), lambda c, pi, k: (k, 0, c)), g_blk,
            _const_spec(fa_pair.shape), _const_spec(fa_inv.shape),
        ],
        out_specs=seq,
        scratch_shapes=[pltpu.VMEM((n1 * _pitch(n1), tc), U32)],
        compiler_params=_params(("parallel", "parallel", "arbitrary")),
        name="long_conv",
    )(zp, spec, g_fwd, fa_pair, fa_inv)


def _mod_rows(mod, lo, hi, d):
    m = mod[lo:hi]
    return [m[:, None, j * d:(j + 1) * d] for j in range(N_MOD)]


def kernel(x, c, ctx, c_ctx, l0_w_mod, l0_b_mod, l0_norm1, l0_norm2, l0_na_w_qkv, l0_na_q_gain, l0_na_k_gain, l0_na_rpb, l0_na_w_o, l0_ffn_w_up, l0_ffn_conv_w, l0_ffn_conv_b, l0_ffn_w_down, l1_w_mod, l1_b_mod, l1_norm1, l1_norm2, l1_hy_w_in, l1_hy_b_in, l1_hy_short_w, l1_hy_short_b, l1_hy_f_w1, l1_hy_f_b1, l1_hy_f_w2, l1_hy_f_b2, l1_hy_f_w3, l1_hy_f_b3, l1_hy_f_wout, l1_hy_f_freq, l1_hy_d_bias, l1_hy_w_out, l1_hy_b_out, l1_ffn_w_up, l1_ffn_conv_w, l1_ffn_conv_b, l1_ffn_w_down):
    b, l, d = x.shape
    n_ctx = ctx.shape[1]
    n_rows = l // GRID_W
    assert n_rows >= K_ROWS and n_rows % (Q_ROWS * NA_SUB) == 0 and b % 2 == 0
    tm = min(512, l)
    tm_qkv = min(1024, l)
    n_dft = math.isqrt(2 * l)

    cond = jnp.zeros((8, d), F32).at[:b].set(c).at[b].set(c_ctx)

    mod = _adaln(cond, l0_w_mod, l0_b_mod)
    sh1, sc1, g1, sh2, sc2, g2 = _mod_rows(mod, 0, b, d)
    csh1, csc1 = _mod_rows(mod, b, b + 1, d)[:2]
    qkv = _qkv(x, sh1, sc1, l0_norm1, l0_na_w_qkv, l0_na_q_gain, l0_na_k_gain, tm_qkv)
    qkv_ctx = _qkv(ctx, csh1, csc1, l0_norm1, l0_na_w_qkv, l0_na_q_gain, l0_na_k_gain, n_ctx)
    bias = _block_bias(_col_bias(l0_na_rpb), n_rows)
    attn = _neighbourhood_attention(qkv, qkv_ctx, bias)
    x = _post(attn, x, l0_na_w_o, jnp.zeros((d,), F32), g1, l0_norm2, sh2, sc2, g2,
              l0_ffn_w_up, l0_ffn_conv_w, l0_ffn_conv_b, l0_ffn_w_down, tm)

    mod = _adaln(cond, l1_w_mod, l1_b_mod)
    sh1, sc1, g1, sh2, sc2, g2 = _mod_rows(mod, 0, b, d)
    x0p, zp = _hy_in(x, sh1, sc1, l1_norm1, l1_hy_w_in, l1_hy_b_in, l1_hy_short_w, l1_hy_short_b, tm, n_dft)
    hf, hb, energy = _hyena_filter(l, d, l1_hy_f_w1, l1_hy_f_b1, l1_hy_f_w2, l1_hy_f_b2, l1_hy_f_w3, l1_hy_f_b3,
                                   l1_hy_f_wout, l1_hy_f_freq, min(1024, l), n_dft)
    conv = _long_conv(zp, hf, hb, energy, l1_hy_d_bias, l)
    x = _post((conv, x0p), x, l1_hy_w_out, l1_hy_b_out, g1, l1_norm2, sh2, sc2, g2,
              l1_ffn_w_up, l1_ffn_conv_w, l1_ffn_conv_b, l1_ffn_w_down, tm, packed_n2=n_dft)
    return x
```

```python
import functools
import math

import numpy as np
import jax
import jax.numpy as jnp
from jax import lax
from jax.experimental import pallas as pl
from jax.experimental.pallas import tpu as pltpu

F32 = jnp.float32
BF16 = jnp.bfloat16
U32 = jnp.uint32
HIGHEST = lax.Precision.HIGHEST

NORM_EPS = 1e-6
N_MOD = 6
HEAD_DIM = 64
GRID_W = 64
WIN_H = 8
WIN_W = 16
HY_BANDS = 16
HY_DECAY_TARGET = 1e-2
HY_SHORT_DECAY_PCT = 0.3
HY_LONG_DECAY_PCT = 1.5

LANES = 128
HALO = 16
Q_ROWS = 4
K_ROWS = Q_ROWS + WIN_H
NA_SUB = 4
NA_CHUNK = 32
DFT_UNROLL = 16
DFT_UNROLL_OUT = 16
LOG2E = math.log2(math.e)
NEG = -1e30
VMEM_LIMIT = 56 * 1024 * 1024


def _params(sem):
    return pltpu.CompilerParams(dimension_semantics=sem, vmem_limit_bytes=VMEM_LIMIT)


def _const_spec(shape):
    return pl.BlockSpec(shape, lambda *_: (0,) * len(shape), pipeline_mode=pl.Buffered(1))


def _rms_mod(x, gain, shift, scale):
    ms = jnp.mean(x * x, axis=-1, keepdims=True)
    y = x * lax.rsqrt(ms + NORM_EPS) * gain
    return y * (1.0 + scale) + shift


def _split_bf16(x):
    hi = x.astype(BF16)
    return hi, (x - hi.astype(F32)).astype(BF16)


def _dot3(x, w):
    x_hi, x_lo = _split_bf16(x)
    w_hi, w_lo = _split_bf16(w)
    return jnp.dot(jnp.concatenate([x_hi, x_lo, x_hi], axis=1), jnp.concatenate([w_hi, w_hi, w_lo], axis=0),
                   preferred_element_type=F32)


def _adaln_kernel(c_ref, w_ref, b_ref, o_ref):
    c = c_ref[...]
    s = c / (1.0 + jnp.exp(-c))
    o_ref[...] = _dot3(s, w_ref[...]) + b_ref[...]


def _adaln(cond, w_mod, b_mod):
    rows, d = cond.shape
    n = w_mod.shape[1]
    tn = d
    return pl.pallas_call(
        _adaln_kernel,
        out_shape=jax.ShapeDtypeStruct((rows, n), F32),
        grid=(n // tn,),
        in_specs=[
            pl.BlockSpec((rows, d), lambda j: (0, 0)),
            pl.BlockSpec((d, tn), lambda j: (0, j)),
            pl.BlockSpec((1, tn), lambda j: (0, j)),
        ],
        out_specs=pl.BlockSpec((rows, tn), lambda j: (0, j)),
        compiler_params=_params(("arbitrary",)),
        name="adaln",
    )(cond, w_mod, b_mod.reshape(1, n))


def _qkv_kernel(x_ref, sh_ref, sc_ref, g_ref, w_ref, qg_ref, kg_ref, p1_ref, p2_ref, o_ref):
    d = x_ref.shape[-1]
    h = _rms_mod(x_ref[0], g_ref[...], sh_ref[0], sc_ref[0]).astype(BF16)
    for j, gain_ref in ((0, qg_ref), (1, kg_ref)):
        t = jnp.dot(h, w_ref[:, j * d:(j + 1) * d], preferred_element_type=F32)
        ms = jnp.dot((t * t).astype(BF16), p1_ref[...], preferred_element_type=F32)
        r = lax.rsqrt(ms + NORM_EPS)
        r_hi = r.astype(BF16)
        r_lo = (r - r_hi.astype(F32)).astype(BF16)
        rr = jnp.dot(jnp.concatenate([r_hi, r_lo], axis=-1), p2_ref[...], preferred_element_type=F32)
        o_ref[0, :, j * d:(j + 1) * d] = (t * rr * gain_ref[...]).astype(BF16)
    v = jnp.dot(h, w_ref[:, 2 * d:3 * d], preferred_element_type=F32)
    o_ref[0, :, 2 * d:3 * d] = v.astype(BF16)


def _qkv(x, shift, scale, gain, w_qkv, q_gain, k_gain, tm):
    b, l, d = x.shape
    heads = d // HEAD_DIM
    per_batch = shift.shape[0] == b
    mod_map = (lambda bi, i: (bi, 0, 0)) if per_batch else (lambda bi, i: (0, 0, 0))
    p1 = np.zeros((d, LANES), np.float32)
    p1[np.arange(d), np.arange(d) // HEAD_DIM] = 1.0 / HEAD_DIM
    p2 = np.zeros((2 * LANES, d), np.float32)
    p2[np.arange(d) // HEAD_DIM, np.arange(d)] = 1.0
    p2[LANES + np.arange(d) // HEAD_DIM, np.arange(d)] = 1.0
    qg = (jnp.tile(q_gain, heads) * (HEAD_DIM ** -0.5 * LOG2E)).reshape(1, d)
    kg = jnp.tile(k_gain, heads).reshape(1, d)
    return pl.pallas_call(
        _qkv_kernel,
        out_shape=jax.ShapeDtypeStruct((b, l, 3 * d), BF16),
        grid=(b, l // tm),
        in_specs=[
            pl.BlockSpec((1, tm, d), lambda bi, i: (bi, i, 0)),
            pl.BlockSpec((1, 1, d), mod_map),
            pl.BlockSpec((1, 1, d), mod_map),
            _const_spec((1, d)),
            _const_spec((d, 3 * d)),
            _const_spec((1, d)),
            _const_spec((1, d)),
            _const_spec((d, LANES)),
            _const_spec((2 * LANES, d)),
        ],
        out_specs=pl.BlockSpec((1, tm, 3 * d), lambda bi, i: (bi, i, 0)),
        compiler_params=_params(("parallel", "parallel")),
        name="qkv",
    )(x, shift, scale, gain.reshape(1, d), w_qkv.astype(BF16), qg, kg,
      jnp.asarray(p1, BF16), jnp.asarray(p2, BF16))


def _toeplitz_kernel(r_ref, oh_ref, m_ref, o_ref):
    o_ref[...] = jnp.dot(r_ref[...], oh_ref[...], preferred_element_type=F32, precision=HIGHEST) + m_ref[...]


def _col_bias(rpb):
    heads, n_dr, n_dc = rpb.shape
    k_pad = 32
    qc = np.arange(GRID_W)[:, None]
    kc = np.arange(GRID_W)[None, :]
    c0 = np.clip(qc - WIN_W // 2, 0, GRID_W - WIN_W)
    valid = (kc >= c0) & (kc < c0 + WIN_W)
    dc = kc - qc + (WIN_W - 1)
    onehot = np.zeros((k_pad, GRID_W, GRID_W), np.float32)
    for j in range(n_dc):
        onehot[j] = ((dc == j) & valid).astype(np.float32)
    onehot = onehot.reshape(k_pad, GRID_W * GRID_W)
    mask = np.where(valid, 0.0, NEG).astype(np.float32).reshape(1, GRID_W * GRID_W)
    rows = heads * n_dr
    rows_pad = -(-rows // 8) * 8
    r2 = jnp.zeros((rows_pad, k_pad), F32).at[:rows, :n_dc].set(rpb.reshape(rows, n_dc))
    t = pl.pallas_call(
        _toeplitz_kernel,
        out_shape=jax.ShapeDtypeStruct((rows_pad, GRID_W * GRID_W), F32),
        name="rpb_toeplitz",
    )(r2, jnp.asarray(onehot), jnp.asarray(mask))
    return t[:rows].reshape(heads, n_dr, GRID_W, GRID_W)


def _block_bias(t, n_rows):
    heads = t.shape[0]
    kh = min(WIN_H, n_rows)
    masked = jnp.full((heads, GRID_W, GRID_W), NEG, F32)
    classes = []
    for q0, k0 in ((0, 0), (Q_ROWS, Q_ROWS - WIN_H // 2), (n_rows - Q_ROWS, n_rows - K_ROWS)):
        strips = []
        for qr in range(Q_ROWS):
            r = q0 + qr
            r0 = min(max(r - kh // 2, 0), n_rows - kh)
            blocks = []
            for kr in range(K_ROWS):
                kk = k0 + kr
                blocks.append(t[:, kk - r + WIN_H - 1] if r0 <= kk < r0 + kh else masked)
            strips.append(jnp.concatenate(blocks, axis=-1))
        classes.append(jnp.concatenate(strips, axis=-2))
    return (jnp.stack(classes) * LOG2E).astype(BF16)


def _na_stages(q_ref, k_ref, v_ref, kc_ref, vc_ref, bias_ref, o_ref, k_scr, v_scr, p_scr, s_new, s_old,
               *, sub_new, sub_old, n_sub):
    tq = q_ref.shape[1] // NA_SUB
    tk = bias_ref.shape[-1]
    n_ctx = kc_ref.shape[1]
    l = k_ref.shape[1]
    first_head = lax.broadcasted_iota(jnp.int32, (1, LANES), 1) < HEAD_DIM

    def window_start(sub):
        return pl.multiple_of(jnp.clip(sub * tq - (tk - tq) // 2, 0, l - tk), tq)

    for j in range(NA_SUB):
        k_scr[j, :tk] = k_ref[0, pl.ds(window_start(sub_new + j), tk), :]
        k_scr[j, tk:] = kc_ref[0]
        q = q_ref[0, j * tq:(j + 1) * tq]
        zero = jnp.zeros_like(q)
        q2 = jnp.concatenate([jnp.where(first_head, q, zero), jnp.where(first_head, zero, q)], axis=0)
        s_new[j] = lax.dot_general(q2, k_scr[j], (((1,), (1,)), ((), ())), preferred_element_type=F32)

    for j in range(NA_SUB):
        sub = sub_old + j
        v_scr[j, :tk, :LANES] = v_ref[0, pl.ds(window_start(sub), tk), :]
        v_scr[j, tk:, :LANES] = vc_ref[0]
        v_scr[j, :, LANES:] = jnp.ones((tk + n_ctx, LANES), BF16)
        edge = jnp.where(sub == 0, 0, jnp.where(sub == n_sub - 1, 2, 1))
        heads = []
        for h in range(2):
            for r in range(0, tq, NA_CHUNK):
                rows = slice(h * tq + r, h * tq + r + NA_CHUNK)
                s_loc = s_old[j, rows, :tk] + bias_ref[edge, h, r:r + NA_CHUNK, :].astype(F32)
                s_ctx = s_old[j, rows, tk:]
                m = jnp.maximum(jnp.max(s_loc, axis=-1, keepdims=True), jnp.max(s_ctx, axis=-1, keepdims=True))
                p_scr[j, rows, :tk] = jnp.exp2((s_loc - m).astype(BF16))
                p_scr[j, rows, tk:] = jnp.exp2((s_ctx - m).astype(BF16))
            o = jnp.dot(p_scr[j, h * tq:(h + 1) * tq], v_scr[j], preferred_element_type=F32)
            heads.append(o[:, :LANES] / o[:, LANES:])
        o_ref[0, j * tq:(j + 1) * tq] = jnp.where(first_head, heads[0], heads[1]).astype(BF16)


def _na_kernel(q_ref, k_ref, v_ref, kc_ref, vc_ref, bias_ref, o_ref, k_scr, v_scr, p_scr, s0, s1, *, nb):
    t = pl.program_id(0)
    last = pl.num_programs(0) - 2
    subs = dict(sub_new=(jnp.minimum(t, last) % nb) * NA_SUB, sub_old=(jnp.clip(t - 1, 0, last) % nb) * NA_SUB,
                n_sub=nb * NA_SUB)
    refs = (q_ref, k_ref, v_ref, kc_ref, vc_ref, bias_ref, o_ref, k_scr, v_scr, p_scr)

    @pl.when(t == 0)
    def _():
        s1[...] = jnp.zeros(s1.shape, F32)

    @pl.when(t % 2 == 0)
    def _():
        _na_stages(*refs, s0, s1, **subs)

    @pl.when(t % 2 == 1)
    def _():
        _na_stages(*refs, s1, s0, **subs)


def _neighbourhood_attention(qkv, qkv_ctx, bias):
    b, l, d3 = qkv.shape
    d = d3 // 3
    n_ctx = qkv_ctx.shape[1]
    pairs = d // LANES
    tq, tk = bias.shape[-2:]
    tb = NA_SUB * tq
    nb = l // tb
    steps = b * pairs * nb

    def decode(t):
        return t // (pairs * nb), (t // nb) % pairs, t % nb

    def lagged(t, lag):
        return decode(jnp.clip(t - lag, 0, steps - 1))

    def q_map(t):
        bi, hp, i = lagged(t, 0)
        return bi, i, hp

    def kv_map(which, lag):
        def index(t):
            bi, hp, _ = lagged(t, lag)
            return bi, 0, which * pairs + hp
        return index

    def bias_map(t):
        return 0, lagged(t, 1)[1], 0, 0

    def out_map(t):
        bi, hp, i = lagged(t, 1)
        return bi, i, hp

    n_keys = tk + n_ctx
    return pl.pallas_call(
        functools.partial(_na_kernel, nb=nb),
        out_shape=jax.ShapeDtypeStruct((b, l, d), BF16),
        grid=(steps + 1,),
        in_specs=[
            pl.BlockSpec((1, tb, LANES), q_map),
            pl.BlockSpec((1, l, LANES), kv_map(1, 0)),
            pl.BlockSpec((1, l, LANES), kv_map(2, 1)),
            pl.BlockSpec((1, n_ctx, LANES), kv_map(1, 0)),
            pl.BlockSpec((1, n_ctx, LANES), kv_map(2, 1)),
            pl.BlockSpec((3, 2, tq, tk), bias_map),
        ],
        out_specs=pl.BlockSpec((1, tb, LANES), out_map),
        scratch_shapes=[
            pltpu.VMEM((NA_SUB, n_keys, LANES), BF16),
            pltpu.VMEM((NA_SUB, n_keys, 2 * LANES), BF16),
            pltpu.VMEM((NA_SUB, 2 * tq, n_keys), BF16),
            pltpu.VMEM((NA_SUB, 2 * tq, n_keys), F32), pltpu.VMEM((NA_SUB, 2 * tq, n_keys), F32),
        ],
        compiler_params=_params(("arbitrary",)),
        name="na_attention",
    )(qkv, qkv, qkv, qkv_ctx, qkv_ctx, bias)


def _halo_rows(prev_ref, main_ref, next_ref, s=0):
    return jnp.concatenate([prev_ref[s], main_ref[s], next_ref[s]], axis=0)


def _zero_outside(u, axis, halo):
    i = pl.program_id(axis)
    rows = u.shape[0]
    keep_top = jnp.where(i == 0, 0.0, 1.0).astype(F32)
    keep_bottom = jnp.where(i == pl.num_programs(axis) - 1, 0.0, 1.0).astype(F32)
    return jnp.concatenate([u[:halo] * keep_top, u[halo:rows - halo], u[rows - halo:] * keep_bottom], axis=0)


def _shift_rows(g, rows):
    return pltpu.roll(g, 1, 0), pltpu.roll(g, rows - 1, 0)


def _pack_pair(hi, lo):
    hi_bits = pltpu.bitcast(hi.astype(BF16).astype(F32), U32)
    lo_bits = pltpu.bitcast(lo.astype(BF16).astype(F32), U32)
    return hi_bits | (lo_bits >> 16)


def _unpack_pair(word):
    return (pltpu.bitcast(word & jnp.uint32(0xFFFF0000), F32), pltpu.bitcast(word << 16, F32))


def _pitch(n2):
    return n2 + 8


def _packed_rows(prev_ref, main_ref, next_ref, n2):
    pitch = _pitch(n2)
    slabs = main_ref.shape[1] // pitch
    words = jnp.concatenate([prev_ref[0]] + [main_ref[0, j * pitch:j * pitch + n2] for j in range(slabs)]
                            + [next_ref[0]], axis=0)
    shift = ((pl.program_id(0) % 2) * 16).astype(U32)
    return pltpu.bitcast((words << shift) & jnp.uint32(0xFFFF0000), F32)


def _post_kernel(*refs, packed_n2):
    n_mix = 6 if packed_n2 else 3
    mix_refs, refs = refs[:n_mix], refs[n_mix:]
    (xp_ref, xm_ref, xn_ref, wmix_ref, bmix_ref, g1_ref, n2_ref, sh_ref, sc_ref, g2_ref, wup_ref, cw_ref, cb_ref,
     wdn_ref, o_ref) = refs
    halo = xp_ref.shape[1]
    rows = xm_ref.shape[1] + 2 * halo
    d_ff = wdn_ref.shape[0]
    if packed_n2:
        mix = (_packed_rows(*mix_refs[:3], packed_n2) * _packed_rows(*mix_refs[3:], packed_n2)).astype(BF16)
    else:
        prev_ref, main_ref, next_ref = mix_refs
        mix = jnp.concatenate([prev_ref[0].astype(F32)[HALO - halo:], main_ref[0].astype(F32),
                               next_ref[0].astype(F32)[:halo]], axis=0).astype(BF16)
    x = _halo_rows(xp_ref, xm_ref, xn_ref)
    y = jnp.dot(mix, wmix_ref[...], preferred_element_type=F32) + bmix_ref[...]
    x1 = x + g1_ref[0] * y
    h32 = _rms_mod(x1, n2_ref[...], sh_ref[0], sc_ref[0])
    h = h32.astype(BF16)
    a = jnp.dot(h32[halo:rows - halo].astype(BF16), wup_ref[:, :d_ff], preferred_element_type=F32)
    g = _zero_outside(jnp.dot(h, wup_ref[:, d_ff:], preferred_element_type=F32), 1, halo)
    g_prev, g_next = _shift_rows(g, rows)
    gc = (g_prev * cw_ref[0:1] + g * cw_ref[1:2] + g_next * cw_ref[2:3] + cb_ref[...])[halo:rows - halo]
    u = a * (0.5 * gc * (1.0 + lax.erf(gc * (2.0 ** -0.5))))
    ffn = jnp.dot(u.astype(BF16), wdn_ref[...], preferred_element_type=F32)
    o_ref[0] = x1[halo:rows - halo] + g2_ref[0] * ffn


def _halo_specs(tm, l, d, halo, nb=1):
    nblk = tm // halo
    last = l // halo - 1
    return [
        pl.BlockSpec((nb, halo, d), lambda bi, i: (bi, jnp.maximum(i * nblk - 1, 0), 0)),
        pl.BlockSpec((nb, tm, d), lambda bi, i: (bi, i, 0)),
        pl.BlockSpec((nb, halo, d), lambda bi, i: (bi, jnp.minimum((i + 1) * nblk, last), 0)),
    ]


def _packed_halo_specs(tm, l, d, n2):
    pitch = _pitch(n2)
    tp = tm // n2 * pitch
    last = l // n2 * pitch // 8 - 1
    return [
        pl.BlockSpec((1, 8, d), lambda bi, i: (bi // 2, jnp.maximum(i * (tp // 8) - 2, 0), 0)),
        pl.BlockSpec((1, tp, d), lambda bi, i: (bi // 2, i, 0)),
        pl.BlockSpec((1, 8, d), lambda bi, i: (bi // 2, jnp.minimum((i + 1) * (tp // 8), last), 0)),
    ]


def _post(mix, x, w_mix, b_mix, g1, norm2, sh2, sc2, g2, w_up, conv_w, conv_b, w_down, tm, packed_n2=0):
    b, l, d = x.shape
    d_ff = w_down.shape[0]
    mod = pl.BlockSpec((1, 1, d), lambda bi, i: (bi, 0, 0))
    halo = 8
    if packed_n2:
        mix_specs = _packed_halo_specs(tm, l, d, packed_n2) * 2
        mix_args = (mix[0],) * 3 + (mix[1],) * 3
    else:
        mix_specs = _halo_specs(tm, l, d, HALO)
        mix_args = (mix,) * 3
    return pl.pallas_call(
        functools.partial(_post_kernel, packed_n2=packed_n2),
        out_shape=jax.ShapeDtypeStruct((b, l, d), F32),
        grid=(b, l // tm),
        in_specs=mix_specs + _halo_specs(tm, l, d, halo) + [
            _const_spec((d, d)), _const_spec((1, d)), mod, _const_spec((1, d)), mod, mod, mod,
            _const_spec((d, 2 * d_ff)), _const_spec((3, d_ff)), _const_spec((1, d_ff)), _const_spec((d_ff, d)),
        ],
        out_specs=pl.BlockSpec((1, tm, d), lambda bi, i: (bi, i, 0)),
        compiler_params=_params(("parallel", "parallel")),
        name="post_ffn",
    )(*mix_args, x, x, x, w_mix.astype(BF16), b_mix.reshape(1, d), g1, norm2.reshape(1, d), sh2, sc2, g2,
      w_up.astype(BF16), conv_w, conv_b.reshape(1, d_ff), w_down.astype(BF16))


def _hy_in_kernel(xp_ref, xm_ref, xn_ref, sh_ref, sc_ref, g_ref, w_ref, cw_ref, cb_ref, e0_ref, e2_ref, x0_ref, z_ref,
                  *, n2):
    halo = xp_ref.shape[1]
    tm = xm_ref.shape[1]
    rows = tm + 2 * halo
    d = xm_ref.shape[-1]
    pitch = _pitch(n2)
    i = pl.program_id(1)
    row = lax.broadcasted_iota(jnp.int32, (8, 1), 0)
    at_start = jnp.where((row == 0) & (i == 0), 1.0, 0.0).astype(F32)
    at_end = jnp.where((row == 7) & (i == pl.num_programs(1) - 1), 1.0, 0.0).astype(F32)
    x0s, zs = [], []
    for s in range(2):
        h = _rms_mod(_halo_rows(xp_ref, xm_ref, xn_ref, s), g_ref[...], sh_ref[s], sc_ref[s]).astype(BF16)
        parts = []
        for j in range(3):
            cols = slice(j * d, (j + 1) * d)
            u = _zero_outside(jnp.dot(h, w_ref[:, cols], preferred_element_type=F32), 1, halo)
            u_prev, u_next = _shift_rows(u, rows)
            cw = cw_ref[:, cols]
            uc = (u_prev * cw[0:1] + u * cw[1:2] + u_next * cw[2:3] + cb_ref[:, cols])[halo:rows - halo]
            parts.append(jnp.concatenate([uc[:8] - at_start * e0_ref[:, cols], uc[8:tm - 8],
                                          uc[tm - 8:] - at_end * e2_ref[:, cols]], axis=0))
        x0s.append(parts[0])
        zs.append(parts[2] * parts[1])
    pad = jnp.zeros((pitch - n2, d), U32)
    for ref, pair in ((x0_ref, x0s), (z_ref, zs)):
        words = _pack_pair(pair[0], pair[1])
        for j in range(tm // n2):
            ref[0, j * pitch:j * pitch + n2] = words[j * n2:(j + 1) * n2]
            ref[0, j * pitch + n2:(j + 1) * pitch] = pad


def _hy_in(x, shift, scale, gain, w_in, b_in, short_w, short_b, tm, n2):
    b, l, d = x.shape
    pitch = _pitch(n2)
    tp = tm // n2 * pitch
    mod = pl.BlockSpec((2, 1, d), lambda pi, i: (pi, 0, 0))
    out = pl.BlockSpec((1, tp, d), lambda pi, i: (pi, i, 0))
    shape = jax.ShapeDtypeStruct((b // 2, l // n2 * pitch, d), U32)
    return pl.pallas_call(
        functools.partial(_hy_in_kernel, n2=n2),
        out_shape=(shape, shape),
        grid=(b // 2, l // tm),
        in_specs=_halo_specs(tm, l, d, 8, 2) + [
            mod, mod, _const_spec((1, d)), _const_spec((d, 3 * d)), _const_spec((3, 3 * d)),
            _const_spec((1, 3 * d)), _const_spec((1, 3 * d)), _const_spec((1, 3 * d)),
        ],
        out_specs=(out, out),
        compiler_params=_params(("parallel", "parallel")),
        name="hyena_in",
    )(x, x, x, shift, scale, gain.reshape(1, d), w_in.astype(BF16), short_w,
      (short_b + b_in * short_w.sum(axis=0)).reshape(1, 3 * d), (b_in * short_w[0]).reshape(1, 3 * d),
      (b_in * short_w[2]).reshape(1, 3 * d))


FILTER_GROUPS = 8


def _filter_kernel(bands_ref, w1t_ref, w1c_ref, w1s_ref, b1_ref, w2_ref, b2_ref, w3_ref, b3_ref, wo_ref,
                   freq_ref, delta_ref, hf_ref, hb_ref, ss_ref, *, seq_len, n2):
    pitch = _pitch(n2)
    tl = hf_ref.shape[0] // pitch * n2
    d = hf_ref.shape[1]
    groups = FILTER_GROUPS
    rows = tl // groups
    width = b1_ref.shape[1] // groups
    base = pl.program_id(0) * tl

    def positions(lanes_per_group, n_lanes):
        lane = lax.broadcasted_iota(jnp.int32, (rows, n_lanes), 1)
        row = lax.broadcasted_iota(jnp.int32, (rows, n_lanes), 0)
        return (base + (lane // lanes_per_group) * rows + row).astype(F32)

    ang = bands_ref[...] * ((2.0 * math.pi / seq_len) * positions(HY_BANDS, groups * HY_BANDS))
    t_wide = positions(width, groups * width) * (1.0 / (seq_len - 1))
    freq = freq_ref[...]
    pre = t_wide * w1t_ref[...] + _dot3(jnp.cos(ang), w1c_ref[...]) - _dot3(jnp.sin(ang), w1s_ref[...])
    hdn = jnp.sin(freq * (pre + b1_ref[...]))
    hdn = jnp.sin(freq * (_dot3(hdn, w2_ref[...]) + b2_ref[...]))
    hdn = jnp.sin(freq * (_dot3(hdn, w3_ref[...]) + b3_ref[...]))

    pad = jnp.zeros((pitch - n2, d), F32)
    energy = jnp.zeros((1, d), F32)
    col = lax.broadcasted_iota(jnp.int32, (rows, 1), 0)
    for q in range(groups // 2):
        h2 = _dot3(hdn[:, 2 * q * width:(2 * q + 2) * width], wo_ref[...])
        for e in range(2):
            p = 2 * q + e
            pos = (base + p * rows + col).astype(F32)
            decay = jnp.exp(-(pos * (1.0 / (seq_len - 1))) * delta_ref[...])
            hf = h2[:, e * 2 * d:e * 2 * d + d] * decay
            hb = h2[:, e * 2 * d + d:(e + 1) * 2 * d] * decay
            energy = energy + jnp.sum(hf * hf + hb * hb, axis=0, keepdims=True)
            for ref, taps in ((hf_ref, hf), (hb_ref, jnp.where(pos > 0.0, hb, 0.0))):
                for j in range(rows // n2):
                    slab = p * (rows // n2) + j
                    ref[slab * pitch:slab * pitch + n2] = taps[j * n2:(j + 1) * n2]
                    ref[slab * pitch + n2:(slab + 1) * pitch] = pad

    @pl.when(pl.program_id(0) == 0)
    def _():
        ss_ref[...] = jnp.zeros_like(ss_ref)

    ss_ref[...] += energy


def _hyena_filter(seq_len, d, w1, b1, w2, b2, w3, b3, w_out, freq, tl, n2):
    groups = FILTER_GROUPS
    assert (tl // groups) % n2 == 0
    tp = tl // n2 * _pitch(n2)
    padded = jax.ShapeDtypeStruct((seq_len // n2 * _pitch(n2), d), F32)
    bands = np.linspace(1e-4, HY_BANDS - 1, HY_BANDS, dtype=np.float32)
    deltas = np.abs(np.linspace(math.log(HY_DECAY_TARGET) / HY_SHORT_DECAY_PCT,
                                math.log(HY_DECAY_TARGET) / HY_LONG_DECAY_PCT, d, dtype=np.float32)).reshape(1, d)
    eye = jnp.eye(groups, dtype=F32)
    tile = lambda v: jnp.tile(v.reshape(1, -1), (1, groups))
    operands = [
        jnp.asarray(np.tile(bands, groups).reshape(1, -1)), tile(w1[0]), jnp.kron(eye, w1[1:1 + HY_BANDS]),
        jnp.kron(eye, w1[1 + HY_BANDS:]), tile(b1), jnp.kron(eye, w2), tile(b2), jnp.kron(eye, w3), tile(b3),
        jnp.kron(jnp.eye(2, dtype=F32), w_out), tile(freq), jnp.asarray(deltas),
    ]
    return pl.pallas_call(
        functools.partial(_filter_kernel, seq_len=seq_len, n2=n2),
        out_shape=(padded, padded, jax.ShapeDtypeStruct((1, d), F32)),
        grid=(seq_len // tl,),
        in_specs=[pl.BlockSpec(o.shape, lambda i: (0, 0)) for o in operands],
        out_specs=(pl.BlockSpec((tp, d), lambda i: (i, 0)), pl.BlockSpec((tp, d), lambda i: (i, 0)),
                   pl.BlockSpec((1, d), lambda i: (0, 0))),
        compiler_params=_params(("arbitrary",)),
        name="hyena_filter",
    )(*operands)


def _dft_tables(n1):
    n = n1 * n1
    half = n1 // 2
    idx = np.arange(n1)
    ang = 2.0 * np.pi * np.outer(idx, idx) / n1
    c, s = np.cos(ang), np.sin(ang)
    fa_pair = np.block([[c[:, :half], s[:, :half]], [-s[:, :half], c[:, :half]]])
    zero = np.zeros((n1, half))
    fa_real = np.block([[c[:, :half], zero], [-s[:, :half], zero], [zero, c[:, :half]], [zero, -s[:, :half]]])
    fa_inv = np.block([[c[:half], -s[:half]], [s[:half], c[:half]]]) / n
    k1 = idx[:, None, None]
    k2 = idx[None, :, None]
    n2 = idx[None, None, :]
    m = (n2 * (k1 + n1 * k2)) % n
    gang = 2.0 * np.pi * m / n
    gr, gi = np.cos(gang), -np.sin(gang)
    g_fwd = np.concatenate([np.concatenate([gr, -gi], axis=2), np.concatenate([gi, gr], axis=2)], axis=1)
    to = lambda a: jnp.asarray(a.astype(np.float32), BF16)
    return to(fa_pair), to(fa_real), to(fa_inv), to(g_fwd)


def _filter_spectrum_kernel(hf_ref, hb_ref, ss_ref, db_ref, fa_ref, g_ref, o_ref, a_scr):
    n1 = fa_ref.shape[0] // 4
    n2 = g_ref.shape[1] // 2
    pitch = _pitch(n2)
    half = hf_ref.shape[0] // pitch
    kb = g_ref.shape[0]

    @pl.when(pl.program_id(1) == 0)
    def _():
        def body(j, carry):
            rows = jnp.concatenate([hf_ref[pl.ds(j, half, stride=pitch), :], hb_ref[pl.ds(j, half, stride=pitch), :]],
                                   axis=0)
            r = jnp.dot(fa_ref[...], rows.astype(BF16), preferred_element_type=F32)
            a_scr[0, pl.ds(j, n1, stride=pitch), :] = _pack_pair(r[:n1], r[n1:2 * n1])
            a_scr[1, pl.ds(j, n1, stride=pitch), :] = _pack_pair(r[2 * n1:3 * n1], r[3 * n1:])
            return carry

        lax.fori_loop(0, n2, body, 0, unroll=DFT_UNROLL)

    scale = lax.rsqrt(ss_ref[...] + NORM_EPS)
    for k in range(kb):
        row0 = pl.multiple_of((pl.program_id(1) * kb + k) * pitch, 8)
        fr, fi = _unpack_pair(a_scr[0, pl.ds(row0, n2), :])
        br, bi = _unpack_pair(a_scr[1, pl.ds(row0, n2), :])
        a = jnp.concatenate([jnp.concatenate([fr, fi], axis=0), jnp.concatenate([br, bi], axis=0)], axis=1)
        p = jnp.dot(g_ref[k], a.astype(BF16), preferred_element_type=F32)
        lanes = p.shape[1] // 2
        o_ref[k] = _pack_pair((p[:n2, :lanes] + p[:n2, lanes:]) * scale + db_ref[...],
                              (p[n2:, :lanes] - p[n2:, lanes:]) * scale)


def _filter_spectrum(fa_filt, g_fwd, hf, hb, energy, d_bias, kb):
    n1 = g_fwd.shape[0]
    n2 = g_fwd.shape[1] // 2
    lp, d = hf.shape
    tc = LANES
    half_spec = pl.BlockSpec((lp, tc), lambda c, k: (0, c))
    return pl.pallas_call(
        _filter_spectrum_kernel,
        out_shape=jax.ShapeDtypeStruct((n1, n2, d), U32),
        grid=(d // tc, n1 // kb),
        in_specs=[
            half_spec, half_spec, pl.BlockSpec((1, tc), lambda c, k: (0, c)), pl.BlockSpec((1, tc), lambda c, k: (0, c)),
            _const_spec(fa_filt.shape),
            pl.BlockSpec((kb, 2 * n2, 2 * n2), lambda c, k: (k, 0, 0)),
        ],
        out_specs=pl.BlockSpec((kb, n2, tc), lambda c, k: (k, 0, c)),
        scratch_shapes=[pltpu.VMEM((2, n1 * _pitch(n2), tc), U32)],
        compiler_params=_params(("parallel", "arbitrary")),
        name="filter_spectrum",
    )(hf, hb, energy, d_bias.reshape(1, d), fa_filt, g_fwd)


def _long_conv_kernel(z_ref, hf_ref, gf_ref, fa_ref, fi_ref, o_ref, a_scr):
    n1 = fa_ref.shape[0] // 2
    n2 = gf_ref.shape[1] // 2
    half = n1 // 2
    pitch = _pitch(n2)
    kb = gf_ref.shape[0]
    step = pl.program_id(2)

    @pl.when(step == 0)
    def _():
        def body(j, carry):
            zr, zi = _unpack_pair(z_ref[0, pl.ds(j, half, stride=pitch), :])
            rows = jnp.concatenate([zr, zi], axis=0).astype(BF16)
            r = jnp.dot(fa_ref[...], rows, preferred_element_type=F32)
            a_scr[pl.ds(j, n1, stride=pitch), :] = _pack_pair(r[:n1], r[n1:])
            return carry

        lax.fori_loop(0, n2, body, 0, unroll=DFT_UNROLL)

    for k in range(kb):
        row0 = pl.multiple_of((step * kb + k) * pitch, 8)
        ar, ai = _unpack_pair(a_scr[pl.ds(row0, n2), :])
        x = jnp.dot(gf_ref[k], jnp.concatenate([ar, ai], axis=0).astype(BF16), preferred_element_type=F32)
        xr, xi = x[:n2], x[n2:]
        hr, hi = _unpack_pair(hf_ref[k])
        y = jnp.concatenate([xr * hr - xi * hi, xr * hi + xi * hr], axis=0).astype(BF16)
        t = lax.dot_general(gf_ref[k], y, (((0,), (0,)), ((), ())), preferred_element_type=F32)
        a_scr[pl.ds(row0, n2), :] = _pack_pair(t[:n2], t[n2:])

    @pl.when(step == pl.num_programs(2) - 1)
    def _():
        def body(j, carry):
            tr, ti = _unpack_pair(a_scr[pl.ds(j, n1, stride=pitch), :])
            y = jnp.dot(fi_ref[...], jnp.concatenate([tr, ti], axis=0).astype(BF16), preferred_element_type=F32)
            o_ref[0, pl.ds(j, half, stride=pitch), :] = _pack_pair(y[:half], y[half:])
            return carry

        lax.fori_loop(0, n2, body, 0, unroll=DFT_UNROLL_OUT)
        pad = jnp.zeros((pitch - n2, o_ref.shape[2]), U32)
        for s in range(half):
            o_ref[0, s * pitch + n2:(s + 1) * pitch] = pad


def _long_conv(zp, hf, hb, energy, d_bias, l):
    p, lp, d = zp.shape
    n1 = math.isqrt(2 * l)
    assert n1 * n1 == 2 * l and lp == l // n1 * _pitch(n1)
    kb = min(64, n1)
    tc = LANES
    fa_pair, fa_real, fa_inv, g_fwd = _dft_tables(n1)
    spec = _filter_spectrum(fa_real, g_fwd, hf, hb, energy, d_bias, min(32, n1))
    seq = pl.BlockSpec((1, lp, tc), lambda c, pi, k: (pi, 0, c))
    g_blk = pl.BlockSpec((kb, 2 * n1, 2 * n1), lambda c, pi, k: (k, 0, 0))
    return pl.pallas_call(
        _long_conv_kernel,
        out_shape=jax.ShapeDtypeStruct((p, lp, d), U32),
        grid=(d // tc, p, n1 // kb),
        in_specs=[
            seq, pl.BlockSpec((kb, n1, tc), lambda c, pi, k: (k, 0, c)), g_blk,
            _const_spec(fa_pair.shape), _const_spec(fa_inv.shape),
        ],
        out_specs=seq,
        scratch_shapes=[pltpu.VMEM((n1 * _pitch(n1), tc), U32)],
        compiler_params=_params(("parallel", "parallel", "arbitrary")),
        name="long_conv",
    )(zp, spec, g_fwd, fa_pair, fa_inv)


def _mod_rows(mod, lo, hi, d):
    m = mod[lo:hi]
    return [m[:, None, j * d:(j + 1) * d] for j in range(N_MOD)]


def kernel(x, c, ctx, c_ctx, l0_w_mod, l0_b_mod, l0_norm1, l0_norm2, l0_na_w_qkv, l0_na_q_gain, l0_na_k_gain, l0_na_rpb, l0_na_w_o, l0_ffn_w_up, l0_ffn_conv_w, l0_ffn_conv_b, l0_ffn_w_down, l1_w_mod, l1_b_mod, l1_norm1, l1_norm2, l1_hy_w_in, l1_hy_b_in, l1_hy_short_w, l1_hy_short_b, l1_hy_f_w1, l1_hy_f_b1, l1_hy_f_w2, l1_hy_f_b2, l1_hy_f_w3, l1_hy_f_b3, l1_hy_f_wout, l1_hy_f_freq, l1_hy_d_bias, l1_hy_w_out, l1_hy_b_out, l1_ffn_w_up, l1_ffn_conv_w, l1_ffn_conv_b, l1_ffn_w_down):
    b, l, d = x.shape
    n_ctx = ctx.shape[1]
    n_rows = l // GRID_W
    assert n_rows >= K_ROWS and n_rows % (Q_ROWS * NA_SUB) == 0 and b % 2 == 0
    tm = min(512, l)
    tm_qkv = min(1024, l)
    n_dft = math.isqrt(2 * l)

    cond = jnp.zeros((8, d), F32).at[:b].set(c).at[b].set(c_ctx)

    mod = _adaln(cond, l0_w_mod, l0_b_mod)
    sh1, sc1, g1, sh2, sc2, g2 = _mod_rows(mod, 0, b, d)
    csh1, csc1 = _mod_rows(mod, b, b + 1, d)[:2]
    qkv = _qkv(x, sh1, sc1, l0_norm1, l0_na_w_qkv, l0_na_q_gain, l0_na_k_gain, tm_qkv)
    qkv_ctx = _qkv(ctx, csh1, csc1, l0_norm1, l0_na_w_qkv, l0_na_q_gain, l0_na_k_gain, n_ctx)
    bias = _block_bias(_col_bias(l0_na_rpb), n_rows)
    attn = _neighbourhood_attention(qkv, qkv_ctx, bias)
    x = _post(attn, x, l0_na_w_o, jnp.zeros((d,), F32), g1, l0_norm2, sh2, sc2, g2,
              l0_ffn_w_up, l0_ffn_conv_w, l0_ffn_conv_b, l0_ffn_w_down, tm)

    mod = _adaln(cond, l1_w_mod, l1_b_mod)
    sh1, sc1, g1, sh2, sc2, g2 = _mod_rows(mod, 0, b, d)
    x0p, zp = _hy_in(x, sh1, sc1, l1_norm1, l1_hy_w_in, l1_hy_b_in, l1_hy_short_w, l1_hy_short_b, tm, n_dft)
    hf, hb, energy = _hyena_filter(l, d, l1_hy_f_w1, l1_hy_f_b1, l1_hy_f_w2, l1_hy_f_b2, l1_hy_f_w3, l1_hy_f_b3,
                                   l1_hy_f_wout, l1_hy_f_freq, min(1024, l), n_dft)
    conv = _long_conv(zp, hf, hb, energy, l1_hy_d_bias, l)
    x = _post((conv, x0p), x, l1_hy_w_out, l1_hy_b_out, g1, l1_norm2, sh2, sc2, g2,
              l1_ffn_w_up, l1_ffn_conv_w, l1_ffn_conv_b, l1_ffn_w_down, tm, packed_n2=n_dft)
    return x
```
